```python
import math
import jax, jax.numpy as jnp
from jax import lax
import numpy as np

D_MODEL = 1024
BATCH = 32
SEQ = 256
DEPTH = 4
DEC_BATCH = 4
DEC_SEQ = 1024
PAST_LEN = 256

GRID_W = 64
N_BRANCH = 4
BRANCH_WIDTH = 512
FFT_GROUPS = 4
FFT_GROUP_CH = 128
FFT_WIDTH = FFT_GROUPS * FFT_GROUP_CH
CONV_WIDTH = 512
CONV_K = 3
SSM_GROUPS = 32
SSM_CH = 16
SSM_WIDTH = SSM_GROUPS * SSM_CH
SSM_STATE = 64
N_HEADS = 8
N_KV = 2
Q_PER_KV = N_HEADS // N_KV
HEAD_DIM = 64
ATT_WIDTH = N_HEADS * HEAD_DIM
WINDOW = 128
ATT_BLOCK = 128
ROPE_BASE = 10000.0
IN_SIZES = (FFT_WIDTH, CONV_WIDTH, CONV_WIDTH, CONV_WIDTH, SSM_WIDTH, ATT_WIDTH,
            N_KV * HEAD_DIM, N_KV * HEAD_DIM, N_BRANCH * D_MODEL)
IN_COLS = sum(IN_SIZES)
PEER_HEADS = 8
N_KEYS = 128
N_EXPERTS = N_KEYS * N_KEYS
PEER_TOPK = 16
KEY_DIM = 256
PEER_CHUNK = 128
LN_EPS = 1e-5
DEEPNORM_ALPHA = (2 * DEPTH) ** 0.25
DEEPNORM_BETA = (8 * DEPTH) ** -0.25
NEG_INF = -1e30

kernel_name = "hybrid_diffusion_prefix_step"


def layer_norm(x, g, b):
    xf = x.astype(jnp.float32)
    mu = jnp.mean(xf, axis=-1, keepdims=True)
    var = jnp.mean(jnp.square(xf - mu), axis=-1, keepdims=True)
    y = (xf - mu) * lax.rsqrt(var + LN_EPS) * g.astype(jnp.float32) + b.astype(jnp.float32)
    return y.astype(x.dtype)


def modulation(cvec, w_ada, b_ada):
    mod = (jax.nn.silu(cvec) @ w_ada + b_ada)[:, None, :]
    return jnp.split(mod, 6, axis=-1)


def axial_rope(length):
    t = jnp.arange(length)
    row = (t // GRID_W).astype(jnp.float32)
    col = (t % GRID_W).astype(jnp.float32)
    n_freq = HEAD_DIM // 4
    inv = 1.0 / (ROPE_BASE ** (jnp.arange(n_freq, dtype=jnp.float32) / n_freq))
    ang = jnp.stack([row[:, None] * inv, col[:, None] * inv], axis=1)
    return jnp.cos(ang), jnp.sin(ang)


def apply_rope(x, cos, sin):
    xr = x.astype(jnp.float32).reshape(x.shape[:-1] + (2, 2, HEAD_DIM // 4))
    extra = x.ndim - 3
    c = cos.reshape((cos.shape[0],) + (1,) * extra + cos.shape[1:])
    s = sin.reshape((sin.shape[0],) + (1,) * extra + sin.shape[1:])
    x1, x2 = xr[..., 0, :], xr[..., 1, :]
    out = jnp.stack([x1 * c - x2 * s, x2 * c + x1 * s], axis=-2)
    return out.reshape(x.shape).astype(x.dtype)


def context_attention(q, k, v, sink):
    b, s_len = q.shape[:2]
    nq = s_len // ATT_BLOCK
    scale = HEAD_DIM ** -0.5
    qb = jnp.moveaxis(q.reshape(b, nq, ATT_BLOCK, N_KV, Q_PER_KV, HEAD_DIM), 1, 0)
    kf, vf = k.astype(jnp.float32), v.astype(jnp.float32)
    sink_b = sink.astype(jnp.float32).reshape(N_KV, Q_PER_KV)[None, :, :, None, None]

    def one_block(qblk):
        sc = jnp.einsum('bqhrd,bkhd->bhrqk', qblk.astype(jnp.float32), kf) * scale
        snk = jnp.broadcast_to(sink_b, sc.shape[:-1] + (1,))
        p = jax.nn.softmax(jnp.concatenate([sc, snk], axis=-1), axis=-1)[..., :s_len]
        return jnp.einsum('bhrqk,bkhd->bqhrd', p, vf)

    out = lax.map(one_block, qb)
    return jnp.moveaxis(out, 0, 1).reshape(b, s_len, ATT_WIDTH).astype(q.dtype)


def latent_attention(q, k, v, kc, vc, sink):
    b, length = q.shape[:2]
    nb = length // ATT_BLOCK
    lc = kc.shape[1]
    nw = 3 * ATT_BLOCK
    scale = HEAD_DIM ** -0.5
    qb = q.reshape(b, nb, ATT_BLOCK, N_KV, Q_PER_KV, HEAD_DIM).astype(jnp.float32)

    def windows(t):
        tp = jnp.pad(t, ((0, 0), (ATT_BLOCK, ATT_BLOCK), (0, 0), (0, 0)))
        tp = tp.reshape(b, nb + 2, ATT_BLOCK, N_KV, HEAD_DIM)
        return jnp.concatenate([tp[:, :nb], tp[:, 1:nb + 1], tp[:, 2:]], axis=2).astype(jnp.float32)

    kw, vw = windows(k), windows(v)
    qpos = jnp.arange(length).reshape(nb, ATT_BLOCK)
    kpos = (jnp.arange(nb)[:, None] - 1) * ATT_BLOCK + jnp.arange(nw)[None, :]
    valid = ((jnp.abs(qpos[:, :, None] - kpos[:, None, :]) <= WINDOW)
             & (kpos[:, None, :] >= 0) & (kpos[:, None, :] < length))
    s_loc = jnp.einsum('bnqhrd,bnkhd->bnhrqk', qb, kw) * scale
    s_loc = jnp.where(valid[None, :, None, None], s_loc, NEG_INF)
    s_ctx = jnp.einsum('bnqhrd,bchd->bnhrqc', qb, kc.astype(jnp.float32)) * scale
    sink_b = sink.astype(jnp.float32).reshape(N_KV, Q_PER_KV)[None, None, :, :, None, None]
    snk = jnp.broadcast_to(sink_b, s_loc.shape[:-1] + (1,))
    p = jax.nn.softmax(jnp.concatenate([s_loc, s_ctx, snk], axis=-1), axis=-1)
    out = (jnp.einsum('bnhrqk,bnkhd->bnqhrd', p[..., :nw], vw)
           + jnp.einsum('bnhrqc,bchd->bnqhrd', p[..., nw:nw + lc], vc.astype(jnp.float32)))
    return out.reshape(b, length, ATT_WIDTH).astype(q.dtype)


def _ssm_combine(e1, e2):
    a1, b1 = e1
    a2, b2 = e2
    return a2 * a1, a2 * b1 + b2


def s5_scan(u, lam_re, lam_im, log_step, b_re, b_im, c_re, c_im, h0, reverse):
    lam = lax.complex(lam_re.astype(jnp.float32), lam_im.astype(jnp.float32))
    dt = jnp.exp(log_step.astype(jnp.float32))[:, None]
    lam_bar = jnp.exp(lam * dt)
    b_bar = ((lam_bar - 1.0) / lam)[..., None] * lax.complex(b_re.astype(jnp.float32), b_im.astype(jnp.float32))
    bu = jnp.einsum('blgh,gph->blgp', u.astype(jnp.complex64), b_bar)
    if reverse:
        bu = jnp.flip(bu, axis=1)
    bu = bu.at[:, 0].add(lam_bar * h0)
    a = jnp.broadcast_to(lam_bar, bu.shape)
    _, h = lax.associative_scan(_ssm_combine, (a, bu), axis=1)
    h_last = h[:, -1]
    if reverse:
        h = jnp.flip(h, axis=1)
    c_mat = lax.complex(c_re.astype(jnp.float32), c_im.astype(jnp.float32))
    y = jnp.einsum('blgp,ghp->blgh', h, c_mat).real
    return y, h_last


def hybrid_mixer(xm, lp, ctx):
    b, length, _ = xm.shape
    z = xm @ lp['w_in']
    split_at = [int(i) for i in np.cumsum(IN_SIZES)[:-1]]
    zf, zb, zc, zh, zs, zq, zk, zv, zg = jnp.split(z, split_at, axis=-1)

    zf4 = zf.reshape(b, length, FFT_GROUPS, FFT_GROUP_CH).astype(jnp.float32)
    y_fft = jnp.fft.fft2(zf4, axes=(1, 3), norm='ortho').real.reshape(b, length, FFT_WIDTH).astype(xm.dtype)

    g = zc * zh
    conv = lax.conv_general_dilated(g, lp['conv_w'][:, None, :], window_strides=(1,),
                                    padding=((CONV_K // 2, CONV_K // 2),),
                                    dimension_numbers=('NWC', 'WIO', 'NWC'),
                                    feature_group_count=CONV_WIDTH)
    y_conv = zb * conv

    u = zs.reshape(b, length, SSM_GROUPS, SSM_CH).astype(jnp.float32)
    if ctx is None:
        h0f = jnp.zeros((b, SSM_GROUPS, SSM_STATE), jnp.complex64)
        h0b = h0f
    else:
        h0f, h0b = ctx[2], ctx[3]
    yf, hf = s5_scan(u, lp['lam_re'][0], lp['lam_im'][0], lp['log_step'][0], lp['b_re'][0], lp['b_im'][0],
                     lp['c_re'][0], lp['c_im'][0], h0f, False)
    yb, hb = s5_scan(u, lp['lam_re'][1], lp['lam_im'][1], lp['log_step'][1], lp['b_re'][1], lp['b_im'][1],
                     lp['c_re'][1], lp['c_im'][1], h0b, True)
    ys = (yf + yb).reshape(b, length, SSM_WIDTH) + lp['ssm_d'].astype(jnp.float32) * zs.astype(jnp.float32)
    ys = jax.nn.gelu(ys).astype(xm.dtype)
    y_ssm = ys * jax.nn.sigmoid(ys @ lp['w_glu'])

    q = zq.reshape(b, length, N_KV, Q_PER_KV, HEAD_DIM)
    k = zk.reshape(b, length, N_KV, HEAD_DIM)
    v = zv.reshape(b, length, N_KV, HEAD_DIM)
    if ctx is None:
        y_att = context_attention(q, k, v, lp['sink'])
    else:
        cos, sin = axial_rope(length)
        y_att = latent_attention(apply_rope(q, cos, sin), apply_rope(k, cos, sin), v, ctx[0], ctx[1], lp['sink'])

    branches = jnp.stack([y_fft, y_conv, y_ssm, y_att], axis=2)
    proj = jnp.einsum('blnc,ncd->blnd', branches, lp['w_branch'])
    gates = jax.nn.sigmoid(zg.reshape(b, length, N_BRANCH, D_MODEL))
    out = jnp.sum(gates * proj, axis=2) @ lp['w_out']
    return out, k, v, hf, hb


def peer_ffn(xm, wq, subkeys, u_tab, v_tab):
    b, length, d = xm.shape
    t = b * length
    xf = xm.reshape(t, d)
    q = (xf @ wq).reshape(t, PEER_HEADS, 2, KEY_DIM // 2).astype(jnp.float32)
    s = jnp.einsum('thcd,hcnd->thcn', q, subkeys.astype(jnp.float32))
    s1, i1 = lax.top_k(s[:, :, 0], PEER_TOPK)
    s2, i2 = lax.top_k(s[:, :, 1], PEER_TOPK)
    cand = (s1[..., :, None] + s2[..., None, :]).reshape(t, PEER_HEADS, PEER_TOPK * PEER_TOPK)
    cidx = (i1[..., :, None] * N_KEYS + i2[..., None, :]).reshape(t, PEER_HEADS, PEER_TOPK * PEER_TOPK)
    best, pos = lax.top_k(cand, PEER_TOPK)
    idx = jnp.take_along_axis(cidx, pos, axis=-1)
    gate = jax.nn.softmax(best, axis=-1).astype(xm.dtype)
    nchunk = t // PEER_CHUNK

    def chunk(args):
        xc, ic, gc = args
        hc = jax.nn.gelu(jnp.einsum('cd,chkd->chk', xc, u_tab[ic]))
        return jnp.einsum('chk,chkd->cd', gc * hc, v_tab[ic])

    out = lax.map(chunk, (xf.reshape(nchunk, PEER_CHUNK, d),
                          idx.reshape(nchunk, PEER_CHUNK, PEER_HEADS, PEER_TOPK),
                          gate.reshape(nchunk, PEER_CHUNK, PEER_HEADS, PEER_TOPK)))
    return out.reshape(b, length, d)


def trunk_layer(x, cvec, lp, ctx):
    sh1, sc1, g1, sh2, sc2, g2 = modulation(cvec, lp['w_ada'], lp['b_ada'])
    mix, k, v, hf, hb = hybrid_mixer(x * (1 + sc1) + sh1, lp, ctx)
    x = layer_norm(DEEPNORM_ALPHA * x + g1 * mix, lp['ln1_g'], lp['ln1_b'])
    ff = peer_ffn(x * (1 + sc2) + sh2, lp['peer_wq'], lp['peer_subkeys'], lp['peer_u'], lp['peer_v'])
    x = layer_norm(DEEPNORM_ALPHA * x + g2 * ff, lp['ln2_g'], lp['ln2_b'])
    return x, k, v, hf, hb


def setup_inputs(seed: int = 0) -> dict:
    key = jax.random.key(seed)
    ks = jax.random.split(key, 40)
    f32 = jnp.float32

    def nrm(k, shape, std):
        return jax.random.normal(k, shape, f32) * std

    lam_im_base = jnp.pi * jnp.arange(SSM_STATE, dtype=f32)
    return {
        "x_prompt": nrm(ks[0], (BATCH, SEQ, D_MODEL), 1.0),
        "x_sample": nrm(ks[1], (DEC_BATCH, DEC_SEQ, D_MODEL), 1.0),
        "cache_k": nrm(ks[2], (DEC_BATCH, DEPTH, PAST_LEN, N_KV, HEAD_DIM), 1.0),
        "cache_v": nrm(ks[3], (DEC_BATCH, DEPTH, PAST_LEN, N_KV, HEAD_DIM), 1.0),
        "state_ssm_re": nrm(ks[4], (DEC_BATCH, DEPTH, 2, SSM_GROUPS, SSM_STATE), 0.3),
        "state_ssm_im": nrm(ks[5], (DEC_BATCH, DEPTH, 2, SSM_GROUPS, SSM_STATE), 0.3),
        "c": nrm(ks[6], (DEC_BATCH, D_MODEL), 1.0),
        "c_ctx": nrm(ks[7], (D_MODEL,), 1.0),
        "w_ada": nrm(ks[8], (DEPTH, D_MODEL, 6 * D_MODEL), 0.5 * D_MODEL ** -0.5),
        "b_ada": nrm(ks[9], (DEPTH, 6 * D_MODEL), 0.01),
        "w_in": nrm(ks[10], (DEPTH, D_MODEL, IN_COLS), D_MODEL ** -0.5),
        "conv_w": nrm(ks[11], (DEPTH, CONV_K, CONV_WIDTH), CONV_K ** -0.5),
        "ssm_lam_re": -0.5 + nrm(ks[12], (DEPTH, 2, SSM_GROUPS, SSM_STATE), 0.01),
        "ssm_lam_im": lam_im_base + nrm(ks[13], (DEPTH, 2, SSM_GROUPS, SSM_STATE), 0.01),
        "ssm_log_step": jax.random.uniform(ks[14], (DEPTH, 2, SSM_GROUPS), f32,
                                           minval=math.log(1e-3), maxval=math.log(1e-1)),
        "ssm_b_re": nrm(ks[15], (DEPTH, 2, SSM_GROUPS, SSM_STATE, SSM_CH), (2 * SSM_CH) ** -0.5),
        "ssm_b_im": nrm(ks[16], (DEPTH, 2, SSM_GROUPS, SSM_STATE, SSM_CH), (2 * SSM_CH) ** -0.5),
        "ssm_c_re": nrm(ks[17], (DEPTH, 2, SSM_GROUPS, SSM_CH, SSM_STATE), (2 * SSM_STATE) ** -0.5),
        "ssm_c_im": nrm(ks[18], (DEPTH, 2, SSM_GROUPS, SSM_CH, SSM_STATE), (2 * SSM_STATE) ** -0.5),
        "ssm_d": nrm(ks[19], (DEPTH, SSM_WIDTH), 1.0),
        "ssm_w_glu": nrm(ks[20], (DEPTH, SSM_WIDTH, SSM_WIDTH), SSM_WIDTH ** -0.5),
        "attn_sink": nrm(ks[21], (DEPTH, N_HEADS), 0.5),
        "w_branch": nrm(ks[22], (DEPTH, N_BRANCH, BRANCH_WIDTH, D_MODEL), BRANCH_WIDTH ** -0.5),
        "w_out": nrm(ks[23], (DEPTH, D_MODEL, D_MODEL), DEEPNORM_BETA * D_MODEL ** -0.5),
        "ln1_g": 1.0 + nrm(ks[24], (DEPTH, D_MODEL), 0.01),
        "ln1_b": nrm(ks[25], (DEPTH, D_MODEL), 0.01),
        "ln2_g": 1.0 + nrm(ks[26], (DEPTH, D_MODEL), 0.01),
        "ln2_b": nrm(ks[27], (DEPTH, D_MODEL), 0.01),
        "peer_wq": nrm(ks[28], (DEPTH, D_MODEL, PEER_HEADS * KEY_DIM), D_MODEL ** -0.5),
        "peer_subkeys": nrm(ks[29], (DEPTH, PEER_HEADS, 2, N_KEYS, KEY_DIM // 2), (KEY_DIM // 2) ** -0.5),
        "peer_u": nrm(ks[30], (DEPTH, N_EXPERTS, D_MODEL), D_MODEL ** -0.5),
        "peer_v": nrm(ks[31], (DEPTH, N_EXPERTS, D_MODEL), DEEPNORM_BETA * PEER_HEADS ** -0.5),
    }


def reference(x_prompt, x_sample, cache_k, cache_v, state_ssm_re, state_ssm_im, c, c_ctx,
              w_ada, b_ada, w_in, conv_w, ssm_lam_re, ssm_lam_im, ssm_log_step,
              ssm_b_re, ssm_b_im, ssm_c_re, ssm_c_im, ssm_d, ssm_w_glu, attn_sink,
              w_branch, w_out, ln1_g, ln1_b, ln2_g, ln2_b,
              peer_wq, peer_subkeys, peer_u, peer_v):
    xp = x_prompt
    xs = x_sample
    cvec_ctx = c_ctx[None, :]
    new_k, new_v, new_re, new_im = [], [], [], []
    for l in range(DEPTH):
        lp = {
            'w_ada': w_ada[l], 'b_ada': b_ada[l], 'w_in': w_in[l], 'conv_w': conv_w[l],
            'lam_re': ssm_lam_re[l], 'lam_im': ssm_lam_im[l], 'log_step': ssm_log_step[l],
            'b_re': ssm_b_re[l], 'b_im': ssm_b_im[l], 'c_re': ssm_c_re[l], 'c_im': ssm_c_im[l],
            'ssm_d': ssm_d[l], 'w_glu': ssm_w_glu[l], 'sink': attn_sink[l],
            'w_branch': w_branch[l], 'w_out': w_out[l],
            'ln1_g': ln1_g[l], 'ln1_b': ln1_b[l], 'ln2_g': ln2_g[l], 'ln2_b': ln2_b[l],
            'peer_wq': peer_wq[l], 'peer_subkeys': peer_subkeys[l], 'peer_u': peer_u[l], 'peer_v': peer_v[l],
        }
        xp, k_l, v_l, hf, hb = trunk_layer(xp, cvec_ctx, lp, None)
        new_k.append(k_l)
        new_v.append(v_l)
        new_re.append(jnp.stack([hf.real, hb.real], axis=1))
        new_im.append(jnp.stack([hf.imag, hb.imag], axis=1))
        h0f = lax.complex(state_ssm_re[:, l, 0].astype(jnp.float32), state_ssm_im[:, l, 0].astype(jnp.float32))
        h0b = lax.complex(state_ssm_re[:, l, 1].astype(jnp.float32), state_ssm_im[:, l, 1].astype(jnp.float32))
        xs, _, _, _, _ = trunk_layer(xs, c, lp, (cache_k[:, l], cache_v[:, l], h0f, h0b))
    new_cache_k = jnp.stack(new_k, axis=1)
    new_cache_v = jnp.stack(new_v, axis=1)
    new_state_re = jnp.stack(new_re, axis=1)
    new_state_im = jnp.stack(new_im, axis=1)
    return (xp, xs, new_cache_k, new_cache_v, new_state_re, new_state_im)
```

```python
import functools
import math

import numpy as np
import jax
import jax.numpy as jnp
from jax import lax
from jax.experimental import pallas as pl
from jax.experimental.pallas import tpu as pltpu

F32 = jnp.float32
BF16 = jnp.bfloat16

D_MODEL = 1024
GRID_W = 64
N_BRANCH = 4
BRANCH_WIDTH = 512
FFT_GROUPS = 4
FFT_GROUP_CH = 128
CONV_K = 3
SSM_GROUPS = 32
SSM_CH = 16
SSM_STATE = 64
N_HEADS = 8
N_KV = 2
Q_PER_KV = N_HEADS // N_KV
HEAD_DIM = 64
WINDOW = 128
ATT_BLOCK = 128
ROPE_BASE = 10000.0
PEER_HEADS = 8
N_KEYS = 128
N_EXPERTS = N_KEYS * N_KEYS
PEER_TOPK = 16
KEY_DIM = 256
LN_EPS = 1e-5
NEG_INF = -1e30

Z_COLS = N_BRANCH * D_MODEL + 6 * BRANCH_WIDTH + 2 * N_KV * HEAD_DIM
ZG_BLK = 0
ZF_BLK, ZB_BLK, ZC_BLK, ZH_BLK, ZS_BLK, ZQ_BLK = 8, 9, 10, 11, 12, 13
ZK_BLK, ZV_BLK = 56, 57

V7X_VMEM_LIMIT_BYTES = 56 * 1024 * 1024
SSM_GBLK = 8
SSM_TCHUNK = 256
SSM_BROWS = 8
ROW_TILE = 256
PEER_TT = 512
PEER_EB = 1024
ROUTE_TT = 256


def _params(sem):
    return pltpu.CompilerParams(dimension_semantics=sem, vmem_limit_bytes=V7X_VMEM_LIMIT_BYTES)


def _gelu(x):
    return 0.5 * x * (1.0 + jnp.tanh(0.7978845608028654 * (x + 0.044715 * (x * x * x))))


def _layer_norm(h, g, b):
    mu = jnp.mean(h, axis=-1, keepdims=True)
    hc = h - mu
    var = jnp.mean(hc * hc, axis=-1, keepdims=True)
    return hc * lax.rsqrt(var + LN_EPS) * g + b


def _mod_kernel(c_ref, w_ref, b_ref, o_ref):
    cv = c_ref[...]
    s = (cv * jax.nn.sigmoid(cv)).astype(BF16)
    o_ref[0] = jnp.dot(s, w_ref[0].astype(BF16), preferred_element_type=F32) + b_ref[0]


def _modulation(cvecs, w_ada, b_ada):
    depth = w_ada.shape[0]
    nrow = cvecs.shape[0]
    return pl.pallas_call(
        _mod_kernel,
        grid=(depth, 6),
        in_specs=[
            pl.BlockSpec((nrow, D_MODEL), lambda l, j: (0, 0)),
            pl.BlockSpec((1, D_MODEL, D_MODEL), lambda l, j: (l, 0, j)),
            pl.BlockSpec((1, 1, D_MODEL), lambda l, j: (l, 0, j)),
        ],
        out_specs=pl.BlockSpec((1, nrow, D_MODEL), lambda l, j: (l, 0, j)),
        out_shape=jax.ShapeDtypeStruct((depth, nrow, 6 * D_MODEL), F32),
        compiler_params=_params(("parallel", "parallel")),
    )(cvecs, w_ada, b_ada.reshape(depth, 1, 6 * D_MODEL))


def _win_kernel(x_ref, mod_ref, w_ref, z_ref):
    sh = mod_ref[0, 0:1, :]
    sc = mod_ref[0, 1:2, :]
    xm = (x_ref[...] * (1.0 + sc) + sh).astype(BF16)
    z_ref[...] = jnp.dot(xm, w_ref[...], preferred_element_type=F32)


def _in_proj(x, mods, w_in, mod_row):
    t = x.shape[0]
    ncol = Z_COLS // 2
    return pl.pallas_call(
        _win_kernel,
        grid=(2, t // ROW_TILE),
        in_specs=[
            pl.BlockSpec((ROW_TILE, D_MODEL), lambda c, i: (i, 0)),
            pl.BlockSpec((1, 6, D_MODEL), lambda c, i: (mod_row(i), 0, 0)),
            pl.BlockSpec((D_MODEL, ncol), lambda c, i: (0, c)),
        ],
        out_specs=pl.BlockSpec((ROW_TILE, ncol), lambda c, i: (i, c)),
        out_shape=jax.ShapeDtypeStruct((t, Z_COLS), F32),
        compiler_params=_params(("parallel", "parallel")),
    )(x, mods, w_in)


def _fft_conv(zf_ref, zb_ref, zc_ref, zh_ref, cw_ref, dl_ref, dc_ref, yf_ref, yc_ref):
    length = zf_ref.shape[0]
    zf = zf_ref[...].astype(BF16)
    ab = jnp.dot(zf, dc_ref[...], preferred_element_type=F32)
    ab = jnp.concatenate([ab[:, :BRANCH_WIDTH], ab[:, BRANCH_WIDTH:]], axis=0).astype(BF16)
    yf_ref[...] = jnp.dot(dl_ref[...], ab, preferred_element_type=F32).astype(BF16)
    g = zc_ref[...] * zh_ref[...]
    row = lax.broadcasted_iota(jnp.int32, g.shape, 0)
    prev = jnp.where(row == 0, 0.0, pltpu.roll(g, 1, 0))
    nxt = jnp.where(row == length - 1, 0.0, pltpu.roll(g, length - 1, 0))
    conv = cw_ref[0:1, :] * prev + cw_ref[1:2, :] * g + cw_ref[2:3, :] * nxt
    yc_ref[...] = (zb_ref[...] * conv).astype(BF16)


def _softmax_pv(s, sink, v):
    m = jnp.maximum(jnp.max(s, axis=1, keepdims=True), sink)
    p = jnp.exp(s - m)
    den = jnp.sum(p, axis=1, keepdims=True) + jnp.exp(sink - m)
    return jnp.dot(p.astype(BF16), v, preferred_element_type=F32) / den


def _mixer_ctx_kernel(zf_ref, zb_ref, zc_ref, zh_ref, zq_ref, zk_ref, zv_ref, cw_ref, sink_ref,
                      dl_ref, dc_ref, yf_ref, yc_ref, ya_ref):
    _fft_conv(zf_ref, zb_ref, zc_ref, zh_ref, cw_ref, dl_ref, dc_ref, yf_ref, yc_ref)
    q = zq_ref[...] * (HEAD_DIM ** -0.5)
    k = zk_ref[...]
    v = zv_ref[...]
    outs = []
    for h in range(N_HEADS):
        g = h // Q_PER_KV
        qh = q[:, h * HEAD_DIM:(h + 1) * HEAD_DIM].astype(BF16)
        kg = k[:, g * HEAD_DIM:(g + 1) * HEAD_DIM].astype(BF16)
        vg = v[:, g * HEAD_DIM:(g + 1) * HEAD_DIM].astype(BF16)
        s = lax.dot_general(qh, kg, (((1,), (1,)), ((), ())), preferred_element_type=F32)
        outs.append(_softmax_pv(s, sink_ref[0:1, h:h + 1], vg))
    ya_ref[...] = jnp.concatenate(outs, axis=1).astype(BF16)


def _zspec(rows, width, row_fn, col_blk):
    return pl.BlockSpec((rows, width), lambda *a: (row_fn(*a), col_blk))


def _mixer_ctx(z, nb, seq, conv_w, sink, dft_l, dft_c):
    rf = lambda b: b
    full = lambda shape: pl.BlockSpec(shape, lambda b: (0,) * len(shape))
    out = jax.ShapeDtypeStruct((nb * seq, BRANCH_WIDTH), BF16)
    ospec = pl.BlockSpec((seq, BRANCH_WIDTH), lambda b: (b, 0))
    return pl.pallas_call(
        _mixer_ctx_kernel,
        grid=(nb,),
        in_specs=[
            _zspec(seq, 512, rf, ZF_BLK), _zspec(seq, 512, rf, ZB_BLK), _zspec(seq, 512, rf, ZC_BLK),
            _zspec(seq, 512, rf, ZH_BLK), _zspec(seq, 512, rf, ZQ_BLK),
            _zspec(seq, 128, rf, ZK_BLK), _zspec(seq, 128, rf, ZV_BLK),
            full((CONV_K, BRANCH_WIDTH)), full((1, N_HEADS)),
            full((seq, 2 * seq)), full((BRANCH_WIDTH, 2 * BRANCH_WIDTH)),
        ],
        out_specs=[ospec, ospec, ospec],
        out_shape=[out, out, out],
        compiler_params=_params(("parallel",)),
    )(z, z, z, z, z, z, z, conv_w, sink, dft_l, dft_c)


def _fftconv_lat_kernel(zf_ref, zb_ref, zc_ref, zh_ref, cw_ref, dl_ref, dc_ref, yf_ref, yc_ref):
    _fft_conv(zf_ref, zb_ref, zc_ref, zh_ref, cw_ref, dl_ref, dc_ref, yf_ref, yc_ref)


def _fftconv_lat(z, row0, nb, seq, conv_w, dft_l, dft_c):
    rf = lambda b: row0 // seq + b
    full = lambda shape: pl.BlockSpec(shape, lambda b: (0,) * len(shape))
    out = jax.ShapeDtypeStruct((nb * seq, BRANCH_WIDTH), BF16)
    ospec = pl.BlockSpec((seq, BRANCH_WIDTH), lambda b: (b, 0))
    return pl.pallas_call(
        _fftconv_lat_kernel,
        grid=(nb,),
        in_specs=[
            _zspec(seq, 512, rf, ZF_BLK), _zspec(seq, 512, rf, ZB_BLK), _zspec(seq, 512, rf, ZC_BLK),
            _zspec(seq, 512, rf, ZH_BLK),
            full((CONV_K, BRANCH_WIDTH)), full((seq, 2 * seq)), full((BRANCH_WIDTH, 2 * BRANCH_WIDTH)),
        ],
        out_specs=[ospec, ospec],
        out_shape=[out, out],
        compiler_params=_params(("parallel",)),
    )(z, z, z, z, conv_w, dft_l, dft_c)


def _rope(x, cos, sin):
    lane = lax.broadcasted_iota(jnp.int32, (x.shape[0], 128), 1)
    first = (lane & 31) < 16
    parts = []
    for c in range(x.shape[1] // 128):
        xc = x[:, c * 128:(c + 1) * 128]
        swapped = jnp.where(first, pltpu.roll(xc, 112, 1), pltpu.roll(xc, 16, 1))
        parts.append(xc * cos[:, c * 128:(c + 1) * 128] + swapped * sin[:, c * 128:(c + 1) * 128])
    return parts[0] if len(parts) == 1 else jnp.concatenate(parts, axis=1)


def _attn_lat_kernel(zq_ref, zk_ref, zv_ref, ck_ref, cv_ref, cosq_ref, sinq_ref, cosk_ref, sinkk_ref,
                     sink_ref, ya_ref):
    n = pl.program_id(1)
    nblk = pl.num_programs(1)
    nwin = 3 * ATT_BLOCK
    q = _rope(zq_ref[...], cosq_ref[...], sinq_ref[...]) * (HEAD_DIM ** -0.5)
    ws = pl.multiple_of(jnp.clip(n - 1, 0, nblk - 3) * ATT_BLOCK, ATT_BLOCK)
    kw = _rope(zk_ref[pl.ds(ws, nwin), :], cosk_ref[pl.ds(ws, nwin), :], sinkk_ref[pl.ds(ws, nwin), :])
    vw = zv_ref[pl.ds(ws, nwin), :]
    k_all = jnp.concatenate([kw, ck_ref[0]], axis=0)
    v_all = jnp.concatenate([vw, cv_ref[0]], axis=0)
    nkey = k_all.shape[0]
    qpos = n * ATT_BLOCK + lax.broadcasted_iota(jnp.int32, (ATT_BLOCK, nkey), 0)
    col = lax.broadcasted_iota(jnp.int32, (ATT_BLOCK, nkey), 1)
    valid = (jnp.abs(qpos - (ws + col)) <= WINDOW) | (col >= nwin)
    outs = []
    for h in range(N_HEADS):
        g = h // Q_PER_KV
        qh = q[:, h * HEAD_DIM:(h + 1) * HEAD_DIM].astype(BF16)
        kg = k_all[:, g * HEAD_DIM:(g + 1) * HEAD_DIM].astype(BF16)
        vg = v_all[:, g * HEAD_DIM:(g + 1) * HEAD_DIM].astype(BF16)
        s = lax.dot_general(qh, kg, (((1,), (1,)), ((), ())), preferred_element_type=F32)
        s = jnp.where(valid, s, NEG_INF)
        outs.append(_softmax_pv(s, sink_ref[0:1, h:h + 1], vg))
    ya_ref[...] = jnp.concatenate(outs, axis=1).astype(BF16)


def _attn_lat(z, row0, nb, seq, ck, cv, cosq, sinq, cosk, sink_k, sink):
    nblk = seq // ATT_BLOCK
    kvw = N_KV * HEAD_DIM
    past = ck.shape[1]
    full = lambda shape: pl.BlockSpec(shape, lambda b, n: (0,) * len(shape))
    return pl.pallas_call(
        _attn_lat_kernel,
        grid=(nb, nblk),
        in_specs=[
            pl.BlockSpec((ATT_BLOCK, 512), lambda b, n: (row0 // ATT_BLOCK + b * nblk + n, ZQ_BLK)),
            pl.BlockSpec((seq, kvw), lambda b, n: (row0 // seq + b, ZK_BLK)),
            pl.BlockSpec((seq, kvw), lambda b, n: (row0 // seq + b, ZV_BLK)),
            pl.BlockSpec((1, past, kvw), lambda b, n: (b, 0, 0)),
            pl.BlockSpec((1, past, kvw), lambda b, n: (b, 0, 0)),
            pl.BlockSpec((ATT_BLOCK, 512), lambda b, n: (n, 0)),
            pl.BlockSpec((ATT_BLOCK, 512), lambda b, n: (n, 0)),
            full((seq, kvw)), full((seq, kvw)), full((1, N_HEADS)),
        ],
        out_specs=pl.BlockSpec((ATT_BLOCK, 512), lambda b, n: (b * nblk + n, 0)),
        out_shape=jax.ShapeDtypeStruct((nb * seq, 512), BF16),
        compiler_params=_params(("parallel", "parallel")),
    )(z, z, z, ck, cv, cosq, sinq, cosk, sink_k, sink)


def _s5_kernel(uf_ref, ub_ref, h0f_ref, h0b_ref, wbf_ref, wbb_ref, cf_ref, cb_ref, af_ref, ab_ref,
               yf_ref, yb_ref, hf_ref, hb_ref, buff, bufb, hst):
    c = pl.program_id(2)
    half = SSM_GBLK * SSM_STATE
    steps = uf_ref.shape[0] // SSM_BROWS

    @pl.when(c == 0)
    def _():
        hst[0] = h0f_ref[0]
        hst[1] = h0b_ref[0]

    buff[...] = jnp.dot(uf_ref[...].astype(BF16), wbf_ref[0], preferred_element_type=F32)
    bufb[...] = jnp.dot(ub_ref[...].astype(BF16), wbb_ref[0], preferred_element_type=F32)
    afr = jnp.broadcast_to(af_ref[0, 0:1, :], (SSM_BROWS, half))
    afi = jnp.broadcast_to(af_ref[0, 1:2, :], (SSM_BROWS, half))
    abr = jnp.broadcast_to(ab_ref[0, 0:1, :], (SSM_BROWS, half))
    abi = jnp.broadcast_to(ab_ref[0, 1:2, :], (SSM_BROWS, half))

    def step(t, carry):
        hfr, hfi, hbr, hbi = carry
        rf = pl.multiple_of(t * SSM_BROWS, SSM_BROWS)
        nfr = afr * hfr - afi * hfi + buff[pl.ds(rf, SSM_BROWS), 0:half]
        nfi = afr * hfi + afi * hfr + buff[pl.ds(rf, SSM_BROWS), half:2 * half]
        buff[pl.ds(rf, SSM_BROWS), 0:half] = nfr
        buff[pl.ds(rf, SSM_BROWS), half:2 * half] = nfi
        rb = pl.multiple_of((steps - 1 - t) * SSM_BROWS, SSM_BROWS)
        nbr = abr * hbr - abi * hbi + bufb[pl.ds(rb, SSM_BROWS), 0:half]
        nbi = abr * hbi + abi * hbr + bufb[pl.ds(rb, SSM_BROWS), half:2 * half]
        bufb[pl.ds(rb, SSM_BROWS), 0:half] = nbr
        bufb[pl.ds(rb, SSM_BROWS), half:2 * half] = nbi
        return nfr, nfi, nbr, nbi

    init = (hst[0, :, 0:half], hst[0, :, half:2 * half], hst[1, :, 0:half], hst[1, :, half:2 * half])
    hfr, hfi, hbr, hbi = lax.fori_loop(0, steps, step, init, unroll=4)
    hst[0, :, 0:half] = hfr
    hst[0, :, half:2 * half] = hfi
    hst[1, :, 0:half] = hbr
    hst[1, :, half:2 * half] = hbi
    yf_ref[...] = jnp.dot(buff[...].astype(BF16), cf_ref[0], preferred_element_type=F32)
    yb_ref[...] = jnp.dot(bufb[...].astype(BF16), cb_ref[0], preferred_element_type=F32)

    @pl.when(c == pl.num_programs(2) - 1)
    def _():
        hf_ref[0] = hst[0]
        hb_ref[0] = hst[1]


def _s5(u_tm, h0f, h0b, sp, nbb, nchunk):
    rows = SSM_TCHUNK * SSM_BROWS
    ngb = SSM_GROUPS // SSM_GBLK
    width = 2 * SSM_GBLK * SSM_STATE
    nbrow = nbb * SSM_BROWS
    cw = SSM_GBLK * SSM_CH
    fwd = lambda bb, j, c: (bb * nchunk + c, j)
    bwd = lambda bb, j, c: (bb * nchunk + nchunk - 1 - c, j)
    par = lambda shape: pl.BlockSpec((1,) + shape, lambda bb, j, c: (j, 0, 0))
    st = pl.BlockSpec((1, SSM_BROWS, width), lambda bb, j, c: (j, bb, 0))
    ysh = jax.ShapeDtypeStruct(u_tm.shape, F32)
    hsh = jax.ShapeDtypeStruct((ngb, nbrow, width), F32)
    return pl.pallas_call(
        _s5_kernel,
        grid=(nbb, ngb, nchunk),
        in_specs=[
            pl.BlockSpec((rows, cw), fwd), pl.BlockSpec((rows, cw), bwd), st, st,
            par((cw, width)), par((cw, width)), par((width, cw)), par((width, cw)),
            par((2, width // 2)), par((2, width // 2)),
        ],
        out_specs=[pl.BlockSpec((rows, cw), fwd), pl.BlockSpec((rows, cw), bwd), st, st],
        out_shape=[ysh, ysh, hsh, hsh],
        scratch_shapes=[pltpu.VMEM((rows, width), F32), pltpu.VMEM((rows, width), F32),
                        pltpu.VMEM((2, SSM_BROWS, width), F32)],
        compiler_params=_params(("parallel", "parallel", "arbitrary")),
    )(u_tm, u_tm, h0f, h0b, sp["wbf"], sp["wbb"], sp["cf"], sp["cb"], sp["af"], sp["ab"])


def _s5_params(lam_re, lam_im, log_step, b_re, b_im, c_re, c_im):
    dt = jnp.exp(log_step)[:, None]
    mag = jnp.exp(lam_re * dt)
    ar = mag * jnp.cos(lam_im * dt)
    ai = mag * jnp.sin(lam_im * dt)
    den = lam_re * lam_re + lam_im * lam_im
    kr = ((ar - 1.0) * lam_re + ai * lam_im) / den
    ki = (ai * lam_re - (ar - 1.0) * lam_im) / den
    bbr = kr[..., None] * b_re - ki[..., None] * b_im
    bbi = kr[..., None] * b_im + ki[..., None] * b_re
    ngb = SSM_GROUPS // SSM_GBLK
    eye = jnp.eye(SSM_GBLK, dtype=F32)

    def blockdiag_in(m):
        m = m.reshape(ngb, SSM_GBLK, SSM_STATE, SSM_CH)
        return jnp.einsum("jgph,gk->jghkp", m, eye).reshape(ngb, SSM_GBLK * SSM_CH, SSM_GBLK * SSM_STATE)

    def blockdiag_out(m):
        m = m.reshape(ngb, SSM_GBLK, SSM_CH, SSM_STATE)
        return jnp.einsum("jghp,gk->jgpkh", m, eye).reshape(ngb, SSM_GBLK * SSM_STATE, SSM_GBLK * SSM_CH)

    wb = jnp.concatenate([blockdiag_in(bbr), blockdiag_in(bbi)], axis=2).astype(BF16)
    cm = jnp.concatenate([blockdiag_out(c_re), -blockdiag_out(c_im)], axis=1).astype(BF16)
    a = jnp.stack([ar.reshape(ngb, -1), ai.reshape(ngb, -1)], axis=1)
    return wb, cm, a


def _to_time_major(u, nb, seq):
    cdim = u.shape[1]
    nbp = -(-nb // SSM_BROWS) * SSM_BROWS
    u = u.reshape(nb, seq, cdim)
    if nbp != nb:
        u = jnp.pad(u, ((0, nbp - nb), (0, 0), (0, 0)))
    u = u.reshape(nbp // SSM_BROWS, SSM_BROWS, seq, cdim).transpose(0, 2, 1, 3)
    return u.reshape(nbp * seq, cdim), nbp


def _from_time_major(y, nb, nbp, seq):
    cdim = y.shape[1]
    y = y.reshape(nbp // SSM_BROWS, seq, SSM_BROWS, cdim).transpose(0, 2, 1, 3)
    return y.reshape(nbp, seq, cdim)[:nb].reshape(nb * seq, cdim)


def _state_to_blocks(re, im, nbp):
    nb = re.shape[0]
    ngb = SSM_GROUPS // SSM_GBLK
    def blk(x):
        return x.reshape(nb, ngb, SSM_GBLK * SSM_STATE).transpose(1, 0, 2)
    h = jnp.concatenate([blk(re), blk(im)], axis=2)
    if nbp != nb:
        h = jnp.pad(h, ((0, 0), (0, nbp - nb), (0, 0)))
    return h


def _blocks_to_state(h, nb):
    half = SSM_GBLK * SSM_STATE
    def unblk(x):
        return x[:, :nb].transpose(1, 0, 2).reshape(nb, SSM_GROUPS, SSM_STATE)
    return unblk(h[:, :, :half]), unblk(h[:, :, half:])


def _merge_kernel(x_ref, mod_ref, yf_ref, yc_ref, ya_ref, ysf_ref, ysb_ref, zs_ref,
                  zg0_ref, zg1_ref, zg2_ref, zg3_ref, d_ref, wglu_ref, wb_ref, wout_ref,
                  g_ref, b_ref, x1_ref, xm_ref):
    ys = ysf_ref[...] + ysb_ref[...] + d_ref[...] * zs_ref[...]
    ys = _gelu(ys)
    yssm = ys * jax.nn.sigmoid(jnp.dot(ys.astype(BF16), wglu_ref[...], preferred_element_type=F32))
    acc = jax.nn.sigmoid(zg0_ref[...]) * jnp.dot(yf_ref[...], wb_ref[0], preferred_element_type=F32)
    acc += jax.nn.sigmoid(zg1_ref[...]) * jnp.dot(yc_ref[...], wb_ref[1], preferred_element_type=F32)
    acc += jax.nn.sigmoid(zg2_ref[...]) * jnp.dot(yssm.astype(BF16), wb_ref[2], preferred_element_type=F32)
    acc += jax.nn.sigmoid(zg3_ref[...]) * jnp.dot(ya_ref[...], wb_ref[3], preferred_element_type=F32)
    mix = jnp.dot(acc.astype(BF16), wout_ref[...], preferred_element_type=F32)
    alpha = (2 * 4) ** 0.25
    x1 = _layer_norm(alpha * x_ref[...] + mod_ref[0, 2:3, :] * mix, g_ref[...], b_ref[...])
    x1_ref[...] = x1
    xm_ref[...] = (x1 * (1.0 + mod_ref[0, 4:5, :]) + mod_ref[0, 3:4, :]).astype(BF16)


def _merge(x, mods, mod_row, yf, yc, ya, ysf, ysb, z, ssm_d, w_glu, w_branch, w_out, ln_g, ln_b):
    t = x.shape[0]
    rf = lambda i: i
    row = lambda w: pl.BlockSpec((ROW_TILE, w), lambda i: (i, 0))
    full = lambda shape: pl.BlockSpec(shape, lambda i: (0,) * len(shape))
    return pl.pallas_call(
        _merge_kernel,
        grid=(t // ROW_TILE,),
        in_specs=[
            row(D_MODEL), pl.BlockSpec((1, 6, D_MODEL), lambda i: (mod_row(i), 0, 0)),
            row(512), row(512), row(512), row(512), row(512),
            _zspec(ROW_TILE, 512, rf, ZS_BLK),
            _zspec(ROW_TILE, 1024, rf, 0), _zspec(ROW_TILE, 1024, rf, 1),
            _zspec(ROW_TILE, 1024, rf, 2), _zspec(ROW_TILE, 1024, rf, 3),
            full((1, 512)), full((512, 512)), full((N_BRANCH, 512, D_MODEL)), full((D_MODEL, D_MODEL)),
            full((1, D_MODEL)), full((1, D_MODEL)),
        ],
        out_specs=[row(D_MODEL), row(D_MODEL)],
        out_shape=[jax.ShapeDtypeStruct((t, D_MODEL), F32), jax.ShapeDtypeStruct((t, D_MODEL), BF16)],
        compiler_params=_params(("parallel",)),
    )(x, mods, yf, yc, ya, ysf, ysb, z, z, z, z, z, ssm_d, w_glu, w_branch, w_out, ln_g, ln_b)


def _top16(s):
    n, w = s.shape
    iota = lax.broadcasted_iota(jnp.int32, (n, w), 0).astype(F32)
    kio = lax.broadcasted_iota(jnp.int32, (PEER_TOPK, w), 0)

    def body(k, carry):
        work, rank, vals, _ = carry
        m = jnp.max(work, axis=0, keepdims=True)
        pos = jnp.min(jnp.where(work == m, iota, float(n)), axis=0, keepdims=True)
        hit = iota == pos
        rank = jnp.where(hit, k.astype(F32), rank)
        work = jnp.where(hit, -jnp.inf, work)
        vals = jnp.where(kio == k, m, vals)
        return work, rank, vals, pos

    init = (s, jnp.full((n, w), 1e9, F32), jnp.zeros((PEER_TOPK, w), F32), jnp.zeros((1, w), F32))
    _, rank, vals, pos = lax.fori_loop(0, PEER_TOPK, body, init)
    return vals, rank, pos


def _route_kernel(xm_ref, wq_ref, keys_ref, s1m_ref, qrow_ref, e1_ref, s2m_ref, pb_ref, e2_ref,
                  thr_ref, qs):
    qs[...] = lax.dot_general(wq_ref[...], xm_ref[...], (((1,), (1,)), ((), ())),
                              preferred_element_type=F32)

    def head(h, carry):
        base = pl.multiple_of(h * KEY_DIM, KEY_DIM)
        q1 = qs[pl.ds(base, N_KEYS), :].astype(BF16)
        q2 = qs[pl.ds(base + N_KEYS, N_KEYS), :].astype(BF16)
        s1 = jnp.dot(keys_ref[2 * h], q1, preferred_element_type=F32)
        s2 = jnp.dot(keys_ref[2 * h + 1], q2, preferred_element_type=F32)
        v1, r1, _ = _top16(s1)
        v2, r2, _ = _top16(s2)
        cand = jnp.concatenate([v1[j:j + 1] + v2 for j in range(PEER_TOPK)], axis=0)
        vc, _, pthr = _top16(cand)
        z = jnp.sum(jnp.exp(vc - vc[0:1]), axis=0, keepdims=True)
        in1 = r1 < 100.0
        in2 = r2 < 100.0
        s1m_ref[h] = jnp.where(in1, s1, -jnp.inf)
        s2m_ref[h] = jnp.where(in2, s2, -jnp.inf)
        e1_ref[h] = jnp.where(in1, jnp.exp(s1 - v1[0:1]), 0.0) / z
        e2_ref[h] = jnp.where(in2, jnp.exp(s2 - v2[0:1]), 0.0)
        qrow_ref[h] = pthr - float(PEER_TOPK) * r1
        pb_ref[h] = r2
        thr_ref[h] = jnp.broadcast_to(vc[PEER_TOPK - 1:PEER_TOPK], (8, vc.shape[1]))
        return carry

    lax.fori_loop(0, PEER_HEADS, head, 0)


def _route(xm, wq_t, keys):
    t = xm.shape[0]
    big = jax.ShapeDtypeStruct((PEER_HEADS, N_KEYS, t), F32)
    bspec = pl.BlockSpec((PEER_HEADS, N_KEYS, ROUTE_TT), lambda i: (0, 0, i))
    return pl.pallas_call(
        _route_kernel,
        grid=(t // ROUTE_TT,),
        in_specs=[
            pl.BlockSpec((ROUTE_TT, D_MODEL), lambda i: (i, 0)),
            pl.BlockSpec((PEER_HEADS * KEY_DIM, D_MODEL), lambda i: (0, 0)),
            pl.BlockSpec((2 * PEER_HEADS, N_KEYS, N_KEYS), lambda i: (0, 0, 0)),
        ],
        out_specs=[bspec] * 6 + [pl.BlockSpec((PEER_HEADS, 8, ROUTE_TT), lambda i: (0, 0, i))],
        out_shape=[big] * 6 + [jax.ShapeDtypeStruct((PEER_HEADS, 8, t), F32)],
        scratch_shapes=[pltpu.VMEM((PEER_HEADS * KEY_DIM, ROUTE_TT), F32)],
        compiler_params=_params(("parallel",)),
    )(xm, wq_t, keys)


def _peer_kernel(xm_ref, u_ref, vt_ref, s1m_ref, qrow_ref, e1_ref, s2m_ref, pb_ref, e2_ref, thr_ref,
                 x1_ref, mod_ref, g_ref, b_ref, o_ref, ht, pt, acc):
    j = pl.program_id(1)
    nrow = PEER_EB // N_KEYS

    @pl.when(j == 0)
    def _():
        acc[...] = jnp.zeros_like(acc)

    ht[...] = lax.dot_general(u_ref[...], xm_ref[...], (((1,), (1,)), ((), ())),
                              preferred_element_type=F32)

    i1_0 = pl.multiple_of(j * nrow, nrow)
    for lg in range(PEER_TT // 128):
        lanes = slice(lg * 128, (lg + 1) * 128)
        s1t = [s1m_ref[h, pl.ds(i1_0, nrow), lanes] for h in range(PEER_HEADS)]
        qrt = [qrow_ref[h, pl.ds(i1_0, nrow), lanes] for h in range(PEER_HEADS)]
        e1t = [e1_ref[h, pl.ds(i1_0, nrow), lanes] for h in range(PEER_HEADS)]
        for r in range(nrow):
            rows = slice(r * N_KEYS, (r + 1) * N_KEYS)
            w = jnp.zeros((N_KEYS, 128), F32)
            for h in range(PEER_HEADS):
                thr = thr_ref[h, 0:1, lanes]
                sc = s1t[h][r:r + 1, :] + s2m_ref[h, :, lanes]
                first = pb_ref[h, :, lanes] <= qrt[h][r:r + 1, :]
                sel = (sc > thr) | (first & (sc == thr))
                w = w + jnp.where(sel, e1t[h][r:r + 1, :] * e2_ref[h, :, lanes], 0.0)
            pt[rows, lanes] = (w * _gelu(ht[rows, lanes])).astype(BF16)
    acc[...] += jnp.dot(vt_ref[...], pt[...], preferred_element_type=F32)

    @pl.when(j == pl.num_programs(1) - 1)
    def _():
        alpha = (2 * 4) ** 0.25
        ff = acc[...].T
        o_ref[...] = _layer_norm(alpha * x1_ref[...] + mod_ref[0, 5:6, :] * ff, g_ref[...], b_ref[...])


def _peer(xm, u_b, vt_b, routing, x1, mods, mod_row_tt, ln_g, ln_b):
    t = xm.shape[0]
    tok = lambda w: pl.BlockSpec((PEER_TT, w), lambda i, j: (i, 0))
    rspec = pl.BlockSpec((PEER_HEADS, N_KEYS, PEER_TT), lambda i, j: (0, 0, i))
    full = lambda shape: pl.BlockSpec(shape, lambda i, j: (0,) * len(shape))
    return pl.pallas_call(
        _peer_kernel,
        grid=(t // PEER_TT, N_EXPERTS // PEER_EB),
        in_specs=[
            tok(D_MODEL),
            pl.BlockSpec((PEER_EB, D_MODEL), lambda i, j: (j, 0)),
            pl.BlockSpec((D_MODEL, PEER_EB), lambda i, j: (0, j)),
            rspec, rspec, rspec, rspec, rspec, rspec,
            pl.BlockSpec((PEER_HEADS, 8, PEER_TT), lambda i, j: (0, 0, i)),
            tok(D_MODEL),
            pl.BlockSpec((1, 6, D_MODEL), lambda i, j: (mod_row_tt(i), 0, 0)),
            full((1, D_MODEL)), full((1, D_MODEL)),
        ],
        out_specs=tok(D_MODEL),
        out_shape=jax.ShapeDtypeStruct((t, D_MODEL), F32),
        scratch_shapes=[pltpu.VMEM((PEER_EB, PEER_TT), F32), pltpu.VMEM((PEER_EB, PEER_TT), BF16),
                        pltpu.VMEM((D_MODEL, PEER_TT), F32)],
        compiler_params=_params(("parallel", "arbitrary")),
    )(xm, u_b, vt_b, *routing, x1, mods, ln_g, ln_b)


def _dft_tables(length):
    n = np.arange(length)
    ang = 2.0 * np.pi * ((n[:, None] * n[None, :]) % length) / length
    dl = np.concatenate([np.cos(ang), -np.sin(ang)], axis=1) / math.sqrt(length)
    c = np.arange(FFT_GROUP_CH)
    angc = 2.0 * np.pi * ((c[:, None] * c[None, :]) % FFT_GROUP_CH) / FFT_GROUP_CH
    eye = np.eye(FFT_GROUPS)
    dc = np.concatenate([np.kron(eye, np.cos(angc)), np.kron(eye, np.sin(angc))], axis=1)
    dc = dc / math.sqrt(FFT_GROUP_CH)
    return jnp.asarray(dl, BF16), jnp.asarray(dc, BF16)


def _rope_tables(length, nheads):
    t = np.arange(length)
    pos = np.stack([t // GRID_W, t % GRID_W], axis=1).astype(np.float32)
    n_freq = HEAD_DIM // 4
    inv = (1.0 / (ROPE_BASE ** (np.arange(n_freq, dtype=np.float32) / n_freq))).astype(np.float32)
    ang = pos[:, :, None] * inv[None, None, :]
    cos = np.repeat(np.cos(ang)[:, :, None, :], 2, axis=2).reshape(length, HEAD_DIM)
    sin = np.sin(ang)
    sin = np.stack([-sin, sin], axis=2).reshape(length, HEAD_DIM)
    return (jnp.asarray(np.tile(cos, (1, nheads)), F32), jnp.asarray(np.tile(sin, (1, nheads)), F32))


def kernel(x_prompt, x_sample, cache_k, cache_v, state_ssm_re, state_ssm_im, c, c_ctx, w_ada, b_ada, w_in, conv_w, ssm_lam_re, ssm_lam_im, ssm_log_step, ssm_b_re, ssm_b_im, ssm_c_re, ssm_c_im, ssm_d, ssm_w_glu, attn_sink, w_branch, w_out, ln1_g, ln1_b, ln2_g, ln2_b, peer_wq, peer_subkeys, peer_u, peer_v):
    nb, seq, _ = x_prompt.shape
    nd, lseq, _ = x_sample.shape
    depth = w_in.shape[0]
    t_ctx = nb * seq
    t_all = t_ctx + nd * lseq
    assert t_ctx % lseq == 0 and t_all % PEER_TT == 0 and (2 * seq) % PEER_TT == 0
    assert lseq % SSM_TCHUNK == 0 and seq == SSM_TCHUNK

    x = jnp.concatenate([x_prompt.reshape(t_ctx, D_MODEL), x_sample.reshape(nd * lseq, D_MODEL)], axis=0)

    nrow = -(-(1 + nd) // 8) * 8
    cvecs = jnp.concatenate([c_ctx[None, :], c, jnp.zeros((nrow - 1 - nd, D_MODEL), F32)], axis=0)
    mods_all = _modulation(cvecs, w_ada, b_ada).reshape(depth, nrow, 6, D_MODEL)

    def mod_row_for(tile):
        nctx = t_ctx // tile
        per = lseq // tile
        return lambda i: jnp.where(i < nctx, 0, 1 + (i - nctx) // per)

    mod_row = mod_row_for(ROW_TILE)
    mod_row_tt = mod_row_for(PEER_TT)

    dl_ctx, dft_c = _dft_tables(seq)
    dl_lat, _ = _dft_tables(lseq)
    cosq, sinq = _rope_tables(lseq, N_HEADS)
    cosk, sink_k = _rope_tables(lseq, N_KV)

    gate0 = sum((512,) * 6) + 2 * N_KV * HEAD_DIM
    new_k, new_v, new_re, new_im = [], [], [], []
    for l in range(depth):
        mods = mods_all[l]
        w_in_l = jnp.concatenate([w_in[l][:, gate0:], w_in[l][:, :gate0]], axis=1).astype(BF16)
        z = _in_proj(x, mods, w_in_l, mod_row)

        sink = attn_sink[l].reshape(1, N_HEADS)
        yf_c, yc_c, ya_c = _mixer_ctx(z, nb, seq, conv_w[l], sink, dl_ctx, dft_c)
        yf_l, yc_l = _fftconv_lat(z, t_ctx, nd, lseq, conv_w[l], dl_lat, dft_c)
        ck = cache_k[:, l].reshape(nd, -1, N_KV * HEAD_DIM)
        cv = cache_v[:, l].reshape(nd, -1, N_KV * HEAD_DIM)
        ya_l = _attn_lat(z, t_ctx, nd, lseq, ck, cv, cosq, sinq, cosk, sink_k, sink)

        sp = {}
        for di, tag in ((0, "f"), (1, "b")):
            wb, cm, a = _s5_params(ssm_lam_re[l, di], ssm_lam_im[l, di], ssm_log_step[l, di],
                                   ssm_b_re[l, di], ssm_b_im[l, di], ssm_c_re[l, di], ssm_c_im[l, di])
            sp["wb" + tag], sp["c" + tag], sp["a" + tag] = wb, cm, a
        zs = z[:, ZS_BLK * 512:(ZS_BLK + 1) * 512]
        u_c, nbp_c = _to_time_major(zs[:t_ctx], nb, seq)
        zero_state = jnp.zeros((SSM_GROUPS // SSM_GBLK, nbp_c, 2 * SSM_GBLK * SSM_STATE), F32)
        ysf_c, ysb_c, hf_c, hb_c = _s5(u_c, zero_state, zero_state, sp, nbp_c // SSM_BROWS, seq // SSM_TCHUNK)
        u_l, nbp_l = _to_time_major(zs[t_ctx:], nd, lseq)
        h0f = _state_to_blocks(state_ssm_re[:, l, 0], state_ssm_im[:, l, 0], nbp_l)
        h0b = _state_to_blocks(state_ssm_re[:, l, 1], state_ssm_im[:, l, 1], nbp_l)
        ysf_l, ysb_l, _, _ = _s5(u_l, h0f, h0b, sp, nbp_l // SSM_BROWS, lseq // SSM_TCHUNK)
        ysf = jnp.concatenate([_from_time_major(ysf_c, nb, nbp_c, seq), _from_time_major(ysf_l, nd, nbp_l, lseq)], axis=0)
        ysb = jnp.concatenate([_from_time_major(ysb_c, nb, nbp_c, seq), _from_time_major(ysb_l, nd, nbp_l, lseq)], axis=0)

        yf = jnp.concatenate([yf_c, yf_l], axis=0)
        yc = jnp.concatenate([yc_c, yc_l], axis=0)
        ya = jnp.concatenate([ya_c, ya_l], axis=0)
        x1, xm2 = _merge(x, mods, mod_row, yf, yc, ya, ysf, ysb, z,
                         ssm_d[l].reshape(1, -1), ssm_w_glu[l].astype(BF16), w_branch[l].astype(BF16),
                         w_out[l].astype(BF16), ln1_g[l].reshape(1, -1), ln1_b[l].reshape(1, -1))

        routing = _route(xm2, peer_wq[l].T.astype(BF16),
                         peer_subkeys[l].reshape(2 * PEER_HEADS, N_KEYS, KEY_DIM // 2).astype(BF16))
        x = _peer(xm2, peer_u[l].astype(BF16), peer_v[l].T.astype(BF16), routing, x1, mods, mod_row_tt,
                  ln2_g[l].reshape(1, -1), ln2_b[l].reshape(1, -1))

        kv = z[:t_ctx, ZK_BLK * 128:(ZV_BLK + 1) * 128].reshape(nb, seq, 2, N_KV, HEAD_DIM)
        new_k.append(kv[:, :, 0])
        new_v.append(kv[:, :, 1])
        fre, fim = _blocks_to_state(hf_c, nb)
        bre, bim = _blocks_to_state(hb_c, nb)
        new_re.append(jnp.stack([fre, bre], axis=1))
        new_im.append(jnp.stack([fim, bim], axis=1))

    return (x[:t_ctx].reshape(nb, seq, D_MODEL), x[t_ctx:].reshape(nd, lseq, D_MODEL),
            jnp.stack(new_k, axis=1), jnp.stack(new_v, axis=1),
            jnp.stack(new_re, axis=1), jnp.stack(new_im, axis=1))
```

```python
import functools
import math

import numpy as np
import jax
import jax.numpy as jnp
from jax import lax
from jax.experimental import pallas as pl
from jax.experimental.pallas import tpu as pltpu

F32 = jnp.float32
BF16 = jnp.bfloat16

D_MODEL = 1024
GRID_W = 64
N_BRANCH = 4
BRANCH_WIDTH = 512
FFT_GROUPS = 4
FFT_GROUP_CH = 128
CONV_K = 3
SSM_GROUPS = 32
SSM_CH = 16
SSM_STATE = 64
N_HEADS = 8
N_KV = 2
Q_PER_KV = N_HEADS // N_KV
HEAD_DIM = 64
WINDOW = 128
ATT_BLOCK = 128
ROPE_BASE = 10000.0
PEER_HEADS = 8
N_KEYS = 128
N_EXPERTS = N_KEYS * N_KEYS
PEER_TOPK = 16
KEY_DIM = 256
LN_EPS = 1e-5
NEG_INF = -1e30

Z_COLS = N_BRANCH * D_MODEL + 6 * BRANCH_WIDTH + 2 * N_KV * HEAD_DIM
ZG_BLK = 0
ZF_BLK, ZB_BLK, ZC_BLK, ZH_BLK, ZS_BLK, ZQ_BLK = 8, 9, 10, 11, 12, 13
ZK_BLK, ZV_BLK = 56, 57

V7X_VMEM_LIMIT_BYTES = 56 * 1024 * 1024
SSM_GBLK = 8
SSM_TCHUNK = 256
SSM_BROWS = 8
ROW_TILE = 256
PEER_TT = 512
PEER_EB = 1024
PEER_CHUNKS = 4
ROUTE_TT = 256


def _params(sem):
    return pltpu.CompilerParams(dimension_semantics=sem, vmem_limit_bytes=V7X_VMEM_LIMIT_BYTES)


def _gelu(x):
    return 0.5 * x * (1.0 + jnp.tanh(0.7978845608028654 * (x + 0.044715 * (x * x * x))))


def _layer_norm(h, g, b):
    mu = jnp.mean(h, axis=-1, keepdims=True)
    hc = h - mu
    var = jnp.mean(hc * hc, axis=-1, keepdims=True)
    return hc * lax.rsqrt(var + LN_EPS) * g + b


def _mod_kernel(c_ref, w_ref, b_ref, o_ref):
    cv = c_ref[...]
    s = (cv * jax.nn.sigmoid(cv)).astype(BF16)
    o_ref[0] = jnp.dot(s, w_ref[0].astype(BF16), preferred_element_type=F32) + b_ref[0]


def _modulation(cvecs, w_ada, b_ada):
    depth = w_ada.shape[0]
    nrow = cvecs.shape[0]
    return pl.pallas_call(
        _mod_kernel,
        grid=(depth, 6),
        in_specs=[
            pl.BlockSpec((nrow, D_MODEL), lambda l, j: (0, 0)),
            pl.BlockSpec((1, D_MODEL, D_MODEL), lambda l, j: (l, 0, j)),
            pl.BlockSpec((1, 1, D_MODEL), lambda l, j: (l, 0, j)),
        ],
        out_specs=pl.BlockSpec((1, nrow, D_MODEL), lambda l, j: (l, 0, j)),
        out_shape=jax.ShapeDtypeStruct((depth, nrow, 6 * D_MODEL), F32),
        compiler_params=_params(("parallel", "parallel")),
    )(cvecs, w_ada, b_ada.reshape(depth, 1, 6 * D_MODEL))


def _win_kernel(x_ref, mod_ref, w_ref, z_ref):
    sh = mod_ref[0, 0:1, :]
    sc = mod_ref[0, 1:2, :]
    xm = (x_ref[...] * (1.0 + sc) + sh).astype(BF16)
    z_ref[...] = jnp.dot(xm, w_ref[...], preferred_element_type=F32)


def _in_proj(x, mods, w_in, mod_row):
    t = x.shape[0]
    ncol = Z_COLS // 2
    return pl.pallas_call(
        _win_kernel,
        grid=(2, t // ROW_TILE),
        in_specs=[
            pl.BlockSpec((ROW_TILE, D_MODEL), lambda c, i: (i, 0)),
            pl.BlockSpec((1, 6, D_MODEL), lambda c, i: (mod_row(i), 0, 0)),
            pl.BlockSpec((D_MODEL, ncol), lambda c, i: (0, c)),
        ],
        out_specs=pl.BlockSpec((ROW_TILE, ncol), lambda c, i: (i, c)),
        out_shape=jax.ShapeDtypeStruct((t, Z_COLS), F32),
        compiler_params=_params(("parallel", "parallel")),
    )(x, mods, w_in)


def _fft_conv(zf_ref, zb_ref, zc_ref, zh_ref, cw_ref, dl_ref, dc_ref, yf_ref, yc_ref):
    length = zf_ref.shape[0]
    zf = zf_ref[...].astype(BF16)
    ab = jnp.dot(zf, dc_ref[...], preferred_element_type=F32)
    ab = jnp.concatenate([ab[:, :BRANCH_WIDTH], ab[:, BRANCH_WIDTH:]], axis=0).astype(BF16)
    yf_ref[...] = jnp.dot(dl_ref[...], ab, preferred_element_type=F32).astype(BF16)
    g = zc_ref[...] * zh_ref[...]
    row = lax.broadcasted_iota(jnp.int32, g.shape, 0)
    prev = jnp.where(row == 0, 0.0, pltpu.roll(g, 1, 0))
    nxt = jnp.where(row == length - 1, 0.0, pltpu.roll(g, length - 1, 0))
    conv = cw_ref[0:1, :] * prev + cw_ref[1:2, :] * g + cw_ref[2:3, :] * nxt
    yc_ref[...] = (zb_ref[...] * conv).astype(BF16)


def _softmax_pv(s, sink, v):
    m = jnp.maximum(jnp.max(s, axis=1, keepdims=True), sink)
    p = jnp.exp(s - m)
    den = jnp.sum(p, axis=1, keepdims=True) + jnp.exp(sink - m)
    return jnp.dot(p.astype(BF16), v, preferred_element_type=F32) / den


def _mixer_ctx_kernel(zf_ref, zb_ref, zc_ref, zh_ref, zq_ref, zk_ref, zv_ref, cw_ref, sink_ref,
                      dl_ref, dc_ref, yf_ref, yc_ref, ya_ref):
    _fft_conv(zf_ref, zb_ref, zc_ref, zh_ref, cw_ref, dl_ref, dc_ref, yf_ref, yc_ref)
    q = zq_ref[...] * (HEAD_DIM ** -0.5)
    k = zk_ref[...]
    v = zv_ref[...]
    outs = []
    for h in range(N_HEADS):
        g = h // Q_PER_KV
        qh = q[:, h * HEAD_DIM:(h + 1) * HEAD_DIM].astype(BF16)
        kg = k[:, g * HEAD_DIM:(g + 1) * HEAD_DIM].astype(BF16)
        vg = v[:, g * HEAD_DIM:(g + 1) * HEAD_DIM].astype(BF16)
        s = lax.dot_general(qh, kg, (((1,), (1,)), ((), ())), preferred_element_type=F32)
        outs.append(_softmax_pv(s, sink_ref[0:1, h:h + 1], vg))
    ya_ref[...] = jnp.concatenate(outs, axis=1).astype(BF16)


def _zspec(rows, width, row_fn, col_blk):
    return pl.BlockSpec((rows, width), lambda *a: (row_fn(*a), col_blk))


def _mixer_ctx(z, nb, seq, conv_w, sink, dft_l, dft_c):
    rf = lambda b: b
    full = lambda shape: pl.BlockSpec(shape, lambda b: (0,) * len(shape))
    out = jax.ShapeDtypeStruct((nb * seq, BRANCH_WIDTH), BF16)
    ospec = pl.BlockSpec((seq, BRANCH_WIDTH), lambda b: (b, 0))
    return pl.pallas_call(
        _mixer_ctx_kernel,
        grid=(nb,),
        in_specs=[
            _zspec(seq, 512, rf, ZF_BLK), _zspec(seq, 512, rf, ZB_BLK), _zspec(seq, 512, rf, ZC_BLK),
            _zspec(seq, 512, rf, ZH_BLK), _zspec(seq, 512, rf, ZQ_BLK),
            _zspec(seq, 128, rf, ZK_BLK), _zspec(seq, 128, rf, ZV_BLK),
            full((CONV_K, BRANCH_WIDTH)), full((1, N_HEADS)),
            full((seq, 2 * seq)), full((BRANCH_WIDTH, 2 * BRANCH_WIDTH)),
        ],
        out_specs=[ospec, ospec, ospec],
        out_shape=[out, out, out],
        compiler_params=_params(("parallel",)),
    )(z, z, z, z, z, z, z, conv_w, sink, dft_l, dft_c)


def _fftconv_lat_kernel(zf_ref, zb_ref, zc_ref, zh_ref, cw_ref, dl_ref, dc_ref, yf_ref, yc_ref):
    _fft_conv(zf_ref, zb_ref, zc_ref, zh_ref, cw_ref, dl_ref, dc_ref, yf_ref, yc_ref)


def _fftconv_lat(z, row0, nb, seq, conv_w, dft_l, dft_c):
    rf = lambda b: row0 // seq + b
    full = lambda shape: pl.BlockSpec(shape, lambda b: (0,) * len(shape))
    out = jax.ShapeDtypeStruct((nb * seq, BRANCH_WIDTH), BF16)
    ospec = pl.BlockSpec((seq, BRANCH_WIDTH), lambda b: (b, 0))
    return pl.pallas_call(
        _fftconv_lat_kernel,
        grid=(nb,),
        in_specs=[
            _zspec(seq, 512, rf, ZF_BLK), _zspec(seq, 512, rf, ZB_BLK), _zspec(seq, 512, rf, ZC_BLK),
            _zspec(seq, 512, rf, ZH_BLK),
            full((CONV_K, BRANCH_WIDTH)), full((seq, 2 * seq)), full((BRANCH_WIDTH, 2 * BRANCH_WIDTH)),
        ],
        out_specs=[ospec, ospec],
        out_shape=[out, out],
        compiler_params=_params(("parallel",)),
    )(z, z, z, z, conv_w, dft_l, dft_c)


def _rope(x, cos, sin):
    lane = lax.broadcasted_iota(jnp.int32, (x.shape[0], 128), 1)
    first = (lane & 31) < 16
    parts = []
    for c in range(x.shape[1] // 128):
        xc = x[:, c * 128:(c + 1) * 128]
        swapped = jnp.where(first, pltpu.roll(xc, 112, 1), pltpu.roll(xc, 16, 1))
        parts.append(xc * cos[:, c * 128:(c + 1) * 128] + swapped * sin[:, c * 128:(c + 1) * 128])
    return parts[0] if len(parts) == 1 else jnp.concatenate(parts, axis=1)


def _attn_lat_kernel(zq_ref, zk_ref, zv_ref, ck_ref, cv_ref, cosq_ref, sinq_ref, cosk_ref, sinkk_ref,
                     sink_ref, ya_ref):
    n = pl.program_id(1)
    nblk = pl.num_programs(1)
    nwin = 3 * ATT_BLOCK
    q = _rope(zq_ref[...], cosq_ref[...], sinq_ref[...]) * (HEAD_DIM ** -0.5)
    ws = pl.multiple_of(jnp.clip(n - 1, 0, nblk - 3) * ATT_BLOCK, ATT_BLOCK)
    kw = _rope(zk_ref[pl.ds(ws, nwin), :], cosk_ref[pl.ds(ws, nwin), :], sinkk_ref[pl.ds(ws, nwin), :])
    vw = zv_ref[pl.ds(ws, nwin), :]
    k_all = jnp.concatenate([kw, ck_ref[0]], axis=0)
    v_all = jnp.concatenate([vw, cv_ref[0]], axis=0)
    nkey = k_all.shape[0]
    qpos = n * ATT_BLOCK + lax.broadcasted_iota(jnp.int32, (ATT_BLOCK, nkey), 0)
    col = lax.broadcasted_iota(jnp.int32, (ATT_BLOCK, nkey), 1)
    valid = (jnp.abs(qpos - (ws + col)) <= WINDOW) | (col >= nwin)
    outs = []
    for h in range(N_HEADS):
        g = h // Q_PER_KV
        qh = q[:, h * HEAD_DIM:(h + 1) * HEAD_DIM].astype(BF16)
        kg = k_all[:, g * HEAD_DIM:(g + 1) * HEAD_DIM].astype(BF16)
        vg = v_all[:, g * HEAD_DIM:(g + 1) * HEAD_DIM].astype(BF16)
        s = lax.dot_general(qh, kg, (((1,), (1,)), ((), ())), preferred_element_type=F32)
        s = jnp.where(valid, s, NEG_INF)
        outs.append(_softmax_pv(s, sink_ref[0:1, h:h + 1], vg))
    ya_ref[...] = jnp.concatenate(outs, axis=1).astype(BF16)


def _attn_lat(z, row0, nb, seq, ck, cv, cosq, sinq, cosk, sink_k, sink):
    nblk = seq // ATT_BLOCK
    kvw = N_KV * HEAD_DIM
    past = ck.shape[1]
    full = lambda shape: pl.BlockSpec(shape, lambda b, n: (0,) * len(shape))
    return pl.pallas_call(
        _attn_lat_kernel,
        grid=(nb, nblk),
        in_specs=[
            pl.BlockSpec((ATT_BLOCK, 512), lambda b, n: (row0 // ATT_BLOCK + b * nblk + n, ZQ_BLK)),
            pl.BlockSpec((seq, kvw), lambda b, n: (row0 // seq + b, ZK_BLK)),
            pl.BlockSpec((seq, kvw), lambda b, n: (row0 // seq + b, ZV_BLK)),
            pl.BlockSpec((1, past, kvw), lambda b, n: (b, 0, 0)),
            pl.BlockSpec((1, past, kvw), lambda b, n: (b, 0, 0)),
            pl.BlockSpec((ATT_BLOCK, 512), lambda b, n: (n, 0)),
            pl.BlockSpec((ATT_BLOCK, 512), lambda b, n: (n, 0)),
            full((seq, kvw)), full((seq, kvw)), full((1, N_HEADS)),
        ],
        out_specs=pl.BlockSpec((ATT_BLOCK, 512), lambda b, n: (b * nblk + n, 0)),
        out_shape=jax.ShapeDtypeStruct((nb * seq, 512), BF16),
        compiler_params=_params(("parallel", "parallel")),
    )(z, z, z, ck, cv, cosq, sinq, cosk, sink_k, sink)


def _s5_kernel(uf_ref, ub_ref, h0f_ref, h0b_ref, wbf_ref, wbb_ref, cf_ref, cb_ref, af_ref, ab_ref,
               yf_ref, yb_ref, hf_ref, hb_ref, buff, bufb, hst):
    c = pl.program_id(2)
    half = SSM_GBLK * SSM_STATE
    steps = uf_ref.shape[0] // SSM_BROWS

    @pl.when(c == 0)
    def _():
        hst[0] = h0f_ref[0]
        hst[1] = h0b_ref[0]

    buff[...] = jnp.dot(uf_ref[...].astype(BF16), wbf_ref[0], preferred_element_type=F32)
    bufb[...] = jnp.dot(ub_ref[...].astype(BF16), wbb_ref[0], preferred_element_type=F32)
    afr = jnp.broadcast_to(af_ref[0, 0:1, :], (SSM_BROWS, half))
    afi = jnp.broadcast_to(af_ref[0, 1:2, :], (SSM_BROWS, half))
    abr = jnp.broadcast_to(ab_ref[0, 0:1, :], (SSM_BROWS, half))
    abi = jnp.broadcast_to(ab_ref[0, 1:2, :], (SSM_BROWS, half))

    def step(t, carry):
        hfr, hfi, hbr, hbi = carry
        rf = pl.multiple_of(t * SSM_BROWS, SSM_BROWS)
        nfr = afr * hfr - afi * hfi + buff[pl.ds(rf, SSM_BROWS), 0:half]
        nfi = afr * hfi + afi * hfr + buff[pl.ds(rf, SSM_BROWS), half:2 * half]
        buff[pl.ds(rf, SSM_BROWS), 0:half] = nfr
        buff[pl.ds(rf, SSM_BROWS), half:2 * half] = nfi
        rb = pl.multiple_of((steps - 1 - t) * SSM_BROWS, SSM_BROWS)
        nbr = abr * hbr - abi * hbi + bufb[pl.ds(rb, SSM_BROWS), 0:half]
        nbi = abr * hbi + abi * hbr + bufb[pl.ds(rb, SSM_BROWS), half:2 * half]
        bufb[pl.ds(rb, SSM_BROWS), 0:half] = nbr
        bufb[pl.ds(rb, SSM_BROWS), half:2 * half] = nbi
        return nfr, nfi, nbr, nbi

    init = (hst[0, :, 0:half], hst[0, :, half:2 * half], hst[1, :, 0:half], hst[1, :, half:2 * half])
    hfr, hfi, hbr, hbi = lax.fori_loop(0, steps, step, init, unroll=4)
    hst[0, :, 0:half] = hfr
    hst[0, :, half:2 * half] = hfi
    hst[1, :, 0:half] = hbr
    hst[1, :, half:2 * half] = hbi
    yf_ref[...] = jnp.dot(buff[...].astype(BF16), cf_ref[0], preferred_element_type=F32)
    yb_ref[...] = jnp.dot(bufb[...].astype(BF16), cb_ref[0], preferred_element_type=F32)

    @pl.when(c == pl.num_programs(2) - 1)
    def _():
        hf_ref[0] = hst[0]
        hb_ref[0] = hst[1]


def _s5(u_tm, h0f, h0b, sp, nbb, nchunk):
    rows = SSM_TCHUNK * SSM_BROWS
    ngb = SSM_GROUPS // SSM_GBLK
    width = 2 * SSM_GBLK * SSM_STATE
    nbrow = nbb * SSM_BROWS
    cw = SSM_GBLK * SSM_CH
    fwd = lambda bb, j, c: (bb * nchunk + c, j)
    bwd = lambda bb, j, c: (bb * nchunk + nchunk - 1 - c, j)
    par = lambda shape: pl.BlockSpec((1,) + shape, lambda bb, j, c: (j, 0, 0))
    st = pl.BlockSpec((1, SSM_BROWS, width), lambda bb, j, c: (j, bb, 0))
    ysh = jax.ShapeDtypeStruct(u_tm.shape, F32)
    hsh = jax.ShapeDtypeStruct((ngb, nbrow, width), F32)
    return pl.pallas_call(
        _s5_kernel,
        grid=(nbb, ngb, nchunk),
        in_specs=[
            pl.BlockSpec((rows, cw), fwd), pl.BlockSpec((rows, cw), bwd), st, st,
            par((cw, width)), par((cw, width)), par((width, cw)), par((width, cw)),
            par((2, width // 2)), par((2, width // 2)),
        ],
        out_specs=[pl.BlockSpec((rows, cw), fwd), pl.BlockSpec((rows, cw), bwd), st, st],
        out_shape=[ysh, ysh, hsh, hsh],
        scratch_shapes=[pltpu.VMEM((rows, width), F32), pltpu.VMEM((rows, width), F32),
                        pltpu.VMEM((2, SSM_BROWS, width), F32)],
        compiler_params=_params(("parallel", "parallel", "arbitrary")),
    )(u_tm, u_tm, h0f, h0b, sp["wbf"], sp["wbb"], sp["cf"], sp["cb"], sp["af"], sp["ab"])


def _s5_params(lam_re, lam_im, log_step, b_re, b_im, c_re, c_im):
    dt = jnp.exp(log_step)[:, None]
    mag = jnp.exp(lam_re * dt)
    ar = mag * jnp.cos(lam_im * dt)
    ai = mag * jnp.sin(lam_im * dt)
    den = lam_re * lam_re + lam_im * lam_im
    kr = ((ar - 1.0) * lam_re + ai * lam_im) / den
    ki = (ai * lam_re - (ar - 1.0) * lam_im) / den
    bbr = kr[..., None] * b_re - ki[..., None] * b_im
    bbi = kr[..., None] * b_im + ki[..., None] * b_re
    ngb = SSM_GROUPS // SSM_GBLK
    eye = jnp.eye(SSM_GBLK, dtype=F32)

    def blockdiag_in(m):
        m = m.reshape(ngb, SSM_GBLK, SSM_STATE, SSM_CH)
        return jnp.einsum("jgph,gk->jghkp", m, eye).reshape(ngb, SSM_GBLK * SSM_CH, SSM_GBLK * SSM_STATE)

    def blockdiag_out(m):
        m = m.reshape(ngb, SSM_GBLK, SSM_CH, SSM_STATE)
        return jnp.einsum("jghp,gk->jgpkh", m, eye).reshape(ngb, SSM_GBLK * SSM_STATE, SSM_GBLK * SSM_CH)

    wb = jnp.concatenate([blockdiag_in(bbr), blockdiag_in(bbi)], axis=2).astype(BF16)
    cm = jnp.concatenate([blockdiag_out(c_re), -blockdiag_out(c_im)], axis=1).astype(BF16)
    a = jnp.stack([ar.reshape(ngb, -1), ai.reshape(ngb, -1)], axis=1)
    return wb, cm, a


def _to_time_major(u, nb, seq):
    cdim = u.shape[1]
    nbp = -(-nb // SSM_BROWS) * SSM_BROWS
    u = u.reshape(nb, seq, cdim)
    if nbp != nb:
        u = jnp.pad(u, ((0, nbp - nb), (0, 0), (0, 0)))
    u = u.reshape(nbp // SSM_BROWS, SSM_BROWS, seq, cdim).transpose(0, 2, 1, 3)
    return u.reshape(nbp * seq, cdim), nbp


def _from_time_major(y, nb, nbp, seq):
    cdim = y.shape[1]
    y = y.reshape(nbp // SSM_BROWS, seq, SSM_BROWS, cdim).transpose(0, 2, 1, 3)
    return y.reshape(nbp, seq, cdim)[:nb].reshape(nb * seq, cdim)


def _state_to_blocks(re, im, nbp):
    nb = re.shape[0]
    ngb = SSM_GROUPS // SSM_GBLK
    def blk(x):
        return x.reshape(nb, ngb, SSM_GBLK * SSM_STATE).transpose(1, 0, 2)
    h = jnp.concatenate([blk(re), blk(im)], axis=2)
    if nbp != nb:
        h = jnp.pad(h, ((0, 0), (0, nbp - nb), (0, 0)))
    return h


def _blocks_to_state(h, nb):
    half = SSM_GBLK * SSM_STATE
    def unblk(x):
        return x[:, :nb].transpose(1, 0, 2).reshape(nb, SSM_GROUPS, SSM_STATE)
    return unblk(h[:, :, :half]), unblk(h[:, :, half:])


def _merge_kernel(x_ref, mod_ref, yf_ref, yc_ref, ya_ref, ysf_ref, ysb_ref, zs_ref,
                  zg0_ref, zg1_ref, zg2_ref, zg3_ref, d_ref, wglu_ref, wb_ref, wout_ref,
                  g_ref, b_ref, x1_ref, xm_ref):
    ys = ysf_ref[...] + ysb_ref[...] + d_ref[...] * zs_ref[...]
    ys = _gelu(ys)
    yssm = ys * jax.nn.sigmoid(jnp.dot(ys.astype(BF16), wglu_ref[...], preferred_element_type=F32))
    acc = jax.nn.sigmoid(zg0_ref[...]) * jnp.dot(yf_ref[...], wb_ref[0], preferred_element_type=F32)
    acc += jax.nn.sigmoid(zg1_ref[...]) * jnp.dot(yc_ref[...], wb_ref[1], preferred_element_type=F32)
    acc += jax.nn.sigmoid(zg2_ref[...]) * jnp.dot(yssm.astype(BF16), wb_ref[2], preferred_element_type=F32)
    acc += jax.nn.sigmoid(zg3_ref[...]) * jnp.dot(ya_ref[...], wb_ref[3], preferred_element_type=F32)
    mix = jnp.dot(acc.astype(BF16), wout_ref[...], preferred_element_type=F32)
    alpha = (2 * 4) ** 0.25
    x1 = _layer_norm(alpha * x_ref[...] + mod_ref[0, 2:3, :] * mix, g_ref[...], b_ref[...])
    x1_ref[...] = x1
    xm_ref[...] = (x1 * (1.0 + mod_ref[0, 4:5, :]) + mod_ref[0, 3:4, :]).astype(BF16)


def _merge(x, mods, mod_row, yf, yc, ya, ysf, ysb, z, ssm_d, w_glu, w_branch, w_out, ln_g, ln_b):
    t = x.shape[0]
    rf = lambda i: i
    row = lambda w: pl.BlockSpec((ROW_TILE, w), lambda i: (i, 0))
    full = lambda shape: pl.BlockSpec(shape, lambda i: (0,) * len(shape))
    return pl.pallas_call(
        _merge_kernel,
        grid=(t // ROW_TILE,),
        in_specs=[
            row(D_MODEL), pl.BlockSpec((1, 6, D_MODEL), lambda i: (mod_row(i), 0, 0)),
            row(512), row(512), row(512), row(512), row(512),
            _zspec(ROW_TILE, 512, rf, ZS_BLK),
            _zspec(ROW_TILE, 1024, rf, 0), _zspec(ROW_TILE, 1024, rf, 1),
            _zspec(ROW_TILE, 1024, rf, 2), _zspec(ROW_TILE, 1024, rf, 3),
            full((1, 512)), full((512, 512)), full((N_BRANCH, 512, D_MODEL)), full((D_MODEL, D_MODEL)),
            full((1, D_MODEL)), full((1, D_MODEL)),
        ],
        out_specs=[row(D_MODEL), row(D_MODEL)],
        out_shape=[jax.ShapeDtypeStruct((t, D_MODEL), F32), jax.ShapeDtypeStruct((t, D_MODEL), BF16)],
        compiler_params=_params(("parallel",)),
    )(x, mods, yf, yc, ya, ysf, ysb, z, z, z, z, z, ssm_d, w_glu, w_branch, w_out, ln_g, ln_b)


def _top16(s):
    n, w = s.shape
    iota = lax.broadcasted_iota(jnp.int32, (n, w), 0).astype(F32)
    kio = lax.broadcasted_iota(jnp.int32, (PEER_TOPK, w), 0)

    def body(k, carry):
        work, rank, vals, _ = carry
        m = jnp.max(work, axis=0, keepdims=True)
        pos = jnp.min(jnp.where(work == m, iota, float(n)), axis=0, keepdims=True)
        hit = iota == pos
        rank = jnp.where(hit, k.astype(F32), rank)
        work = jnp.where(hit, -jnp.inf, work)
        vals = jnp.where(kio == k, m, vals)
        return work, rank, vals, pos

    init = (s, jnp.full((n, w), 1e9, F32), jnp.zeros((PEER_TOPK, w), F32), jnp.zeros((1, w), F32))
    _, rank, vals, pos = lax.fori_loop(0, PEER_TOPK, body, init)
    return vals, rank, pos


def _max16(s):
    w = s.shape[1]
    kio = lax.broadcasted_iota(jnp.int32, (PEER_TOPK, w), 0)

    def body(k, carry):
        work, vals = carry
        m = jnp.max(work, axis=0, keepdims=True)
        return jnp.where(work == m, -jnp.inf, work), jnp.where(kio == k, m, vals)

    _, vals = lax.fori_loop(0, PEER_TOPK, body, (s, jnp.zeros((PEER_TOPK, w), F32)))
    return vals


_STAIR = [(j, PEER_TOPK // (j + 1)) for j in range(PEER_TOPK)]
_STAIR_ROWS = -(-sum(k for _, k in _STAIR) // 8) * 8


def _stair_candidates(v1, v2):
    w = v1.shape[1]
    rows = [v1[j:j + 1] + v2[0:k] for j, k in _STAIR]
    npad = _STAIR_ROWS - sum(k for _, k in _STAIR)
    return jnp.concatenate(rows + [jnp.full((npad, w), -jnp.inf, F32)], axis=0)


def _stair_positions(w):
    rows = [float(PEER_TOPK * j) + lax.broadcasted_iota(jnp.int32, (k, w), 0).astype(F32) for j, k in _STAIR]
    npad = _STAIR_ROWS - sum(k for _, k in _STAIR)
    return jnp.concatenate(rows + [jnp.full((npad, w), 1e9, F32)], axis=0)


def _next_up(x):
    b = lax.bitcast_convert_type(x, jnp.int32)
    up = jnp.where(x > 0.0, b + 1, jnp.where(x < 0.0, b - 1, jnp.int32(0x00800000)))
    return lax.bitcast_convert_type(up, F32)


def _route_kernel(xm_ref, wq_ref, keys_ref, s1m_ref, qrow_ref, e1_ref, s2m_ref, pb_ref, e2_ref,
                  thr_ref, qs):
    qs[...] = lax.dot_general(wq_ref[...], xm_ref[...], (((1,), (1,)), ((), ())),
                              preferred_element_type=F32)
    w = xm_ref.shape[0]

    def count(mask):
        return jnp.sum(mask.astype(F32), axis=0, keepdims=True)

    def emit(h, s1, s2, in1, in2, m1, m2, vc, qrow, pb, thr_up):
        z = jnp.sum(jnp.exp(vc - vc[0:1]), axis=0, keepdims=True)
        s1m_ref[h] = jnp.where(in1, s1, -jnp.inf)
        s2m_ref[h] = jnp.where(in2, s2, -jnp.inf)
        e1_ref[h] = jnp.where(in1, jnp.exp(s1 - m1), 0.0) / z
        e2_ref[h] = jnp.where(in2, jnp.exp(s2 - m2), 0.0)
        qrow_ref[h] = qrow
        pb_ref[h] = pb
        thr = vc[PEER_TOPK - 1:PEER_TOPK]
        thr_ref[h] = jnp.concatenate([thr, thr_up, jnp.zeros((6, w), F32)], axis=0)

    def head(h, carry):
        base = pl.multiple_of(h * KEY_DIM, KEY_DIM)
        q1 = qs[pl.ds(base, N_KEYS), :].astype(BF16)
        q2 = qs[pl.ds(base + N_KEYS, N_KEYS), :].astype(BF16)
        s1 = jnp.dot(keys_ref[2 * h], q1, preferred_element_type=F32)
        s2 = jnp.dot(keys_ref[2 * h + 1], q2, preferred_element_type=F32)

        v1 = _max16(s1)
        v2 = _max16(s2)
        in1 = s1 >= v1[PEER_TOPK - 1:PEER_TOPK]
        in2 = s2 >= v2[PEER_TOPK - 1:PEER_TOPK]
        cand = _stair_candidates(v1, v2)
        vc = _max16(cand)
        thr = vc[PEER_TOPK - 1:PEER_TOPK]
        zero = jnp.zeros((N_KEYS, w), F32)
        emit(h, s1, s2, in1, in2, v1[0:1], v2[0:1], vc, zero, zero, thr)
        k = float(PEER_TOPK)
        bad = jnp.abs(count(in1) - k) + jnp.abs(count(in2) - k) + jnp.abs(count(cand >= thr) - k)

        @pl.when(jnp.max(bad) > 0.0)
        def _():
            xv1, r1, _ = _top16(s1)
            xv2, r2, _ = _top16(s2)
            xcand = _stair_candidates(xv1, xv2)
            xvc, _, prow = _top16(xcand)
            riota = lax.broadcasted_iota(jnp.int32, xcand.shape, 0).astype(F32)
            pthr = jnp.sum(jnp.where(riota == prow, _stair_positions(w), 0.0), axis=0, keepdims=True)
            emit(h, s1, s2, r1 < 100.0, r2 < 100.0, xv1[0:1], xv2[0:1], xvc,
                 pthr - k * r1, r2, _next_up(xvc[PEER_TOPK - 1:PEER_TOPK]))

        return carry

    lax.fori_loop(0, PEER_HEADS, head, 0)


def _route(xm, wq_t, keys):
    t = xm.shape[0]
    big = jax.ShapeDtypeStruct((PEER_HEADS, N_KEYS, t), F32)
    bspec = pl.BlockSpec((PEER_HEADS, N_KEYS, ROUTE_TT), lambda i: (0, 0, i))
    return pl.pallas_call(
        _route_kernel,
        grid=(t // ROUTE_TT,),
        in_specs=[
            pl.BlockSpec((ROUTE_TT, D_MODEL), lambda i: (i, 0)),
            pl.BlockSpec((PEER_HEADS * KEY_DIM, D_MODEL), lambda i: (0, 0)),
            pl.BlockSpec((2 * PEER_HEADS, N_KEYS, N_KEYS), lambda i: (0, 0, 0)),
        ],
        out_specs=[bspec] * 6 + [pl.BlockSpec((PEER_HEADS, 8, ROUTE_TT), lambda i: (0, 0, i))],
        out_shape=[big] * 6 + [jax.ShapeDtypeStruct((PEER_HEADS, 8, t), F32)],
        scratch_shapes=[pltpu.VMEM((PEER_HEADS * KEY_DIM, ROUTE_TT), F32)],
        compiler_params=_params(("parallel",)),
    )(xm, wq_t, keys)


def _peer_kernel(xm_ref, u_ref, vt_ref, s1m_ref, qrow_ref, e1_ref, s2m_ref, pb_ref, e2_ref, thr_ref,
                 x1_ref, mod_ref, g_ref, b_ref, o_ref, ht, wacc, pt, acc):
    j = pl.program_id(1)
    nblk = pl.num_programs(1) - 1
    nrow = PEER_EB // N_KEYS
    cur = j % 2

    @pl.when(j == 0)
    def _():
        acc[...] = jnp.zeros_like(acc)
        pt[1] = jnp.zeros((PEER_EB, PEER_TT), BF16)

    i1_0 = pl.multiple_of(jnp.minimum(j, nblk - 1) * nrow, nrow)
    heads_per_chunk = PEER_HEADS // PEER_CHUNKS
    crow = PEER_EB // PEER_CHUNKS

    def rows_of(tile):
        return jnp.stack([jnp.broadcast_to(tile[r:r + 1, :], (8, 128)) for r in range(nrow)])

    wacc[...] = jnp.zeros_like(wacc)

    def chunk(c, carry):
        r0 = pl.multiple_of(c * crow, crow)
        hc = lax.dot_general(u_ref[pl.ds(r0, crow), :], xm_ref[...], (((1,), (1,)), ((), ())),
                             preferred_element_type=F32)
        for rr in range(crow // N_KEYS):
            ht[c * (crow // N_KEYS) + rr] = hc[rr * N_KEYS:(rr + 1) * N_KEYS, :]
        acc[...] += jnp.dot(vt_ref[c], pt[1 - cur, pl.ds(r0, crow), :], preferred_element_type=F32)
        for hh in range(heads_per_chunk):
            h = c * heads_per_chunk + hh
            for lg in range(PEER_TT // 128):
                lanes = slice(lg * 128, (lg + 1) * 128)
                thr = thr_ref[h, 0:1, lanes]
                thr_up = thr_ref[h, 1:2, lanes]
                s1r = rows_of(s1m_ref[h, pl.ds(i1_0, nrow), lanes])
                qr = rows_of(qrow_ref[h, pl.ds(i1_0, nrow), lanes])
                e1r = rows_of(e1_ref[h, pl.ds(i1_0, nrow), lanes])
                for v in range(N_KEYS // 8):
                    sub = slice(v * 8, (v + 1) * 8)
                    sc = s1r + s2m_ref[h, sub, lanes][None]
                    first = pb_ref[h, sub, lanes][None] <= qr
                    wacc[:, sub, lanes] += jnp.where(sc >= jnp.where(first, thr, thr_up),
                                                     e1r * e2_ref[h, sub, lanes][None], 0.0)
        return carry

    lax.fori_loop(0, PEER_CHUNKS, chunk, 0)

    for lg in range(PEER_TT // 128):
        lanes = slice(lg * 128, (lg + 1) * 128)
        for r in range(nrow):
            pt[cur, r * N_KEYS:(r + 1) * N_KEYS, lanes] = (
                wacc[r, :, lanes] * _gelu(ht[r, :, lanes])).astype(BF16)

    @pl.when(j == nblk)
    def _():
        alpha = (2 * 4) ** 0.25
        ff = acc[...].T
        o_ref[...] = _layer_norm(alpha * x1_ref[...] + mod_ref[0, 5:6, :] * ff, g_ref[...], b_ref[...])


def _peer(xm, u_b, vt_b, routing, x1, mods, mod_row_tt, ln_g, ln_b):
    t = xm.shape[0]
    tok = lambda w: pl.BlockSpec((PEER_TT, w), lambda i, j: (i, 0))
    rspec = pl.BlockSpec((PEER_HEADS, N_KEYS, PEER_TT), lambda i, j: (0, 0, i))
    full = lambda shape: pl.BlockSpec(shape, lambda i, j: (0,) * len(shape))
    nblk = N_EXPERTS // PEER_EB
    return pl.pallas_call(
        _peer_kernel,
        grid=(t // PEER_TT, nblk + 1),
        in_specs=[
            tok(D_MODEL),
            pl.BlockSpec((PEER_EB, D_MODEL), lambda i, j: (jnp.minimum(j, nblk - 1), 0)),
            pl.BlockSpec((PEER_CHUNKS, D_MODEL, PEER_EB // PEER_CHUNKS),
                         lambda i, j: (jnp.maximum(j - 1, 0), 0, 0)),
            rspec, rspec, rspec, rspec, rspec, rspec,
            pl.BlockSpec((PEER_HEADS, 8, PEER_TT), lambda i, j: (0, 0, i)),
            tok(D_MODEL),
            pl.BlockSpec((1, 6, D_MODEL), lambda i, j: (mod_row_tt(i), 0, 0)),
            full((1, D_MODEL)), full((1, D_MODEL)),
        ],
        out_specs=tok(D_MODEL),
        out_shape=jax.ShapeDtypeStruct((t, D_MODEL), F32),
        scratch_shapes=[pltpu.VMEM((PEER_EB // N_KEYS, N_KEYS, PEER_TT), F32),
                        pltpu.VMEM((PEER_EB // N_KEYS, N_KEYS, PEER_TT), F32),
                        pltpu.VMEM((2, PEER_EB, PEER_TT), BF16),
                        pltpu.VMEM((D_MODEL, PEER_TT), F32)],
        compiler_params=_params(("parallel", "arbitrary")),
    )(xm, u_b, vt_b, *routing, x1, mods, ln_g, ln_b)


def _dft_tables(length):
    n = np.arange(length)
    ang = 2.0 * np.pi * ((n[:, None] * n[None, :]) % length) / length
    dl = np.concatenate([np.cos(ang), -np.sin(ang)], axis=1) / math.sqrt(length)
    c = np.arange(FFT_GROUP_CH)
    angc = 2.0 * np.pi * ((c[:, None] * c[None, :]) % FFT_GROUP_CH) / FFT_GROUP_CH
    eye = np.eye(FFT_GROUPS)
    dc = np.concatenate([np.kron(eye, np.cos(angc)), np.kron(eye, np.sin(angc))], axis=1)
    dc = dc / math.sqrt(FFT_GROUP_CH)
    return jnp.asarray(dl, BF16), jnp.asarray(dc, BF16)


def _rope_tables(length, nheads):
    t = np.arange(length)
    pos = np.stack([t // GRID_W, t % GRID_W], axis=1).astype(np.float32)
    n_freq = HEAD_DIM // 4
    inv = (1.0 / (ROPE_BASE ** (np.arange(n_freq, dtype=np.float32) / n_freq))).astype(np.float32)
    ang = pos[:, :, None] * inv[None, None, :]
    cos = np.repeat(np.cos(ang)[:, :, None, :], 2, axis=2).reshape(length, HEAD_DIM)
    sin = np.sin(ang)
    sin = np.stack([-sin, sin], axis=2).reshape(length, HEAD_DIM)
    return (jnp.asarray(np.tile(cos, (1, nheads)), F32), jnp.asarray(np.tile(sin, (1, nheads)), F32))


def kernel(x_prompt, x_sample, cache_k, cache_v, state_ssm_re, state_ssm_im, c, c_ctx, w_ada, b_ada, w_in, conv_w, ssm_lam_re, ssm_lam_im, ssm_log_step, ssm_b_re, ssm_b_im, ssm_c_re, ssm_c_im, ssm_d, ssm_w_glu, attn_sink, w_branch, w_out, ln1_g, ln1_b, ln2_g, ln2_b, peer_wq, peer_subkeys, peer_u, peer_v):
    nb, seq, _ = x_prompt.shape
    nd, lseq, _ = x_sample.shape
    depth = w_in.shape[0]
    t_ctx = nb * seq
    t_all = t_ctx + nd * lseq
    assert t_ctx % lseq == 0 and t_all % PEER_TT == 0 and (2 * seq) % PEER_TT == 0
    assert lseq % SSM_TCHUNK == 0 and seq == SSM_TCHUNK

    x = jnp.concatenate([x_prompt.reshape(t_ctx, D_MODEL), x_sample.reshape(nd * lseq, D_MODEL)], axis=0)

    nrow = -(-(1 + nd) // 8) * 8
    cvecs = jnp.concatenate([c_ctx[None, :], c, jnp.zeros((nrow - 1 - nd, D_MODEL), F32)], axis=0)
    mods_all = _modulation(cvecs, w_ada, b_ada).reshape(depth, nrow, 6, D_MODEL)

    def mod_row_for(tile):
        nctx = t_ctx // tile
        per = lseq // tile
        return lambda i: jnp.where(i < nctx, 0, 1 + (i - nctx) // per)

    mod_row = mod_row_for(ROW_TILE)
    mod_row_tt = mod_row_for(PEER_TT)

    dl_ctx, dft_c = _dft_tables(seq)
    dl_lat, _ = _dft_tables(lseq)
    cosq, sinq = _rope_tables(lseq, N_HEADS)
    cosk, sink_k = _rope_tables(lseq, N_KV)

    gate0 = sum((512,) * 6) + 2 * N_KV * HEAD_DIM
    new_k, new_v, new_re, new_im = [], [], [], []
    for l in range(depth):
        mods = mods_all[l]
        w_in_l = jnp.concatenate([w_in[l][:, gate0:], w_in[l][:, :gate0]], axis=1).astype(BF16)
        z = _in_proj(x, mods, w_in_l, mod_row)

        sink = attn_sink[l].reshape(1, N_HEADS)
        yf_c, yc_c, ya_c = _mixer_ctx(z, nb, seq, conv_w[l], sink, dl_ctx, dft_c)
        yf_l, yc_l = _fftconv_lat(z, t_ctx, nd, lseq, conv_w[l], dl_lat, dft_c)
        ck = cache_k[:, l].reshape(nd, -1, N_KV * HEAD_DIM)
        cv = cache_v[:, l].reshape(nd, -1, N_KV * HEAD_DIM)
        ya_l = _attn_lat(z, t_ctx, nd, lseq, ck, cv, cosq, sinq, cosk, sink_k, sink)

        sp = {}
        for di, tag in ((0, "f"), (1, "b")):
            wb, cm, a = _s5_params(ssm_lam_re[l, di], ssm_lam_im[l, di], ssm_log_step[l, di],
                                   ssm_b_re[l, di], ssm_b_im[l, di], ssm_c_re[l, di], ssm_c_im[l, di])
            sp["wb" + tag], sp["c" + tag], sp["a" + tag] = wb, cm, a
        zs = z[:, ZS_BLK * 512:(ZS_BLK + 1) * 512]
        u_c, nbp_c = _to_time_major(zs[:t_ctx], nb, seq)
        zero_state = jnp.zeros((SSM_GROUPS // SSM_GBLK, nbp_c, 2 * SSM_GBLK * SSM_STATE), F32)
        ysf_c, ysb_c, hf_c, hb_c = _s5(u_c, zero_state, zero_state, sp, nbp_c // SSM_BROWS, seq // SSM_TCHUNK)
        u_l, nbp_l = _to_time_major(zs[t_ctx:], nd, lseq)
        h0f = _state_to_blocks(state_ssm_re[:, l, 0], state_ssm_im[:, l, 0], nbp_l)
        h0b = _state_to_blocks(state_ssm_re[:, l, 1], state_ssm_im[:, l, 1], nbp_l)
        ysf_l, ysb_l, _, _ = _s5(u_l, h0f, h0b, sp, nbp_l // SSM_BROWS, lseq // SSM_TCHUNK)
        ysf = jnp.concatenate([_from_time_major(ysf_c, nb, nbp_c, seq), _from_time_major(ysf_l, nd, nbp_l, lseq)], axis=0)
        ysb = jnp.concatenate([_from_time_major(ysb_c, nb, nbp_c, seq), _from_time_major(ysb_l, nd, nbp_l, lseq)], axis=0)

        yf = jnp.concatenate([yf_c, yf_l], axis=0)
        yc = jnp.concatenate([yc_c, yc_l], axis=0)
        ya = jnp.concatenate([ya_c, ya_l], axis=0)
        x1, xm2 = _merge(x, mods, mod_row, yf, yc, ya, ysf, ysb, z,
                         ssm_d[l].reshape(1, -1), ssm_w_glu[l].astype(BF16), w_branch[l].astype(BF16),
                         w_out[l].astype(BF16), ln1_g[l].reshape(1, -1), ln1_b[l].reshape(1, -1))

        routing = _route(xm2, peer_wq[l].T.astype(BF16),
                         peer_subkeys[l].reshape(2 * PEER_HEADS, N_KEYS, KEY_DIM // 2).astype(BF16))
        kchunk = PEER_EB // PEER_CHUNKS
        vt = peer_v[l].reshape(N_EXPERTS // kchunk, kchunk, D_MODEL).transpose(0, 2, 1).astype(BF16)
        x = _peer(xm2, peer_u[l].astype(BF16), vt, routing, x1, mods, mod_row_tt,
                  ln2_g[l].reshape(1, -1), ln2_b[l].reshape(1, -1))

        kv = z[:t_ctx, ZK_BLK * 128:(ZV_BLK + 1) * 128].reshape(nb, seq, 2, N_KV, HEAD_DIM)
        new_k.append(kv[:, :, 0])
        new_v.append(kv[:, :, 1])
        fre, fim = _blocks_to_state(hf_c, nb)
        bre, bim = _blocks_to_state(hb_c, nb)
        new_re.append(jnp.stack([fre, bre], axis=1))
        new_im.append(jnp.stack([fim, bim], axis=1))

    return (x[:t_ctx].reshape(nb, seq, D_MODEL), x[t_ctx:].reshape(nd, lseq, D_MODEL),
            jnp.stack(new_k, axis=1), jnp.stack(new_v, axis=1),
            jnp.stack(new_re, axis=1), jnp.stack(new_im, axis=1))
```

```python
import functools
import math

import numpy as np
import jax
import jax.numpy as jnp
from jax import lax
from jax.experimental import pallas as pl
from jax.experimental.pallas import tpu as pltpu

F32 = jnp.float32
BF16 = jnp.bfloat16

D_MODEL = 1024
GRID_W = 64
N_BRANCH = 4
BRANCH_WIDTH = 512
FFT_GROUPS = 4
FFT_GROUP_CH = 128
CONV_K = 3
SSM_GROUPS = 32
SSM_CH = 16
SSM_STATE = 64
N_HEADS = 8
N_KV = 2
Q_PER_KV = N_HEADS // N_KV
HEAD_DIM = 64
WINDOW = 128
ATT_BLOCK = 128
ROPE_BASE = 10000.0
PEER_HEADS = 8
N_KEYS = 128
N_EXPERTS = N_KEYS * N_KEYS
PEER_TOPK = 16
KEY_DIM = 256
LN_EPS = 1e-5
NEG_INF = -1e30

Z_COLS = N_BRANCH * D_MODEL + 6 * BRANCH_WIDTH + 2 * N_KV * HEAD_DIM
ZG_BLK = 0
ZF_BLK, ZB_BLK, ZC_BLK, ZH_BLK, ZS_BLK, ZQ_BLK = 8, 9, 10, 11, 12, 13
ZK_BLK, ZV_BLK = 56, 57

V7X_VMEM_LIMIT_BYTES = 56 * 1024 * 1024
SSM_GBLK = 8
SSM_TCHUNK = 256
SSM_BROWS = 8
ROW_TILE = 256
PEER_TT = 512
PEER_EB = 1024
PEER_CHUNKS = 4
ROUTE_TT = 256


def _params(sem):
    return pltpu.CompilerParams(dimension_semantics=sem, vmem_limit_bytes=V7X_VMEM_LIMIT_BYTES)


def _gelu(x):
    return 0.5 * x * (1.0 + jnp.tanh(0.7978845608028654 * (x + 0.044715 * (x * x * x))))


def _layer_norm(h, g, b):
    mu = jnp.mean(h, axis=-1, keepdims=True)
    hc = h - mu
    var = jnp.mean(hc * hc, axis=-1, keepdims=True)
    return hc * lax.rsqrt(var + LN_EPS) * g + b


def _mod_kernel(c_ref, w_ref, b_ref, o_ref):
    cv = c_ref[...]
    s = (cv * jax.nn.sigmoid(cv)).astype(BF16)
    o_ref[0] = jnp.dot(s, w_ref[0].astype(BF16), preferred_element_type=F32) + b_ref[0]


def _modulation(cvecs, w_ada, b_ada):
    depth = w_ada.shape[0]
    nrow = cvecs.shape[0]
    return pl.pallas_call(
        _mod_kernel,
        grid=(depth, 6),
        in_specs=[
            pl.BlockSpec((nrow, D_MODEL), lambda l, j: (0, 0)),
            pl.BlockSpec((1, D_MODEL, D_MODEL), lambda l, j: (l, 0, j)),
            pl.BlockSpec((1, 1, D_MODEL), lambda l, j: (l, 0, j)),
        ],
        out_specs=pl.BlockSpec((1, nrow, D_MODEL), lambda l, j: (l, 0, j)),
        out_shape=jax.ShapeDtypeStruct((depth, nrow, 6 * D_MODEL), F32),
        compiler_params=_params(("parallel", "parallel")),
    )(cvecs, w_ada, b_ada.reshape(depth, 1, 6 * D_MODEL))


def _win_kernel(x_ref, mod_ref, w_ref, z_ref):
    sh = mod_ref[0, 0:1, :]
    sc = mod_ref[0, 1:2, :]
    xm = (x_ref[...] * (1.0 + sc) + sh).astype(BF16)
    z_ref[...] = jnp.dot(xm, w_ref[...], preferred_element_type=F32)


def _in_proj(x, mods, w_in, mod_row):
    t = x.shape[0]
    ncol = Z_COLS // 2
    return pl.pallas_call(
        _win_kernel,
        grid=(2, t // ROW_TILE),
        in_specs=[
            pl.BlockSpec((ROW_TILE, D_MODEL), lambda c, i: (i, 0)),
            pl.BlockSpec((1, 6, D_MODEL), lambda c, i: (mod_row(i), 0, 0)),
            pl.BlockSpec((D_MODEL, ncol), lambda c, i: (0, c)),
        ],
        out_specs=pl.BlockSpec((ROW_TILE, ncol), lambda c, i: (i, c)),
        out_shape=jax.ShapeDtypeStruct((t, Z_COLS), F32),
        compiler_params=_params(("parallel", "parallel")),
    )(x, mods, w_in)


def _fft_conv(zf_ref, zb_ref, zc_ref, zh_ref, cw_ref, dl_ref, dc_ref, yf_ref, yc_ref):
    length = zf_ref.shape[0]
    zf = zf_ref[...].astype(BF16)
    ab = jnp.dot(zf, dc_ref[...], preferred_element_type=F32)
    ab = jnp.concatenate([ab[:, :BRANCH_WIDTH], ab[:, BRANCH_WIDTH:]], axis=0).astype(BF16)
    yf_ref[...] = jnp.dot(dl_ref[...], ab, preferred_element_type=F32).astype(BF16)
    g = zc_ref[...] * zh_ref[...]
    row = lax.broadcasted_iota(jnp.int32, g.shape, 0)
    prev = jnp.where(row == 0, 0.0, pltpu.roll(g, 1, 0))
    nxt = jnp.where(row == length - 1, 0.0, pltpu.roll(g, length - 1, 0))
    conv = cw_ref[0:1, :] * prev + cw_ref[1:2, :] * g + cw_ref[2:3, :] * nxt
    yc_ref[...] = (zb_ref[...] * conv).astype(BF16)


def _softmax_pv(s, sink, v):
    m = jnp.maximum(jnp.max(s, axis=1, keepdims=True), sink)
    p = jnp.exp(s - m)
    den = jnp.sum(p, axis=1, keepdims=True) + jnp.exp(sink - m)
    return jnp.dot(p.astype(BF16), v, preferred_element_type=F32) / den


def _mixer_ctx_kernel(zf_ref, zb_ref, zc_ref, zh_ref, zq_ref, zk_ref, zv_ref, cw_ref, sink_ref,
                      dl_ref, dc_ref, yf_ref, yc_ref, ya_ref):
    _fft_conv(zf_ref, zb_ref, zc_ref, zh_ref, cw_ref, dl_ref, dc_ref, yf_ref, yc_ref)
    q = zq_ref[...] * (HEAD_DIM ** -0.5)
    k = zk_ref[...]
    v = zv_ref[...]
    outs = []
    for h in range(N_HEADS):
        g = h // Q_PER_KV
        qh = q[:, h * HEAD_DIM:(h + 1) * HEAD_DIM].astype(BF16)
        kg = k[:, g * HEAD_DIM:(g + 1) * HEAD_DIM].astype(BF16)
        vg = v[:, g * HEAD_DIM:(g + 1) * HEAD_DIM].astype(BF16)
        s = lax.dot_general(qh, kg, (((1,), (1,)), ((), ())), preferred_element_type=F32)
        outs.append(_softmax_pv(s, sink_ref[0:1, h:h + 1], vg))
    ya_ref[...] = jnp.concatenate(outs, axis=1).astype(BF16)


def _zspec(rows, width, row_fn, col_blk):
    return pl.BlockSpec((rows, width), lambda *a: (row_fn(*a), col_blk))


def _mixer_ctx(z, nb, seq, conv_w, sink, dft_l, dft_c):
    rf = lambda b: b
    full = lambda shape: pl.BlockSpec(shape, lambda b: (0,) * len(shape))
    out = jax.ShapeDtypeStruct((nb * seq, BRANCH_WIDTH), BF16)
    ospec = pl.BlockSpec((seq, BRANCH_WIDTH), lambda b: (b, 0))
    return pl.pallas_call(
        _mixer_ctx_kernel,
        grid=(nb,),
        in_specs=[
            _zspec(seq, 512, rf, ZF_BLK), _zspec(seq, 512, rf, ZB_BLK), _zspec(seq, 512, rf, ZC_BLK),
            _zspec(seq, 512, rf, ZH_BLK), _zspec(seq, 512, rf, ZQ_BLK),
            _zspec(seq, 128, rf, ZK_BLK), _zspec(seq, 128, rf, ZV_BLK),
            full((CONV_K, BRANCH_WIDTH)), full((1, N_HEADS)),
            full((seq, 2 * seq)), full((BRANCH_WIDTH, 2 * BRANCH_WIDTH)),
        ],
        out_specs=[ospec, ospec, ospec],
        out_shape=[out, out, out],
        compiler_params=_params(("parallel",)),
    )(z, z, z, z, z, z, z, conv_w, sink, dft_l, dft_c)


def _fftconv_lat_kernel(zf_ref, zb_ref, zc_ref, zh_ref, cw_ref, dl_ref, dc_ref, yf_ref, yc_ref):
    _fft_conv(zf_ref, zb_ref, zc_ref, zh_ref, cw_ref, dl_ref, dc_ref, yf_ref, yc_ref)


def _fftconv_lat(z, row0, nb, seq, conv_w, dft_l, dft_c):
    rf = lambda b: row0 // seq + b
    full = lambda shape: pl.BlockSpec(shape, lambda b: (0,) * len(shape))
    out = jax.ShapeDtypeStruct((nb * seq, BRANCH_WIDTH), BF16)
    ospec = pl.BlockSpec((seq, BRANCH_WIDTH), lambda b: (b, 0))
    return pl.pallas_call(
        _fftconv_lat_kernel,
        grid=(nb,),
        in_specs=[
            _zspec(seq, 512, rf, ZF_BLK), _zspec(seq, 512, rf, ZB_BLK), _zspec(seq, 512, rf, ZC_BLK),
            _zspec(seq, 512, rf, ZH_BLK),
            full((CONV_K, BRANCH_WIDTH)), full((seq, 2 * seq)), full((BRANCH_WIDTH, 2 * BRANCH_WIDTH)),
        ],
        out_specs=[ospec, ospec],
        out_shape=[out, out],
        compiler_params=_params(("parallel",)),
    )(z, z, z, z, conv_w, dft_l, dft_c)


def _rope(x, cos, sin):
    lane = lax.broadcasted_iota(jnp.int32, (x.shape[0], 128), 1)
    first = (lane & 31) < 16
    parts = []
    for c in range(x.shape[1] // 128):
        xc = x[:, c * 128:(c + 1) * 128]
        swapped = jnp.where(first, pltpu.roll(xc, 112, 1), pltpu.roll(xc, 16, 1))
        parts.append(xc * cos[:, c * 128:(c + 1) * 128] + swapped * sin[:, c * 128:(c + 1) * 128])
    return parts[0] if len(parts) == 1 else jnp.concatenate(parts, axis=1)


def _attn_lat_kernel(zq_ref, zk_ref, zv_ref, ck_ref, cv_ref, cosq_ref, sinq_ref, cosk_ref, sinkk_ref,
                     sink_ref, ya_ref):
    n = pl.program_id(1)
    nblk = pl.num_programs(1)
    nwin = 3 * ATT_BLOCK
    q = _rope(zq_ref[...], cosq_ref[...], sinq_ref[...]) * (HEAD_DIM ** -0.5)
    ws = pl.multiple_of(jnp.clip(n - 1, 0, nblk - 3) * ATT_BLOCK, ATT_BLOCK)
    kw = _rope(zk_ref[pl.ds(ws, nwin), :], cosk_ref[pl.ds(ws, nwin), :], sinkk_ref[pl.ds(ws, nwin), :])
    vw = zv_ref[pl.ds(ws, nwin), :]
    k_all = jnp.concatenate([kw, ck_ref[0]], axis=0)
    v_all = jnp.concatenate([vw, cv_ref[0]], axis=0)
    nkey = k_all.shape[0]
    qpos = n * ATT_BLOCK + lax.broadcasted_iota(jnp.int32, (ATT_BLOCK, nkey), 0)
    col = lax.broadcasted_iota(jnp.int32, (ATT_BLOCK, nkey), 1)
    valid = (jnp.abs(qpos - (ws + col)) <= WINDOW) | (col >= nwin)
    outs = []
    for h in range(N_HEADS):
        g = h // Q_PER_KV
        qh = q[:, h * HEAD_DIM:(h + 1) * HEAD_DIM].astype(BF16)
        kg = k_all[:, g * HEAD_DIM:(g + 1) * HEAD_DIM].astype(BF16)
        vg = v_all[:, g * HEAD_DIM:(g + 1) * HEAD_DIM].astype(BF16)
        s = lax.dot_general(qh, kg, (((1,), (1,)), ((), ())), preferred_element_type=F32)
        s = jnp.where(valid, s, NEG_INF)
        outs.append(_softmax_pv(s, sink_ref[0:1, h:h + 1], vg))
    ya_ref[...] = jnp.concatenate(outs, axis=1).astype(BF16)


def _attn_lat(z, row0, nb, seq, ck, cv, cosq, sinq, cosk, sink_k, sink):
    nblk = seq // ATT_BLOCK
    kvw = N_KV * HEAD_DIM
    past = ck.shape[1]
    full = lambda shape: pl.BlockSpec(shape, lambda b, n: (0,) * len(shape))
    return pl.pallas_call(
        _attn_lat_kernel,
        grid=(nb, nblk),
        in_specs=[
            pl.BlockSpec((ATT_BLOCK, 512), lambda b, n: (row0 // ATT_BLOCK + b * nblk + n, ZQ_BLK)),
            pl.BlockSpec((seq, kvw), lambda b, n: (row0 // seq + b, ZK_BLK)),
            pl.BlockSpec((seq, kvw), lambda b, n: (row0 // seq + b, ZV_BLK)),
            pl.BlockSpec((1, past, kvw), lambda b, n: (b, 0, 0)),
            pl.BlockSpec((1, past, kvw), lambda b, n: (b, 0, 0)),
            pl.BlockSpec((ATT_BLOCK, 512), lambda b, n: (n, 0)),
            pl.BlockSpec((ATT_BLOCK, 512), lambda b, n: (n, 0)),
            full((seq, kvw)), full((seq, kvw)), full((1, N_HEADS)),
        ],
        out_specs=pl.BlockSpec((ATT_BLOCK, 512), lambda b, n: (b * nblk + n, 0)),
        out_shape=jax.ShapeDtypeStruct((nb * seq, 512), BF16),
        compiler_params=_params(("parallel", "parallel")),
    )(z, z, z, ck, cv, cosq, sinq, cosk, sink_k, sink)


def _s5_kernel(uf_ref, ub_ref, h0f_ref, h0b_ref, wbf_ref, wbb_ref, cf_ref, cb_ref, af_ref, ab_ref,
               yf_ref, yb_ref, hf_ref, hb_ref, buff, bufb, hst):
    c = pl.program_id(2)
    half = SSM_GBLK * SSM_STATE
    steps = uf_ref.shape[0] // SSM_BROWS

    @pl.when(c == 0)
    def _():
        hst[0] = h0f_ref[0]
        hst[1] = h0b_ref[0]

    buff[...] = jnp.dot(uf_ref[...].astype(BF16), wbf_ref[0], preferred_element_type=F32)
    bufb[...] = jnp.dot(ub_ref[...].astype(BF16), wbb_ref[0], preferred_element_type=F32)
    afr = jnp.broadcast_to(af_ref[0, 0:1, :], (SSM_BROWS, half))
    afi = jnp.broadcast_to(af_ref[0, 1:2, :], (SSM_BROWS, half))
    abr = jnp.broadcast_to(ab_ref[0, 0:1, :], (SSM_BROWS, half))
    abi = jnp.broadcast_to(ab_ref[0, 1:2, :], (SSM_BROWS, half))

    def step(t, carry):
        hfr, hfi, hbr, hbi = carry
        rf = pl.multiple_of(t * SSM_BROWS, SSM_BROWS)
        nfr = afr * hfr - afi * hfi + buff[pl.ds(rf, SSM_BROWS), 0:half]
        nfi = afr * hfi + afi * hfr + buff[pl.ds(rf, SSM_BROWS), half:2 * half]
        buff[pl.ds(rf, SSM_BROWS), 0:half] = nfr
        buff[pl.ds(rf, SSM_BROWS), half:2 * half] = nfi
        rb = pl.multiple_of((steps - 1 - t) * SSM_BROWS, SSM_BROWS)
        nbr = abr * hbr - abi * hbi + bufb[pl.ds(rb, SSM_BROWS), 0:half]
        nbi = abr * hbi + abi * hbr + bufb[pl.ds(rb, SSM_BROWS), half:2 * half]
        bufb[pl.ds(rb, SSM_BROWS), 0:half] = nbr
        bufb[pl.ds(rb, SSM_BROWS), half:2 * half] = nbi
        return nfr, nfi, nbr, nbi

    init = (hst[0, :, 0:half], hst[0, :, half:2 * half], hst[1, :, 0:half], hst[1, :, half:2 * half])
    hfr, hfi, hbr, hbi = lax.fori_loop(0, steps, step, init, unroll=4)
    hst[0, :, 0:half] = hfr
    hst[0, :, half:2 * half] = hfi
    hst[1, :, 0:half] = hbr
    hst[1, :, half:2 * half] = hbi
    yf_ref[...] = jnp.dot(buff[...].astype(BF16), cf_ref[0], preferred_element_type=F32)
    yb_ref[...] = jnp.dot(bufb[...].astype(BF16), cb_ref[0], preferred_element_type=F32)

    @pl.when(c == pl.num_programs(2) - 1)
    def _():
        hf_ref[0] = hst[0]
        hb_ref[0] = hst[1]


def _s5(u_tm, h0f, h0b, sp, nbb, nchunk):
    rows = SSM_TCHUNK * SSM_BROWS
    ngb = SSM_GROUPS // SSM_GBLK
    width = 2 * SSM_GBLK * SSM_STATE
    nbrow = nbb * SSM_BROWS
    cw = SSM_GBLK * SSM_CH
    fwd = lambda bb, j, c: (bb * nchunk + c, j)
    bwd = lambda bb, j, c: (bb * nchunk + nchunk - 1 - c, j)
    par = lambda shape: pl.BlockSpec((1,) + shape, lambda bb, j, c: (j, 0, 0))
    st = pl.BlockSpec((1, SSM_BROWS, width), lambda bb, j, c: (j, bb, 0))
    ysh = jax.ShapeDtypeStruct(u_tm.shape, F32)
    hsh = jax.ShapeDtypeStruct((ngb, nbrow, width), F32)
    return pl.pallas_call(
        _s5_kernel,
        grid=(nbb, ngb, nchunk),
        in_specs=[
            pl.BlockSpec((rows, cw), fwd), pl.BlockSpec((rows, cw), bwd), st, st,
            par((cw, width)), par((cw, width)), par((width, cw)), par((width, cw)),
            par((2, width // 2)), par((2, width // 2)),
        ],
        out_specs=[pl.BlockSpec((rows, cw), fwd), pl.BlockSpec((rows, cw), bwd), st, st],
        out_shape=[ysh, ysh, hsh, hsh],
        scratch_shapes=[pltpu.VMEM((rows, width), F32), pltpu.VMEM((rows, width), F32),
                        pltpu.VMEM((2, SSM_BROWS, width), F32)],
        compiler_params=_params(("parallel", "parallel", "arbitrary")),
    )(u_tm, u_tm, h0f, h0b, sp["wbf"], sp["wbb"], sp["cf"], sp["cb"], sp["af"], sp["ab"])


def _s5_params(lam_re, lam_im, log_step, b_re, b_im, c_re, c_im):
    dt = jnp.exp(log_step)[:, None]
    mag = jnp.exp(lam_re * dt)
    ar = mag * jnp.cos(lam_im * dt)
    ai = mag * jnp.sin(lam_im * dt)
    den = lam_re * lam_re + lam_im * lam_im
    kr = ((ar - 1.0) * lam_re + ai * lam_im) / den
    ki = (ai * lam_re - (ar - 1.0) * lam_im) / den
    bbr = kr[..., None] * b_re - ki[..., None] * b_im
    bbi = kr[..., None] * b_im + ki[..., None] * b_re
    ngb = SSM_GROUPS // SSM_GBLK
    eye = jnp.eye(SSM_GBLK, dtype=F32)

    def blockdiag_in(m):
        m = m.reshape(ngb, SSM_GBLK, SSM_STATE, SSM_CH)
        return jnp.einsum("jgph,gk->jghkp", m, eye).reshape(ngb, SSM_GBLK * SSM_CH, SSM_GBLK * SSM_STATE)

    def blockdiag_out(m):
        m = m.reshape(ngb, SSM_GBLK, SSM_CH, SSM_STATE)
        return jnp.einsum("jghp,gk->jgpkh", m, eye).reshape(ngb, SSM_GBLK * SSM_STATE, SSM_GBLK * SSM_CH)

    wb = jnp.concatenate([blockdiag_in(bbr), blockdiag_in(bbi)], axis=2).astype(BF16)
    cm = jnp.concatenate([blockdiag_out(c_re), -blockdiag_out(c_im)], axis=1).astype(BF16)
    a = jnp.stack([ar.reshape(ngb, -1), ai.reshape(ngb, -1)], axis=1)
    return wb, cm, a


def _to_time_major(u, nb, seq):
    cdim = u.shape[1]
    nbp = -(-nb // SSM_BROWS) * SSM_BROWS
    u = u.reshape(nb, seq, cdim)
    if nbp != nb:
        u = jnp.pad(u, ((0, nbp - nb), (0, 0), (0, 0)))
    u = u.reshape(nbp // SSM_BROWS, SSM_BROWS, seq, cdim).transpose(0, 2, 1, 3)
    return u.reshape(nbp * seq, cdim), nbp


def _from_time_major(y, nb, nbp, seq):
    cdim = y.shape[1]
    y = y.reshape(nbp // SSM_BROWS, seq, SSM_BROWS, cdim).transpose(0, 2, 1, 3)
    return y.reshape(nbp, seq, cdim)[:nb].reshape(nb * seq, cdim)


def _state_to_blocks(re, im, nbp):
    nb = re.shape[0]
    ngb = SSM_GROUPS // SSM_GBLK
    def blk(x):
        return x.reshape(nb, ngb, SSM_GBLK * SSM_STATE).transpose(1, 0, 2)
    h = jnp.concatenate([blk(re), blk(im)], axis=2)
    if nbp != nb:
        h = jnp.pad(h, ((0, 0), (0, nbp - nb), (0, 0)))
    return h


def _blocks_to_state(h, nb):
    half = SSM_GBLK * SSM_STATE
    def unblk(x):
        return x[:, :nb].transpose(1, 0, 2).reshape(nb, SSM_GROUPS, SSM_STATE)
    return unblk(h[:, :, :half]), unblk(h[:, :, half:])


def _merge_kernel(x_ref, mod_ref, yf_ref, yc_ref, ya_ref, ysf_ref, ysb_ref, zs_ref,
                  zg0_ref, zg1_ref, zg2_ref, zg3_ref, d_ref, wglu_ref, wb_ref, wout_ref,
                  g_ref, b_ref, x1_ref, xm_ref):
    ys = ysf_ref[...] + ysb_ref[...] + d_ref[...] * zs_ref[...]
    ys = _gelu(ys)
    yssm = ys * jax.nn.sigmoid(jnp.dot(ys.astype(BF16), wglu_ref[...], preferred_element_type=F32))
    acc = jax.nn.sigmoid(zg0_ref[...]) * jnp.dot(yf_ref[...], wb_ref[0], preferred_element_type=F32)
    acc += jax.nn.sigmoid(zg1_ref[...]) * jnp.dot(yc_ref[...], wb_ref[1], preferred_element_type=F32)
    acc += jax.nn.sigmoid(zg2_ref[...]) * jnp.dot(yssm.astype(BF16), wb_ref[2], preferred_element_type=F32)
    acc += jax.nn.sigmoid(zg3_ref[...]) * jnp.dot(ya_ref[...], wb_ref[3], preferred_element_type=F32)
    mix = jnp.dot(acc.astype(BF16), wout_ref[...], preferred_element_type=F32)
    alpha = (2 * 4) ** 0.25
    x1 = _layer_norm(alpha * x_ref[...] + mod_ref[0, 2:3, :] * mix, g_ref[...], b_ref[...])
    x1_ref[...] = x1
    xm_ref[...] = (x1 * (1.0 + mod_ref[0, 4:5, :]) + mod_ref[0, 3:4, :]).astype(BF16)


def _merge(x, mods, mod_row, yf, yc, ya, ysf, ysb, z, ssm_d, w_glu, w_branch, w_out, ln_g, ln_b):
    t = x.shape[0]
    rf = lambda i: i
    row = lambda w: pl.BlockSpec((ROW_TILE, w), lambda i: (i, 0))
    full = lambda shape: pl.BlockSpec(shape, lambda i: (0,) * len(shape))
    return pl.pallas_call(
        _merge_kernel,
        grid=(t // ROW_TILE,),
        in_specs=[
            row(D_MODEL), pl.BlockSpec((1, 6, D_MODEL), lambda i: (mod_row(i), 0, 0)),
            row(512), row(512), row(512), row(512), row(512),
            _zspec(ROW_TILE, 512, rf, ZS_BLK),
            _zspec(ROW_TILE, 1024, rf, 0), _zspec(ROW_TILE, 1024, rf, 1),
            _zspec(ROW_TILE, 1024, rf, 2), _zspec(ROW_TILE, 1024, rf, 3),
            full((1, 512)), full((512, 512)), full((N_BRANCH, 512, D_MODEL)), full((D_MODEL, D_MODEL)),
            full((1, D_MODEL)), full((1, D_MODEL)),
        ],
        out_specs=[row(D_MODEL), row(D_MODEL)],
        out_shape=[jax.ShapeDtypeStruct((t, D_MODEL), F32), jax.ShapeDtypeStruct((t, D_MODEL), BF16)],
        compiler_params=_params(("parallel",)),
    )(x, mods, yf, yc, ya, ysf, ysb, z, z, z, z, z, ssm_d, w_glu, w_branch, w_out, ln_g, ln_b)


def _top16(s):
    n, w = s.shape
    iota = lax.broadcasted_iota(jnp.int32, (n, w), 0).astype(F32)
    kio = lax.broadcasted_iota(jnp.int32, (PEER_TOPK, w), 0)

    def body(k, carry):
        work, rank, vals, _ = carry
        m = jnp.max(work, axis=0, keepdims=True)
        pos = jnp.min(jnp.where(work == m, iota, float(n)), axis=0, keepdims=True)
        hit = iota == pos
        rank = jnp.where(hit, lax.convert_element_type(k, F32), rank)
        work = jnp.where(hit, -jnp.inf, work)
        vals = jnp.where(kio == k, m, vals)
        return work, rank, vals, pos

    init = (s, jnp.full((n, w), 1e9, F32), jnp.zeros((PEER_TOPK, w), F32), jnp.zeros((1, w), F32))
    _, rank, vals, pos = lax.fori_loop(0, PEER_TOPK, body, init)
    return vals, rank, pos


def _max16(s):
    w = s.shape[1]
    kio = lax.broadcasted_iota(jnp.int32, (PEER_TOPK, w), 0)

    def body(k, carry):
        work, vals = carry
        m = jnp.max(work, axis=0, keepdims=True)
        return jnp.where(work == m, -jnp.inf, work), jnp.where(kio == k, m, vals)

    _, vals = lax.fori_loop(0, PEER_TOPK, body, (s, jnp.zeros((PEER_TOPK, w), F32)))
    return vals


_STAIR = [(j, PEER_TOPK // (j + 1)) for j in range(PEER_TOPK)]
_STAIR_ROWS = -(-sum(k for _, k in _STAIR) // 8) * 8


def _stair_candidates(v1, v2):
    w = v1.shape[1]
    rows = [v1[j:j + 1] + v2[0:k] for j, k in _STAIR]
    npad = _STAIR_ROWS - sum(k for _, k in _STAIR)
    return jnp.concatenate(rows + [jnp.full((npad, w), -jnp.inf, F32)], axis=0)


def _stair_positions(w):
    rows = [float(PEER_TOPK * j) + lax.broadcasted_iota(jnp.int32, (k, w), 0).astype(F32) for j, k in _STAIR]
    npad = _STAIR_ROWS - sum(k for _, k in _STAIR)
    return jnp.concatenate(rows + [jnp.full((npad, w), 1e9, F32)], axis=0)


def _next_up(x):
    b = lax.bitcast_convert_type(x, jnp.int32)
    up = jnp.where(x > 0.0, b + 1, jnp.where(x < 0.0, b - 1, jnp.int32(0x00800000)))
    return lax.bitcast_convert_type(up, F32)


def _route_kernel(xm_ref, wq_ref, keys_ref, s1m_ref, qrow_ref, e1_ref, s2m_ref, pb_ref, e2_ref,
                  thr_ref, qs):
    qs[...] = lax.dot_general(wq_ref[...], xm_ref[...], (((1,), (1,)), ((), ())),
                              preferred_element_type=F32)
    w = xm_ref.shape[0]

    def count(mask):
        return jnp.sum(mask.astype(F32), axis=0, keepdims=True)

    def emit(h, s1, s2, in1, in2, m1, m2, vc, qrow, pb, thr_up, bad):
        z = jnp.sum(jnp.exp(vc - vc[0:1]), axis=0, keepdims=True)
        s1m_ref[h] = jnp.where(in1, s1, -jnp.inf)
        s2m_ref[h] = jnp.where(in2, s2, -jnp.inf)
        e1_ref[h] = jnp.where(in1, jnp.exp(s1 - m1), 0.0) / z
        e2_ref[h] = jnp.where(in2, jnp.exp(s2 - m2), 0.0)
        qrow_ref[h] = qrow
        pb_ref[h] = pb
        thr = vc[PEER_TOPK - 1:PEER_TOPK]
        thr_ref[h] = jnp.concatenate([thr, thr_up, bad, jnp.zeros((5, w), F32)], axis=0)

    def head(h, carry):
        base = pl.multiple_of(h * KEY_DIM, KEY_DIM)
        q1 = qs[pl.ds(base, N_KEYS), :].astype(BF16)
        q2 = qs[pl.ds(base + N_KEYS, N_KEYS), :].astype(BF16)
        s1 = jnp.dot(keys_ref[2 * h], q1, preferred_element_type=F32)
        s2 = jnp.dot(keys_ref[2 * h + 1], q2, preferred_element_type=F32)

        v1 = _max16(s1)
        v2 = _max16(s2)
        in1 = s1 >= v1[PEER_TOPK - 1:PEER_TOPK]
        in2 = s2 >= v2[PEER_TOPK - 1:PEER_TOPK]
        cand = _stair_candidates(v1, v2)
        vc = _max16(cand)
        thr = vc[PEER_TOPK - 1:PEER_TOPK]
        zero = jnp.zeros((N_KEYS, w), F32)
        k = float(PEER_TOPK)
        bad = jnp.abs(count(in1) - k) + jnp.abs(count(in2) - k) + jnp.abs(count(cand >= thr) - k)
        emit(h, s1, s2, in1, in2, v1[0:1], v2[0:1], vc, zero, zero, thr, bad)

        @pl.when(jnp.max(bad) > 0.0)
        def _():
            xv1, r1, _ = _top16(s1)
            xv2, r2, _ = _top16(s2)
            xcand = _stair_candidates(xv1, xv2)
            xvc, _, prow = _top16(xcand)
            riota = lax.broadcasted_iota(jnp.int32, xcand.shape, 0).astype(F32)
            pthr = jnp.sum(jnp.where(riota == prow, _stair_positions(w), 0.0), axis=0, keepdims=True)
            emit(h, s1, s2, r1 < 100.0, r2 < 100.0, xv1[0:1], xv2[0:1], xvc,
                 pthr - k * r1, r2, _next_up(xvc[PEER_TOPK - 1:PEER_TOPK]), bad)

        return carry

    lax.fori_loop(0, PEER_HEADS, head, 0)


def _route(xm, wq_t, keys):
    t = xm.shape[0]
    big = jax.ShapeDtypeStruct((PEER_HEADS, N_KEYS, t), F32)
    bspec = pl.BlockSpec((PEER_HEADS, N_KEYS, ROUTE_TT), lambda i: (0, 0, i))
    return pl.pallas_call(
        _route_kernel,
        grid=(t // ROUTE_TT,),
        in_specs=[
            pl.BlockSpec((ROUTE_TT, D_MODEL), lambda i: (i, 0)),
            pl.BlockSpec((PEER_HEADS * KEY_DIM, D_MODEL), lambda i: (0, 0)),
            pl.BlockSpec((2 * PEER_HEADS, N_KEYS, N_KEYS), lambda i: (0, 0, 0)),
        ],
        out_specs=[bspec] * 6 + [pl.BlockSpec((PEER_HEADS, 8, ROUTE_TT), lambda i: (0, 0, i))],
        out_shape=[big] * 6 + [jax.ShapeDtypeStruct((PEER_HEADS, 8, t), F32)],
        scratch_shapes=[pltpu.VMEM((PEER_HEADS * KEY_DIM, ROUTE_TT), F32)],
        compiler_params=_params(("parallel",)),
    )(xm, wq_t, keys)


def _peer_kernel(flag_ref, xm_ref, u_ref, vt_ref, s1m_ref, qrow_ref, e1_ref, s2m_ref, pb_ref, e2_ref,
                 thr_ref, x1_ref, mod_ref, g_ref, b_ref, o_ref, ht, wacc, pt, acc):
    j = pl.program_id(1)
    nrow = PEER_EB // N_KEYS

    @pl.when(j == 0)
    def _():
        acc[...] = jnp.zeros_like(acc)

    i1_0 = pl.multiple_of(j * nrow, nrow)
    heads_per_chunk = PEER_HEADS // PEER_CHUNKS
    rpass = 4

    def rows_of(tile, r0):
        return jnp.stack([jnp.broadcast_to(tile[r:r + 1, :], (8, 128)) for r in range(r0, r0 + rpass)])

    def chunk_work(c, exact):
        for hh in range(heads_per_chunk):
            h = c * heads_per_chunk + hh
            for lg in range(PEER_TT // 128):
                lanes = slice(lg * 128, (lg + 1) * 128)
                thr = thr_ref[h, 0:1, lanes]
                s1t = s1m_ref[h, pl.ds(i1_0, nrow), lanes]
                e1t = e1_ref[h, pl.ds(i1_0, nrow), lanes]
                if exact:
                    thr_up = thr_ref[h, 1:2, lanes]
                    qrt = qrow_ref[h, pl.ds(i1_0, nrow), lanes]
                for rp in range(0, nrow, rpass):
                    s1r = rows_of(s1t, rp)
                    e1r = rows_of(e1t, rp)
                    if exact:
                        qr = rows_of(qrt, rp)
                    for v in range(N_KEYS // 8):
                        sub = slice(v * 8, (v + 1) * 8)
                        sc = s1r + s2m_ref[h, sub, lanes][None]
                        if exact:
                            first = pb_ref[h, sub, lanes][None] <= qr
                            sel = sc >= jnp.where(first, thr, thr_up)
                        else:
                            sel = sc >= thr
                        gate = jnp.where(sel, e1r * e2_ref[h, sub, lanes][None], 0.0)
                        if h == 0:
                            wacc[rp:rp + rpass, sub, lanes] = gate
                        else:
                            wacc[rp:rp + rpass, sub, lanes] += gate

    hfull = lax.dot_general(u_ref[...], xm_ref[...], (((1,), (1,)), ((), ())),
                            preferred_element_type=F32)
    for r in range(nrow):
        ht[r] = hfull[r * N_KEYS:(r + 1) * N_KEYS, :]
    for c in range(PEER_CHUNKS):
        needs_ties = flag_ref[pl.program_id(0), c] > 0
        pl.when(needs_ties)(functools.partial(chunk_work, c, True))
        pl.when(jnp.logical_not(needs_ties))(functools.partial(chunk_work, c, False))
    for lg in range(PEER_TT // 128):
        lanes = slice(lg * 128, (lg + 1) * 128)
        for r in range(nrow):
            pt[r * N_KEYS:(r + 1) * N_KEYS, lanes] = (
                wacc[r, :, lanes] * _gelu(ht[r, :, lanes])).astype(BF16)
    acc[...] += jnp.dot(vt_ref[...], pt[...], preferred_element_type=F32)

    @pl.when(j == pl.num_programs(1) - 1)
    def _():
        alpha = (2 * 4) ** 0.25
        ff = acc[...].T
        o_ref[...] = _layer_norm(alpha * x1_ref[...] + mod_ref[0, 5:6, :] * ff, g_ref[...], b_ref[...])


def _peer(xm, u_b, vt_b, routing, x1, mods, mod_row_tt, ln_g, ln_b):
    t = xm.shape[0]
    tok = lambda w: pl.BlockSpec((PEER_TT, w), lambda i, j: (i, 0))
    rspec = pl.BlockSpec((PEER_HEADS, N_KEYS, PEER_TT), lambda i, j: (0, 0, i))
    full = lambda shape: pl.BlockSpec(shape, lambda i, j: (0,) * len(shape))
    nblk = N_EXPERTS // PEER_EB
    bad = routing[6][:, 2, :].reshape(PEER_CHUNKS, PEER_HEADS // PEER_CHUNKS, t // PEER_TT, PEER_TT)
    flags = (jnp.max(bad, axis=(1, 3)) > 0.0).astype(jnp.int32).T
    return pl.pallas_call(
        _peer_kernel,
        grid=(t // PEER_TT, nblk),
        in_specs=[
            pl.BlockSpec(memory_space=pltpu.SMEM),
            tok(D_MODEL),
            pl.BlockSpec((PEER_EB, D_MODEL), lambda i, j: (j, 0)),
            pl.BlockSpec((D_MODEL, PEER_EB), lambda i, j: (0, j)),
            rspec, rspec, rspec, rspec, rspec, rspec,
            pl.BlockSpec((PEER_HEADS, 8, PEER_TT), lambda i, j: (0, 0, i)),
            tok(D_MODEL),
            pl.BlockSpec((1, 6, D_MODEL), lambda i, j: (mod_row_tt(i), 0, 0)),
            full((1, D_MODEL)), full((1, D_MODEL)),
        ],
        out_specs=tok(D_MODEL),
        out_shape=jax.ShapeDtypeStruct((t, D_MODEL), F32),
        scratch_shapes=[pltpu.VMEM((PEER_EB // N_KEYS, N_KEYS, PEER_TT), F32),
                        pltpu.VMEM((PEER_EB // N_KEYS, N_KEYS, PEER_TT), F32),
                        pltpu.VMEM((PEER_EB, PEER_TT), BF16),
                        pltpu.VMEM((D_MODEL, PEER_TT), F32)],
        compiler_params=_params(("parallel", "arbitrary")),
    )(flags, xm, u_b, vt_b, *routing, x1, mods, ln_g, ln_b)


def _dft_tables(length):
    n = np.arange(length)
    ang = 2.0 * np.pi * ((n[:, None] * n[None, :]) % length) / length
    dl = np.concatenate([np.cos(ang), -np.sin(ang)], axis=1) / math.sqrt(length)
    c = np.arange(FFT_GROUP_CH)
    angc = 2.0 * np.pi * ((c[:, None] * c[None, :]) % FFT_GROUP_CH) / FFT_GROUP_CH
    eye = np.eye(FFT_GROUPS)
    dc = np.concatenate([np.kron(eye, np.cos(angc)), np.kron(eye, np.sin(angc))], axis=1)
    dc = dc / math.sqrt(FFT_GROUP_CH)
    return jnp.asarray(dl, BF16), jnp.asarray(dc, BF16)


def _rope_tables(length, nheads):
    t = np.arange(length)
    pos = np.stack([t // GRID_W, t % GRID_W], axis=1).astype(np.float32)
    n_freq = HEAD_DIM // 4
    inv = (1.0 / (ROPE_BASE ** (np.arange(n_freq, dtype=np.float32) / n_freq))).astype(np.float32)
    ang = pos[:, :, None] * inv[None, None, :]
    cos = np.repeat(np.cos(ang)[:, :, None, :], 2, axis=2).reshape(length, HEAD_DIM)
    sin = np.sin(ang)
    sin = np.stack([-sin, sin], axis=2).reshape(length, HEAD_DIM)
    return (jnp.asarray(np.tile(cos, (1, nheads)), F32), jnp.asarray(np.tile(sin, (1, nheads)), F32))


def kernel(x_prompt, x_sample, cache_k, cache_v, state_ssm_re, state_ssm_im, c, c_ctx, w_ada, b_ada, w_in, conv_w, ssm_lam_re, ssm_lam_im, ssm_log_step, ssm_b_re, ssm_b_im, ssm_c_re, ssm_c_im, ssm_d, ssm_w_glu, attn_sink, w_branch, w_out, ln1_g, ln1_b, ln2_g, ln2_b, peer_wq, peer_subkeys, peer_u, peer_v):
    nb, seq, _ = x_prompt.shape
    nd, lseq, _ = x_sample.shape
    depth = w_in.shape[0]
    t_ctx = nb * seq
    t_all = t_ctx + nd * lseq
    assert t_ctx % lseq == 0 and t_all % PEER_TT == 0 and (2 * seq) % PEER_TT == 0
    assert lseq % SSM_TCHUNK == 0 and seq == SSM_TCHUNK

    x = jnp.concatenate([x_prompt.reshape(t_ctx, D_MODEL), x_sample.reshape(nd * lseq, D_MODEL)], axis=0)

    nrow = -(-(1 + nd) // 8) * 8
    cvecs = jnp.concatenate([c_ctx[None, :], c, jnp.zeros((nrow - 1 - nd, D_MODEL), F32)], axis=0)
    mods_all = _modulation(cvecs, w_ada, b_ada).reshape(depth, nrow, 6, D_MODEL)

    def mod_row_for(tile):
        nctx = t_ctx // tile
        per = lseq // tile
        return lambda i: jnp.where(i < nctx, 0, 1 + (i - nctx) // per)

    mod_row = mod_row_for(ROW_TILE)
    mod_row_tt = mod_row_for(PEER_TT)

    dl_ctx, dft_c = _dft_tables(seq)
    dl_lat, _ = _dft_tables(lseq)
    cosq, sinq = _rope_tables(lseq, N_HEADS)
    cosk, sink_k = _rope_tables(lseq, N_KV)

    gate0 = sum((512,) * 6) + 2 * N_KV * HEAD_DIM
    new_k, new_v, new_re, new_im = [], [], [], []
    for l in range(depth):
        mods = mods_all[l]
        w_in_l = jnp.concatenate([w_in[l][:, gate0:], w_in[l][:, :gate0]], axis=1).astype(BF16)
        z = _in_proj(x, mods, w_in_l, mod_row)

        sink = attn_sink[l].reshape(1, N_HEADS)
        yf_c, yc_c, ya_c = _mixer_ctx(z, nb, seq, conv_w[l], sink, dl_ctx, dft_c)
        yf_l, yc_l = _fftconv_lat(z, t_ctx, nd, lseq, conv_w[l], dl_lat, dft_c)
        ck = cache_k[:, l].reshape(nd, -1, N_KV * HEAD_DIM)
        cv = cache_v[:, l].reshape(nd, -1, N_KV * HEAD_DIM)
        ya_l = _attn_lat(z, t_ctx, nd, lseq, ck, cv, cosq, sinq, cosk, sink_k, sink)

        sp = {}
        for di, tag in ((0, "f"), (1, "b")):
            wb, cm, a = _s5_params(ssm_lam_re[l, di], ssm_lam_im[l, di], ssm_log_step[l, di],
                                   ssm_b_re[l, di], ssm_b_im[l, di], ssm_c_re[l, di], ssm_c_im[l, di])
            sp["wb" + tag], sp["c" + tag], sp["a" + tag] = wb, cm, a
        zs = z[:, ZS_BLK * 512:(ZS_BLK + 1) * 512]
        u_c, nbp_c = _to_time_major(zs[:t_ctx], nb, seq)
        zero_state = jnp.zeros((SSM_GROUPS // SSM_GBLK, nbp_c, 2 * SSM_GBLK * SSM_STATE), F32)
        ysf_c, ysb_c, hf_c, hb_c = _s5(u_c, zero_state, zero_state, sp, nbp_c // SSM_BROWS, seq // SSM_TCHUNK)
        u_l, nbp_l = _to_time_major(zs[t_ctx:], nd, lseq)
        h0f = _state_to_blocks(state_ssm_re[:, l, 0], state_ssm_im[:, l, 0], nbp_l)
        h0b = _state_to_blocks(state_ssm_re[:, l, 1], state_ssm_im[:, l, 1], nbp_l)
        ysf_l, ysb_l, _, _ = _s5(u_l, h0f, h0b, sp, nbp_l // SSM_BROWS, lseq // SSM_TCHUNK)
        ysf = jnp.concatenate([_from_time_major(ysf_c, nb, nbp_c, seq), _from_time_major(ysf_l, nd, nbp_l, lseq)], axis=0)
        ysb = jnp.concatenate([_from_time_major(ysb_c, nb, nbp_c, seq), _from_time_major(ysb_l, nd, nbp_l, lseq)], axis=0)

        yf = jnp.concatenate([yf_c, yf_l], axis=0)
        yc = jnp.concatenate([yc_c, yc_l], axis=0)
        ya = jnp.concatenate([ya_c, ya_l], axis=0)
        x1, xm2 = _merge(x, mods, mod_row, yf, yc, ya, ysf, ysb, z,
                         ssm_d[l].reshape(1, -1), ssm_w_glu[l].astype(BF16), w_branch[l].astype(BF16),
                         w_out[l].astype(BF16), ln1_g[l].reshape(1, -1), ln1_b[l].reshape(1, -1))

        routing = _route(xm2, peer_wq[l].T.astype(BF16),
                         peer_subkeys[l].reshape(2 * PEER_HEADS, N_KEYS, KEY_DIM // 2).astype(BF16))
        x = _peer(xm2, peer_u[l].astype(BF16), peer_v[l].T.astype(BF16), routing, x1, mods, mod_row_tt,
                  ln2_g[l].reshape(1, -1), ln2_b[l].reshape(1, -1))

        kv = z[:t_ctx, ZK_BLK * 128:(ZV_BLK + 1) * 128].reshape(nb, seq, 2, N_KV, HEAD_DIM)
        new_k.append(kv[:, :, 0])
        new_v.append(kv[:, :, 1])
        fre, fim = _blocks_to_state(hf_c, nb)
        bre, bim = _blocks_to_state(hb_c, nb)
        new_re.append(jnp.stack([fre, bre], axis=1))
        new_im.append(jnp.stack([fim, bim], axis=1))

    return (x[:t_ctx].reshape(nb, seq, D_MODEL), x[t_ctx:].reshape(nd, lseq, D_MODEL),
            jnp.stack(new_k, axis=1), jnp.stack(new_v, axis=1),
            jnp.stack(new_re, axis=1), jnp.stack(new_im, axis=1))
```

```python
import functools
import math

import numpy as np
import jax
import jax.numpy as jnp
from jax import lax
from jax.experimental import pallas as pl
from jax.experimental.pallas import tpu as pltpu

F32 = jnp.float32
BF16 = jnp.bfloat16

D_MODEL = 1024
GRID_W = 64
N_BRANCH = 4
BRANCH_WIDTH = 512
FFT_GROUPS = 4
FFT_GROUP_CH = 128
CONV_K = 3
SSM_GROUPS = 32
SSM_CH = 16
SSM_STATE = 64
N_HEADS = 8
N_KV = 2
Q_PER_KV = N_HEADS // N_KV
HEAD_DIM = 64
WINDOW = 128
ATT_BLOCK = 128
ROPE_BASE = 10000.0
PEER_HEADS = 8
N_KEYS = 128
N_EXPERTS = N_KEYS * N_KEYS
PEER_TOPK = 16
KEY_DIM = 256
LN_EPS = 1e-5
NEG_INF = -1e30

Z_COLS = N_BRANCH * D_MODEL + 6 * BRANCH_WIDTH + 2 * N_KV * HEAD_DIM
ZG_BLK = 0
ZF_BLK, ZB_BLK, ZC_BLK, ZH_BLK, ZS_BLK, ZQ_BLK = 8, 9, 10, 11, 12, 13
ZK_BLK, ZV_BLK = 56, 57

V7X_VMEM_LIMIT_BYTES = 56 * 1024 * 1024
SSM_GBLK = 8
SSM_TCHUNK = 256
SSM_BROWS = 8
ROW_TILE = 256
PEER_TT = 512
PEER_EB = 1024
PEER_CHUNKS = 4
ROUTE_TT = 256


def _params(sem):
    return pltpu.CompilerParams(dimension_semantics=sem, vmem_limit_bytes=V7X_VMEM_LIMIT_BYTES)


def _gelu(x):
    return 0.5 * x * (1.0 + jnp.tanh(0.7978845608028654 * (x + 0.044715 * (x * x * x))))


def _layer_norm(h, g, b):
    mu = jnp.mean(h, axis=-1, keepdims=True)
    hc = h - mu
    var = jnp.mean(hc * hc, axis=-1, keepdims=True)
    return hc * lax.rsqrt(var + LN_EPS) * g + b


def _mod_kernel(c_ref, w_ref, b_ref, o_ref):
    cv = c_ref[...]
    s = (cv * jax.nn.sigmoid(cv)).astype(BF16)
    o_ref[0] = jnp.dot(s, w_ref[0].astype(BF16), preferred_element_type=F32) + b_ref[0]


def _modulation(cvecs, w_ada, b_ada):
    depth = w_ada.shape[0]
    nrow = cvecs.shape[0]
    return pl.pallas_call(
        _mod_kernel,
        grid=(depth, 6),
        in_specs=[
            pl.BlockSpec((nrow, D_MODEL), lambda l, j: (0, 0)),
            pl.BlockSpec((1, D_MODEL, D_MODEL), lambda l, j: (l, 0, j)),
            pl.BlockSpec((1, 1, D_MODEL), lambda l, j: (l, 0, j)),
        ],
        out_specs=pl.BlockSpec((1, nrow, D_MODEL), lambda l, j: (l, 0, j)),
        out_shape=jax.ShapeDtypeStruct((depth, nrow, 6 * D_MODEL), F32),
        compiler_params=_params(("parallel", "parallel")),
    )(cvecs, w_ada, b_ada.reshape(depth, 1, 6 * D_MODEL))


def _win_kernel(x_ref, mod_ref, w_ref, z_ref):
    sh = mod_ref[0, 0:1, :]
    sc = mod_ref[0, 1:2, :]
    xm = (x_ref[...] * (1.0 + sc) + sh).astype(BF16)
    z_ref[...] = jnp.dot(xm, w_ref[...], preferred_element_type=F32)


def _in_proj(x, mods, w_in, mod_row, l):
    t = x.shape[0]
    ncol = Z_COLS // 2
    return pl.pallas_call(
        _win_kernel,
        grid=(2, t // ROW_TILE),
        in_specs=[
            pl.BlockSpec((ROW_TILE, D_MODEL), lambda c, i: (i, 0)),
            pl.BlockSpec((None, 1, 6, D_MODEL), lambda c, i: (l, mod_row(i), 0, 0)),
            pl.BlockSpec((None, D_MODEL, ncol), lambda c, i: (l, 0, c)),
        ],
        out_specs=pl.BlockSpec((ROW_TILE, ncol), lambda c, i: (i, c)),
        out_shape=jax.ShapeDtypeStruct((t, Z_COLS), F32),
        compiler_params=_params(("parallel", "parallel")),
    )(x, mods, w_in)


def _fft_conv(zf_ref, zb_ref, zc_ref, zh_ref, cw_ref, dl_ref, dc_ref, yf_ref, yc_ref):
    length = zf_ref.shape[0]
    zf = zf_ref[...].astype(BF16)
    ab = jnp.dot(zf, dc_ref[...], preferred_element_type=F32)
    ab = jnp.concatenate([ab[:, :BRANCH_WIDTH], ab[:, BRANCH_WIDTH:]], axis=0).astype(BF16)
    yf_ref[...] = jnp.dot(dl_ref[...], ab, preferred_element_type=F32).astype(BF16)
    g = zc_ref[...] * zh_ref[...]
    row = lax.broadcasted_iota(jnp.int32, g.shape, 0)
    prev = jnp.where(row == 0, 0.0, pltpu.roll(g, 1, 0))
    nxt = jnp.where(row == length - 1, 0.0, pltpu.roll(g, length - 1, 0))
    conv = cw_ref[0:1, :] * prev + cw_ref[1:2, :] * g + cw_ref[2:3, :] * nxt
    yc_ref[...] = (zb_ref[...] * conv).astype(BF16)


def _softmax_pv(s, sink, v):
    m = jnp.maximum(jnp.max(s, axis=1, keepdims=True), sink)
    p = jnp.exp(s - m)
    den = jnp.sum(p, axis=1, keepdims=True) + jnp.exp(sink - m)
    return jnp.dot(p.astype(BF16), v, preferred_element_type=F32) / den


def _mixer_ctx_kernel(zf_ref, zb_ref, zc_ref, zh_ref, zq_ref, zk_ref, zv_ref, cw_ref, sink_ref,
                      dl_ref, dc_ref, yf_ref, yc_ref, ya_ref):
    _fft_conv(zf_ref, zb_ref, zc_ref, zh_ref, cw_ref, dl_ref, dc_ref, yf_ref, yc_ref)
    q = zq_ref[...] * (HEAD_DIM ** -0.5)
    k = zk_ref[...]
    v = zv_ref[...]
    outs = []
    for h in range(N_HEADS):
        g = h // Q_PER_KV
        qh = q[:, h * HEAD_DIM:(h + 1) * HEAD_DIM].astype(BF16)
        kg = k[:, g * HEAD_DIM:(g + 1) * HEAD_DIM].astype(BF16)
        vg = v[:, g * HEAD_DIM:(g + 1) * HEAD_DIM].astype(BF16)
        s = lax.dot_general(qh, kg, (((1,), (1,)), ((), ())), preferred_element_type=F32)
        outs.append(_softmax_pv(s, sink_ref[0:1, h:h + 1], vg))
    ya_ref[...] = jnp.concatenate(outs, axis=1).astype(BF16)


def _zspec(rows, width, row_fn, col_blk):
    return pl.BlockSpec((rows, width), lambda *a: (row_fn(*a), col_blk))


def _mixer_ctx(z, nb, seq, conv_w, sink, dft_l, dft_c, l):
    rf = lambda b: b
    full = lambda shape: pl.BlockSpec(shape, lambda b: (0,) * len(shape))
    layer = lambda shape: pl.BlockSpec((None,) + shape, lambda b: (l,) + (0,) * len(shape))
    out = jax.ShapeDtypeStruct((z.shape[0], BRANCH_WIDTH), BF16)
    ospec = pl.BlockSpec((seq, BRANCH_WIDTH), lambda b: (b, 0))
    return pl.pallas_call(
        _mixer_ctx_kernel,
        grid=(nb,),
        in_specs=[
            _zspec(seq, 512, rf, ZF_BLK), _zspec(seq, 512, rf, ZB_BLK), _zspec(seq, 512, rf, ZC_BLK),
            _zspec(seq, 512, rf, ZH_BLK), _zspec(seq, 512, rf, ZQ_BLK),
            _zspec(seq, 128, rf, ZK_BLK), _zspec(seq, 128, rf, ZV_BLK),
            layer((CONV_K, BRANCH_WIDTH)), layer((1, N_HEADS)),
            full((seq, 2 * seq)), full((BRANCH_WIDTH, 2 * BRANCH_WIDTH)),
        ],
        out_specs=[ospec, ospec, ospec],
        out_shape=[out, out, out],
        compiler_params=_params(("parallel",)),
    )(z, z, z, z, z, z, z, conv_w, sink, dft_l, dft_c)


def _fftconv_lat_kernel(zf_ref, zb_ref, zc_ref, zh_ref, cw_ref, dl_ref, dc_ref, yf_in, yc_in,
                        yf_ref, yc_ref):
    del yf_in, yc_in
    _fft_conv(zf_ref, zb_ref, zc_ref, zh_ref, cw_ref, dl_ref, dc_ref, yf_ref, yc_ref)


def _fftconv_lat(z, row0, nb, seq, conv_w, dft_l, dft_c, yf, yc, l):
    rf = lambda b: row0 // seq + b
    full = lambda shape: pl.BlockSpec(shape, lambda b: (0,) * len(shape))
    out = jax.ShapeDtypeStruct(yf.shape, BF16)
    ospec = pl.BlockSpec((seq, BRANCH_WIDTH), lambda b: (row0 // seq + b, 0))
    anyspec = pl.BlockSpec(memory_space=pl.ANY)
    return pl.pallas_call(
        _fftconv_lat_kernel,
        grid=(nb,),
        in_specs=[
            _zspec(seq, 512, rf, ZF_BLK), _zspec(seq, 512, rf, ZB_BLK), _zspec(seq, 512, rf, ZC_BLK),
            _zspec(seq, 512, rf, ZH_BLK),
            pl.BlockSpec((None, CONV_K, BRANCH_WIDTH), lambda b: (l, 0, 0)),
            full((seq, 2 * seq)), full((BRANCH_WIDTH, 2 * BRANCH_WIDTH)),
            anyspec, anyspec,
        ],
        out_specs=[ospec, ospec],
        out_shape=[out, out],
        input_output_aliases={7: 0, 8: 1},
        compiler_params=_params(("parallel",)),
    )(z, z, z, z, conv_w, dft_l, dft_c, yf, yc)


def _rope(x, cos, sin):
    lane = lax.broadcasted_iota(jnp.int32, (x.shape[0], 128), 1)
    first = (lane & 31) < 16
    parts = []
    for c in range(x.shape[1] // 128):
        xc = x[:, c * 128:(c + 1) * 128]
        swapped = jnp.where(first, pltpu.roll(xc, 112, 1), pltpu.roll(xc, 16, 1))
        parts.append(xc * cos[:, c * 128:(c + 1) * 128] + swapped * sin[:, c * 128:(c + 1) * 128])
    return parts[0] if len(parts) == 1 else jnp.concatenate(parts, axis=1)


def _attn_lat_kernel(zq_ref, zk_ref, zv_ref, ck_ref, cv_ref, cosq_ref, sinq_ref, cosk_ref, sinkk_ref,
                     sink_ref, ya_in, ya_ref):
    del ya_in
    n = pl.program_id(1)
    nblk = pl.num_programs(1)
    nwin = 3 * ATT_BLOCK
    q = _rope(zq_ref[...], cosq_ref[...], sinq_ref[...]) * (HEAD_DIM ** -0.5)
    ws = pl.multiple_of(jnp.clip(n - 1, 0, nblk - 3) * ATT_BLOCK, ATT_BLOCK)
    kw = _rope(zk_ref[pl.ds(ws, nwin), :], cosk_ref[pl.ds(ws, nwin), :], sinkk_ref[pl.ds(ws, nwin), :])
    vw = zv_ref[pl.ds(ws, nwin), :]
    k_all = jnp.concatenate([kw, ck_ref[...]], axis=0)
    v_all = jnp.concatenate([vw, cv_ref[...]], axis=0)
    nkey = k_all.shape[0]
    qpos = n * ATT_BLOCK + lax.broadcasted_iota(jnp.int32, (ATT_BLOCK, nkey), 0)
    col = lax.broadcasted_iota(jnp.int32, (ATT_BLOCK, nkey), 1)
    valid = (jnp.abs(qpos - (ws + col)) <= WINDOW) | (col >= nwin)
    outs = []
    for h in range(N_HEADS):
        g = h // Q_PER_KV
        qh = q[:, h * HEAD_DIM:(h + 1) * HEAD_DIM].astype(BF16)
        kg = k_all[:, g * HEAD_DIM:(g + 1) * HEAD_DIM].astype(BF16)
        vg = v_all[:, g * HEAD_DIM:(g + 1) * HEAD_DIM].astype(BF16)
        s = lax.dot_general(qh, kg, (((1,), (1,)), ((), ())), preferred_element_type=F32)
        s = jnp.where(valid, s, NEG_INF)
        outs.append(_softmax_pv(s, sink_ref[0:1, h:h + 1], vg))
    ya_ref[...] = jnp.concatenate(outs, axis=1).astype(BF16)


def _attn_lat(z, row0, nb, seq, ck, cv, cosq, sinq, cosk, sin_k, sink, ya, l):
    nblk = seq // ATT_BLOCK
    kvw = N_KV * HEAD_DIM
    past = ck.shape[2]
    full = lambda shape: pl.BlockSpec(shape, lambda b, n: (0,) * len(shape))
    cache = pl.BlockSpec((None, None, past, kvw), lambda b, n: (b, l, 0, 0))
    return pl.pallas_call(
        _attn_lat_kernel,
        grid=(nb, nblk),
        in_specs=[
            pl.BlockSpec((ATT_BLOCK, 512), lambda b, n: (row0 // ATT_BLOCK + b * nblk + n, ZQ_BLK)),
            pl.BlockSpec((seq, kvw), lambda b, n: (row0 // seq + b, ZK_BLK)),
            pl.BlockSpec((seq, kvw), lambda b, n: (row0 // seq + b, ZV_BLK)),
            cache, cache,
            pl.BlockSpec((ATT_BLOCK, 512), lambda b, n: (n, 0)),
            pl.BlockSpec((ATT_BLOCK, 512), lambda b, n: (n, 0)),
            full((seq, kvw)), full((seq, kvw)),
            pl.BlockSpec((None, 1, N_HEADS), lambda b, n: (l, 0, 0)),
            pl.BlockSpec(memory_space=pl.ANY),
        ],
        out_specs=pl.BlockSpec((ATT_BLOCK, 512), lambda b, n: (row0 // ATT_BLOCK + b * nblk + n, 0)),
        out_shape=jax.ShapeDtypeStruct(ya.shape, BF16),
        input_output_aliases={10: 0},
        compiler_params=_params(("parallel", "parallel")),
    )(z, z, z, ck, cv, cosq, sinq, cosk, sin_k, sink, ya)


def _s5_kernel(uf_ref, ub_ref, h0f_ref, h0b_ref, wbf_ref, wbb_ref, cf_ref, cb_ref, af_ref, ab_ref,
               yf_ref, yb_ref, hf_ref, hb_ref, buff, bufb, hst):
    c = pl.program_id(2)
    half = SSM_GBLK * SSM_STATE
    steps = uf_ref.shape[0] // SSM_BROWS

    @pl.when(c == 0)
    def _():
        hst[0] = h0f_ref[0]
        hst[1] = h0b_ref[0]

    buff[...] = jnp.dot(uf_ref[...].astype(BF16), wbf_ref[...], preferred_element_type=F32)
    bufb[...] = jnp.dot(ub_ref[...].astype(BF16), wbb_ref[...], preferred_element_type=F32)
    afr = jnp.broadcast_to(af_ref[0:1, :], (SSM_BROWS, half))
    afi = jnp.broadcast_to(af_ref[1:2, :], (SSM_BROWS, half))
    abr = jnp.broadcast_to(ab_ref[0:1, :], (SSM_BROWS, half))
    abi = jnp.broadcast_to(ab_ref[1:2, :], (SSM_BROWS, half))

    def step(t, carry):
        hfr, hfi, hbr, hbi = carry
        rf = pl.multiple_of(t * SSM_BROWS, SSM_BROWS)
        nfr = afr * hfr - afi * hfi + buff[pl.ds(rf, SSM_BROWS), 0:half]
        nfi = afr * hfi + afi * hfr + buff[pl.ds(rf, SSM_BROWS), half:2 * half]
        buff[pl.ds(rf, SSM_BROWS), 0:half] = nfr
        buff[pl.ds(rf, SSM_BROWS), half:2 * half] = nfi
        rb = pl.multiple_of((steps - 1 - t) * SSM_BROWS, SSM_BROWS)
        nbr = abr * hbr - abi * hbi + bufb[pl.ds(rb, SSM_BROWS), 0:half]
        nbi = abr * hbi + abi * hbr + bufb[pl.ds(rb, SSM_BROWS), half:2 * half]
        bufb[pl.ds(rb, SSM_BROWS), 0:half] = nbr
        bufb[pl.ds(rb, SSM_BROWS), half:2 * half] = nbi
        return nfr, nfi, nbr, nbi

    init = (hst[0, :, 0:half], hst[0, :, half:2 * half], hst[1, :, 0:half], hst[1, :, half:2 * half])
    hfr, hfi, hbr, hbi = lax.fori_loop(0, steps, step, init, unroll=4)
    hst[0, :, 0:half] = hfr
    hst[0, :, half:2 * half] = hfi
    hst[1, :, 0:half] = hbr
    hst[1, :, half:2 * half] = hbi
    yf_ref[...] = jnp.dot(buff[...].astype(BF16), cf_ref[...], preferred_element_type=F32)
    yb_ref[...] = jnp.dot(bufb[...].astype(BF16), cb_ref[...], preferred_element_type=F32)

    @pl.when(c == pl.num_programs(2) - 1)
    def _():
        hf_ref[0] = hst[0]
        hb_ref[0] = hst[1]


def _s5(u_tm, h0f, h0b, sp, nbb, nchunk, l):
    rows = SSM_TCHUNK * SSM_BROWS
    ngb = SSM_GROUPS // SSM_GBLK
    width = 2 * SSM_GBLK * SSM_STATE
    nbrow = nbb * SSM_BROWS
    cw = SSM_GBLK * SSM_CH
    fwd = lambda bb, j, c: (bb * nchunk + c, j)
    bwd = lambda bb, j, c: (bb * nchunk + nchunk - 1 - c, j)
    par = lambda shape, d: pl.BlockSpec((None, None, None) + shape, lambda bb, j, c: (l, d, j, 0, 0))
    st = pl.BlockSpec((1, SSM_BROWS, width), lambda bb, j, c: (j, bb, 0))
    wb, cm, a = sp
    ysh = jax.ShapeDtypeStruct(u_tm.shape, F32)
    hsh = jax.ShapeDtypeStruct((ngb, nbrow, width), F32)
    return pl.pallas_call(
        _s5_kernel,
        grid=(nbb, ngb, nchunk),
        in_specs=[
            pl.BlockSpec((rows, cw), fwd), pl.BlockSpec((rows, cw), bwd), st, st,
            par((cw, width), 0), par((cw, width), 1), par((width, cw), 0), par((width, cw), 1),
            par((2, width // 2), 0), par((2, width // 2), 1),
        ],
        out_specs=[pl.BlockSpec((rows, cw), fwd), pl.BlockSpec((rows, cw), bwd), st, st],
        out_shape=[ysh, ysh, hsh, hsh],
        scratch_shapes=[pltpu.VMEM((rows, width), F32), pltpu.VMEM((rows, width), F32),
                        pltpu.VMEM((2, SSM_BROWS, width), F32)],
        compiler_params=_params(("parallel", "parallel", "arbitrary")),
    )(u_tm, u_tm, h0f, h0b, wb, wb, cm, cm, a, a)


def _s5_params(lam_re, lam_im, log_step, b_re, b_im, c_re, c_im):
    lead = lam_re.shape[:-2]
    dt = jnp.exp(log_step)[..., None]
    mag = jnp.exp(lam_re * dt)
    ar = mag * jnp.cos(lam_im * dt)
    ai = mag * jnp.sin(lam_im * dt)
    den = lam_re * lam_re + lam_im * lam_im
    kr = ((ar - 1.0) * lam_re + ai * lam_im) / den
    ki = (ai * lam_re - (ar - 1.0) * lam_im) / den
    bbr = kr[..., None] * b_re - ki[..., None] * b_im
    bbi = kr[..., None] * b_im + ki[..., None] * b_re
    ngb = SSM_GROUPS // SSM_GBLK
    eye = jnp.eye(SSM_GBLK, dtype=F32)

    def blockdiag_in(m):
        m = m.reshape(lead + (ngb, SSM_GBLK, SSM_STATE, SSM_CH))
        m = jnp.einsum("...jgph,gk->...jghkp", m, eye)
        return m.reshape(lead + (ngb, SSM_GBLK * SSM_CH, SSM_GBLK * SSM_STATE))

    def blockdiag_out(m):
        m = m.reshape(lead + (ngb, SSM_GBLK, SSM_CH, SSM_STATE))
        m = jnp.einsum("...jghp,gk->...jgpkh", m, eye)
        return m.reshape(lead + (ngb, SSM_GBLK * SSM_STATE, SSM_GBLK * SSM_CH))

    wb = jnp.concatenate([blockdiag_in(bbr), blockdiag_in(bbi)], axis=-1).astype(BF16)
    cm = jnp.concatenate([blockdiag_out(c_re), -blockdiag_out(c_im)], axis=-2).astype(BF16)
    a = jnp.stack([ar.reshape(lead + (ngb, -1)), ai.reshape(lead + (ngb, -1))], axis=-2)
    return wb, cm, a


def _to_time_major(u, nb, seq):
    cdim = u.shape[1]
    nbp = -(-nb // SSM_BROWS) * SSM_BROWS
    u = u.reshape(nb, seq, cdim)
    if nbp != nb:
        u = jnp.pad(u, ((0, nbp - nb), (0, 0), (0, 0)))
    u = u.reshape(nbp // SSM_BROWS, SSM_BROWS, seq, cdim).transpose(0, 2, 1, 3)
    return u.reshape(nbp * seq, cdim), nbp


def _from_time_major(y, nb, nbp, seq):
    cdim = y.shape[1]
    y = y.reshape(nbp // SSM_BROWS, seq, SSM_BROWS, cdim).transpose(0, 2, 1, 3)
    return y.reshape(nbp, seq, cdim)[:nb].reshape(nb * seq, cdim)


def _state_to_blocks(re, im, nbp):
    nb = re.shape[0]
    ngb = SSM_GROUPS // SSM_GBLK
    def blk(x):
        return x.reshape(nb, ngb, SSM_GBLK * SSM_STATE).transpose(1, 0, 2)
    h = jnp.concatenate([blk(re), blk(im)], axis=2)
    if nbp != nb:
        h = jnp.pad(h, ((0, 0), (0, nbp - nb), (0, 0)))
    return h


def _blocks_to_state(h, nb):
    half = SSM_GBLK * SSM_STATE
    def unblk(x):
        return x[:, :nb].transpose(1, 0, 2).reshape(nb, SSM_GROUPS, SSM_STATE)
    return unblk(h[:, :, :half]), unblk(h[:, :, half:])


def _merge_kernel(x_ref, mod_ref, yf_ref, yc_ref, ya_ref, ysf_ref, ysb_ref, zs_ref,
                  zg0_ref, zg1_ref, zg2_ref, zg3_ref, d_ref, wglu_ref, wb_ref, wout_ref,
                  g_ref, b_ref, x1_ref, xm_ref):
    ys = ysf_ref[...] + ysb_ref[...] + d_ref[...] * zs_ref[...]
    ys = _gelu(ys)
    yssm = ys * jax.nn.sigmoid(jnp.dot(ys.astype(BF16), wglu_ref[...], preferred_element_type=F32))
    acc = jax.nn.sigmoid(zg0_ref[...]) * jnp.dot(yf_ref[...], wb_ref[0], preferred_element_type=F32)
    acc += jax.nn.sigmoid(zg1_ref[...]) * jnp.dot(yc_ref[...], wb_ref[1], preferred_element_type=F32)
    acc += jax.nn.sigmoid(zg2_ref[...]) * jnp.dot(yssm.astype(BF16), wb_ref[2], preferred_element_type=F32)
    acc += jax.nn.sigmoid(zg3_ref[...]) * jnp.dot(ya_ref[...], wb_ref[3], preferred_element_type=F32)
    mix = jnp.dot(acc.astype(BF16), wout_ref[...], preferred_element_type=F32)
    alpha = (2 * 4) ** 0.25
    x1 = _layer_norm(alpha * x_ref[...] + mod_ref[0, 2:3, :] * mix, g_ref[...], b_ref[...])
    x1_ref[...] = x1
    xm_ref[...] = (x1 * (1.0 + mod_ref[0, 4:5, :]) + mod_ref[0, 3:4, :]).astype(BF16)


def _merge(x, mods, mod_row, yf, yc, ya, ysf, ysb, z, ssm_d, w_glu, w_branch, w_out, ln_g, ln_b, l):
    t = x.shape[0]
    rf = lambda i: i
    row = lambda w: pl.BlockSpec((ROW_TILE, w), lambda i: (i, 0))
    full = lambda shape: pl.BlockSpec((None,) + shape, lambda i: (l,) + (0,) * len(shape))
    return pl.pallas_call(
        _merge_kernel,
        grid=(t // ROW_TILE,),
        in_specs=[
            row(D_MODEL), pl.BlockSpec((None, 1, 6, D_MODEL), lambda i: (l, mod_row(i), 0, 0)),
            row(512), row(512), row(512), row(512), row(512),
            _zspec(ROW_TILE, 512, rf, ZS_BLK),
            _zspec(ROW_TILE, 1024, rf, 0), _zspec(ROW_TILE, 1024, rf, 1),
            _zspec(ROW_TILE, 1024, rf, 2), _zspec(ROW_TILE, 1024, rf, 3),
            full((1, 512)), full((512, 512)), full((N_BRANCH, 512, D_MODEL)), full((D_MODEL, D_MODEL)),
            full((1, D_MODEL)), full((1, D_MODEL)),
        ],
        out_specs=[row(D_MODEL), row(D_MODEL)],
        out_shape=[jax.ShapeDtypeStruct((t, D_MODEL), F32), jax.ShapeDtypeStruct((t, D_MODEL), BF16)],
        compiler_params=_params(("parallel",)),
    )(x, mods, yf, yc, ya, ysf, ysb, z, z, z, z, z, ssm_d, w_glu, w_branch, w_out, ln_g, ln_b)


def _top16(s):
    n, w = s.shape
    iota = lax.broadcasted_iota(jnp.int32, (n, w), 0).astype(F32)
    kio = lax.broadcasted_iota(jnp.int32, (PEER_TOPK, w), 0)

    def body(k, carry):
        work, rank, vals, _ = carry
        m = jnp.max(work, axis=0, keepdims=True)
        pos = jnp.min(jnp.where(work == m, iota, float(n)), axis=0, keepdims=True)
        hit = iota == pos
        rank = jnp.where(hit, lax.convert_element_type(k, F32), rank)
        work = jnp.where(hit, -jnp.inf, work)
        vals = jnp.where(kio == k, m, vals)
        return work, rank, vals, pos

    init = (s, jnp.full((n, w), 1e9, F32), jnp.zeros((PEER_TOPK, w), F32), jnp.zeros((1, w), F32))
    _, rank, vals, pos = lax.fori_loop(0, PEER_TOPK, body, init)
    return vals, rank, pos


def _max16(s):
    w = s.shape[1]
    kio = lax.broadcasted_iota(jnp.int32, (PEER_TOPK, w), 0)

    def body(k, carry):
        work, vals = carry
        m = jnp.max(work, axis=0, keepdims=True)
        return jnp.where(work == m, -jnp.inf, work), jnp.where(kio == k, m, vals)

    _, vals = lax.fori_loop(0, PEER_TOPK, body, (s, jnp.zeros((PEER_TOPK, w), F32)))
    return vals


_STAIR = [(j, PEER_TOPK // (j + 1)) for j in range(PEER_TOPK)]
_STAIR_ROWS = -(-sum(k for _, k in _STAIR) // 8) * 8


def _stair_candidates(v1, v2):
    w = v1.shape[1]
    rows = [v1[j:j + 1] + v2[0:k] for j, k in _STAIR]
    npad = _STAIR_ROWS - sum(k for _, k in _STAIR)
    return jnp.concatenate(rows + [jnp.full((npad, w), -jnp.inf, F32)], axis=0)


def _stair_positions(w):
    rows = [float(PEER_TOPK * j) + lax.broadcasted_iota(jnp.int32, (k, w), 0).astype(F32) for j, k in _STAIR]
    npad = _STAIR_ROWS - sum(k for _, k in _STAIR)
    return jnp.concatenate(rows + [jnp.full((npad, w), 1e9, F32)], axis=0)


def _next_up(x):
    b = lax.bitcast_convert_type(x, jnp.int32)
    up = jnp.where(x > 0.0, b + 1, jnp.where(x < 0.0, b - 1, jnp.int32(0x00800000)))
    return lax.bitcast_convert_type(up, F32)


def _route_kernel(xm_ref, wq_ref, keys_ref, s1m_ref, qrow_ref, e1_ref, s2m_ref, pb_ref, e2_ref,
                  thr_ref, qs):
    qs[...] = lax.dot_general(wq_ref[...], xm_ref[...], (((1,), (1,)), ((), ())),
                              preferred_element_type=F32)
    w = xm_ref.shape[0]

    def count(mask):
        return jnp.sum(mask.astype(F32), axis=0, keepdims=True)

    def emit(h, s1, s2, in1, in2, m1, m2, vc, qrow, pb, thr_up, bad):
        z = jnp.sum(jnp.exp(vc - vc[0:1]), axis=0, keepdims=True)
        s1m_ref[h] = jnp.where(in1, s1, -jnp.inf)
        s2m_ref[h] = jnp.where(in2, s2, -jnp.inf)
        e1_ref[h] = jnp.where(in1, jnp.exp(s1 - m1), 0.0) / z
        e2_ref[h] = jnp.where(in2, jnp.exp(s2 - m2), 0.0)
        qrow_ref[h] = qrow
        pb_ref[h] = pb
        thr = vc[PEER_TOPK - 1:PEER_TOPK]
        thr_ref[h] = jnp.concatenate([thr, thr_up, bad, jnp.zeros((5, w), F32)], axis=0)

    def head(h, carry):
        base = pl.multiple_of(h * KEY_DIM, KEY_DIM)
        q1 = qs[pl.ds(base, N_KEYS), :].astype(BF16)
        q2 = qs[pl.ds(base + N_KEYS, N_KEYS), :].astype(BF16)
        s1 = jnp.dot(keys_ref[2 * h], q1, preferred_element_type=F32)
        s2 = jnp.dot(keys_ref[2 * h + 1], q2, preferred_element_type=F32)

        v1 = _max16(s1)
        v2 = _max16(s2)
        in1 = s1 >= v1[PEER_TOPK - 1:PEER_TOPK]
        in2 = s2 >= v2[PEER_TOPK - 1:PEER_TOPK]
        cand = _stair_candidates(v1, v2)
        vc = _max16(cand)
        thr = vc[PEER_TOPK - 1:PEER_TOPK]
        zero = jnp.zeros((N_KEYS, w), F32)
        k = float(PEER_TOPK)
        bad = jnp.abs(count(in1) - k) + jnp.abs(count(in2) - k) + jnp.abs(count(cand >= thr) - k)
        emit(h, s1, s2, in1, in2, v1[0:1], v2[0:1], vc, zero, zero, thr, bad)

        @pl.when(jnp.max(bad) > 0.0)
        def _():
            xv1, r1, _ = _top16(s1)
            xv2, r2, _ = _top16(s2)
            xcand = _stair_candidates(xv1, xv2)
            xvc, _, prow = _top16(xcand)
            riota = lax.broadcasted_iota(jnp.int32, xcand.shape, 0).astype(F32)
            pthr = jnp.sum(jnp.where(riota == prow, _stair_positions(w), 0.0), axis=0, keepdims=True)
            emit(h, s1, s2, r1 < 100.0, r2 < 100.0, xv1[0:1], xv2[0:1], xvc,
                 pthr - k * r1, r2, _next_up(xvc[PEER_TOPK - 1:PEER_TOPK]), bad)

        return carry

    lax.fori_loop(0, PEER_HEADS, head, 0)


def _route(xm, wq_t, keys, l):
    t = xm.shape[0]
    big = jax.ShapeDtypeStruct((PEER_HEADS, N_KEYS, t), F32)
    bspec = pl.BlockSpec((PEER_HEADS, N_KEYS, ROUTE_TT), lambda i: (0, 0, i))
    return pl.pallas_call(
        _route_kernel,
        grid=(t // ROUTE_TT,),
        in_specs=[
            pl.BlockSpec((ROUTE_TT, D_MODEL), lambda i: (i, 0)),
            pl.BlockSpec((None, PEER_HEADS * KEY_DIM, D_MODEL), lambda i: (l, 0, 0)),
            pl.BlockSpec((None, 2 * PEER_HEADS, N_KEYS, N_KEYS), lambda i: (l, 0, 0, 0)),
        ],
        out_specs=[bspec] * 6 + [pl.BlockSpec((PEER_HEADS, 8, ROUTE_TT), lambda i: (0, 0, i))],
        out_shape=[big] * 6 + [jax.ShapeDtypeStruct((PEER_HEADS, 8, t), F32)],
        scratch_shapes=[pltpu.VMEM((PEER_HEADS * KEY_DIM, ROUTE_TT), F32)],
        compiler_params=_params(("parallel",)),
    )(xm, wq_t, keys)


def _peer_kernel(flag_ref, xm_ref, u_ref, vt_ref, s1m_ref, qrow_ref, e1_ref, s2m_ref, pb_ref, e2_ref,
                 thr_ref, x1_ref, mod_ref, g_ref, b_ref, o_ref, ht, wacc, pt, acc):
    j = pl.program_id(1)
    nrow = PEER_EB // N_KEYS

    @pl.when(j == 0)
    def _():
        acc[...] = jnp.zeros_like(acc)

    i1_0 = pl.multiple_of(j * nrow, nrow)
    heads_per_chunk = PEER_HEADS // PEER_CHUNKS
    rpass = 4

    def rows_of(tile, r0):
        return jnp.stack([jnp.broadcast_to(tile[r:r + 1, :], (8, 128)) for r in range(r0, r0 + rpass)])

    def chunk_work(c, exact):
        for hh in range(heads_per_chunk):
            h = c * heads_per_chunk + hh
            for lg in range(PEER_TT // 128):
                lanes = slice(lg * 128, (lg + 1) * 128)
                thr = thr_ref[h, 0:1, lanes]
                s1t = s1m_ref[h, pl.ds(i1_0, nrow), lanes]
                e1t = e1_ref[h, pl.ds(i1_0, nrow), lanes]
                if exact:
                    thr_up = thr_ref[h, 1:2, lanes]
                    qrt = qrow_ref[h, pl.ds(i1_0, nrow), lanes]
                for rp in range(0, nrow, rpass):
                    s1r = rows_of(s1t, rp)
                    e1r = rows_of(e1t, rp)
                    if exact:
                        qr = rows_of(qrt, rp)
                    for v in range(N_KEYS // 8):
                        sub = slice(v * 8, (v + 1) * 8)
                        sc = s1r + s2m_ref[h, sub, lanes][None]
                        if exact:
                            first = pb_ref[h, sub, lanes][None] <= qr
                            sel = sc >= jnp.where(first, thr, thr_up)
                        else:
                            sel = sc >= thr
                        gate = jnp.where(sel, e1r * e2_ref[h, sub, lanes][None], 0.0)
                        if h == 0:
                            wacc[rp:rp + rpass, sub, lanes] = gate
                        else:
                            wacc[rp:rp + rpass, sub, lanes] += gate

    hfull = lax.dot_general(u_ref[...], xm_ref[...], (((1,), (1,)), ((), ())),
                            preferred_element_type=F32)
    for r in range(nrow):
        ht[r] = hfull[r * N_KEYS:(r + 1) * N_KEYS, :]
    for c in range(PEER_CHUNKS):
        needs_ties = flag_ref[pl.program_id(0), c] > 0
        pl.when(needs_ties)(functools.partial(chunk_work, c, True))
        pl.when(jnp.logical_not(needs_ties))(functools.partial(chunk_work, c, False))
    for lg in range(PEER_TT // 128):
        lanes = slice(lg * 128, (lg + 1) * 128)
        for r in range(nrow):
            pt[r * N_KEYS:(r + 1) * N_KEYS, lanes] = (
                wacc[r, :, lanes] * _gelu(ht[r, :, lanes])).astype(BF16)
    acc[...] += jnp.dot(vt_ref[...], pt[...], preferred_element_type=F32)

    @pl.when(j == pl.num_programs(1) - 1)
    def _():
        alpha = (2 * 4) ** 0.25
        ff = acc[...].T
        o_ref[...] = _layer_norm(alpha * x1_ref[...] + mod_ref[0, 5:6, :] * ff, g_ref[...], b_ref[...])


def _peer(xm, u_b, vt_b, routing, x1, mods, mod_row_tt, ln_g, ln_b, l):
    t = xm.shape[0]
    tok = lambda w: pl.BlockSpec((PEER_TT, w), lambda i, j: (i, 0))
    rspec = pl.BlockSpec((PEER_HEADS, N_KEYS, PEER_TT), lambda i, j: (0, 0, i))
    full = lambda shape: pl.BlockSpec((None,) + shape, lambda i, j: (l,) + (0,) * len(shape))
    nblk = N_EXPERTS // PEER_EB
    bad = routing[6][:, 2, :].reshape(PEER_CHUNKS, PEER_HEADS // PEER_CHUNKS, t // PEER_TT, PEER_TT)
    flags = (jnp.max(bad, axis=(1, 3)) > 0.0).astype(jnp.int32).T
    return pl.pallas_call(
        _peer_kernel,
        grid=(t // PEER_TT, nblk),
        in_specs=[
            pl.BlockSpec(memory_space=pltpu.SMEM),
            tok(D_MODEL),
            pl.BlockSpec((None, PEER_EB, D_MODEL), lambda i, j: (l, j, 0)),
            pl.BlockSpec((None, D_MODEL, PEER_EB), lambda i, j: (l, 0, j)),
            rspec, rspec, rspec, rspec, rspec, rspec,
            pl.BlockSpec((PEER_HEADS, 8, PEER_TT), lambda i, j: (0, 0, i)),
            tok(D_MODEL),
            pl.BlockSpec((None, 1, 6, D_MODEL), lambda i, j: (l, mod_row_tt(i), 0, 0)),
            full((1, D_MODEL)), full((1, D_MODEL)),
        ],
        out_specs=tok(D_MODEL),
        out_shape=jax.ShapeDtypeStruct((t, D_MODEL), F32),
        scratch_shapes=[pltpu.VMEM((PEER_EB // N_KEYS, N_KEYS, PEER_TT), F32),
                        pltpu.VMEM((PEER_EB // N_KEYS, N_KEYS, PEER_TT), F32),
                        pltpu.VMEM((PEER_EB, PEER_TT), BF16),
                        pltpu.VMEM((D_MODEL, PEER_TT), F32)],
        compiler_params=_params(("parallel", "arbitrary")),
    )(flags, xm, u_b, vt_b, *routing, x1, mods, ln_g, ln_b)


def _dft_tables(length):
    n = np.arange(length)
    ang = 2.0 * np.pi * ((n[:, None] * n[None, :]) % length) / length
    dl = np.concatenate([np.cos(ang), -np.sin(ang)], axis=1) / math.sqrt(length)
    c = np.arange(FFT_GROUP_CH)
    angc = 2.0 * np.pi * ((c[:, None] * c[None, :]) % FFT_GROUP_CH) / FFT_GROUP_CH
    eye = np.eye(FFT_GROUPS)
    dc = np.concatenate([np.kron(eye, np.cos(angc)), np.kron(eye, np.sin(angc))], axis=1)
    dc = dc / math.sqrt(FFT_GROUP_CH)
    return jnp.asarray(dl, BF16), jnp.asarray(dc, BF16)


def _rope_tables(length, nheads):
    t = np.arange(length)
    pos = np.stack([t // GRID_W, t % GRID_W], axis=1).astype(np.float32)
    n_freq = HEAD_DIM // 4
    inv = (1.0 / (ROPE_BASE ** (np.arange(n_freq, dtype=np.float32) / n_freq))).astype(np.float32)
    ang = pos[:, :, None] * inv[None, None, :]
    cos = np.repeat(np.cos(ang)[:, :, None, :], 2, axis=2).reshape(length, HEAD_DIM)
    sin = np.sin(ang)
    sin = np.stack([-sin, sin], axis=2).reshape(length, HEAD_DIM)
    return (jnp.asarray(np.tile(cos, (1, nheads)), F32), jnp.asarray(np.tile(sin, (1, nheads)), F32))


def kernel(x_prompt, x_sample, cache_k, cache_v, state_ssm_re, state_ssm_im, c, c_ctx, w_ada, b_ada, w_in, conv_w, ssm_lam_re, ssm_lam_im, ssm_log_step, ssm_b_re, ssm_b_im, ssm_c_re, ssm_c_im, ssm_d, ssm_w_glu, attn_sink, w_branch, w_out, ln1_g, ln1_b, ln2_g, ln2_b, peer_wq, peer_subkeys, peer_u, peer_v):
    nb, seq, _ = x_prompt.shape
    nd, lseq, _ = x_sample.shape
    depth = w_in.shape[0]
    t_ctx = nb * seq
    t_all = t_ctx + nd * lseq
    assert t_ctx % lseq == 0 and t_all % PEER_TT == 0 and (2 * seq) % PEER_TT == 0
    assert lseq % SSM_TCHUNK == 0 and seq == SSM_TCHUNK

    x = jnp.concatenate([x_prompt.reshape(t_ctx, D_MODEL), x_sample.reshape(nd * lseq, D_MODEL)], axis=0)

    nrow = -(-(1 + nd) // 8) * 8
    cvecs = jnp.concatenate([c_ctx[None, :], c, jnp.zeros((nrow - 1 - nd, D_MODEL), F32)], axis=0)
    mods_all = _modulation(cvecs, w_ada, b_ada).reshape(depth, nrow, 6, D_MODEL)

    def mod_row_for(tile):
        nctx = t_ctx // tile
        per = lseq // tile
        return lambda i: jnp.where(i < nctx, 0, 1 + (i - nctx) // per)

    mod_row = mod_row_for(ROW_TILE)
    mod_row_tt = mod_row_for(PEER_TT)

    dl_ctx, dft_c = _dft_tables(seq)
    dl_lat, _ = _dft_tables(lseq)
    cosq, sinq = _rope_tables(lseq, N_HEADS)
    cosk, sin_k = _rope_tables(lseq, N_KV)

    gate0 = sum((512,) * 6) + 2 * N_KV * HEAD_DIM
    w_in_b = jnp.concatenate([w_in[:, :, gate0:], w_in[:, :, :gate0]], axis=2).astype(BF16)
    w_glu_b = ssm_w_glu.astype(BF16)
    w_branch_b = w_branch.astype(BF16)
    w_out_b = w_out.astype(BF16)
    wq_t = peer_wq.transpose(0, 2, 1).astype(BF16)
    keys = peer_subkeys.reshape(depth, 2 * PEER_HEADS, N_KEYS, KEY_DIM // 2).astype(BF16)
    u_b = peer_u.astype(BF16)
    vt_b = peer_v.transpose(0, 2, 1).astype(BF16)
    sp = _s5_params(ssm_lam_re, ssm_lam_im, ssm_log_step, ssm_b_re, ssm_b_im, ssm_c_re, ssm_c_im)
    sink = attn_sink.reshape(depth, 1, N_HEADS)
    ssm_d3 = ssm_d.reshape(depth, 1, -1)
    ln1_g3, ln1_b3 = ln1_g.reshape(depth, 1, -1), ln1_b.reshape(depth, 1, -1)
    ln2_g3, ln2_b3 = ln2_g.reshape(depth, 1, -1), ln2_b.reshape(depth, 1, -1)
    ck = cache_k.reshape(nd, depth, -1, N_KV * HEAD_DIM)
    cv = cache_v.reshape(nd, depth, -1, N_KV * HEAD_DIM)
    nbp_c = -(-nb // SSM_BROWS) * SSM_BROWS
    nbp_l = -(-nd // SSM_BROWS) * SSM_BROWS
    zero_state = jnp.zeros((SSM_GROUPS // SSM_GBLK, nbp_c, 2 * SSM_GBLK * SSM_STATE), F32)

    new_k, new_v, new_re, new_im = [], [], [], []
    for l in range(depth):
        z = _in_proj(x, mods_all, w_in_b, mod_row, l)

        yf, yc, ya = _mixer_ctx(z, nb, seq, conv_w, sink, dl_ctx, dft_c, l)
        yf, yc = _fftconv_lat(z, t_ctx, nd, lseq, conv_w, dl_lat, dft_c, yf, yc, l)
        ya = _attn_lat(z, t_ctx, nd, lseq, ck, cv, cosq, sinq, cosk, sin_k, sink, ya, l)

        zs = z[:, ZS_BLK * 512:(ZS_BLK + 1) * 512]
        u_c, _ = _to_time_major(zs[:t_ctx], nb, seq)
        ysf_c, ysb_c, hf_c, hb_c = _s5(u_c, zero_state, zero_state, sp, nbp_c // SSM_BROWS,
                                       seq // SSM_TCHUNK, l)
        u_l, _ = _to_time_major(zs[t_ctx:], nd, lseq)
        h0f = _state_to_blocks(state_ssm_re[:, l, 0], state_ssm_im[:, l, 0], nbp_l)
        h0b = _state_to_blocks(state_ssm_re[:, l, 1], state_ssm_im[:, l, 1], nbp_l)
        ysf_l, ysb_l, _, _ = _s5(u_l, h0f, h0b, sp, nbp_l // SSM_BROWS, lseq // SSM_TCHUNK, l)
        ysf = jnp.concatenate([_from_time_major(ysf_c, nb, nbp_c, seq), _from_time_major(ysf_l, nd, nbp_l, lseq)], axis=0)
        ysb = jnp.concatenate([_from_time_major(ysb_c, nb, nbp_c, seq), _from_time_major(ysb_l, nd, nbp_l, lseq)], axis=0)

        x1, xm2 = _merge(x, mods_all, mod_row, yf, yc, ya, ysf, ysb, z,
                         ssm_d3, w_glu_b, w_branch_b, w_out_b, ln1_g3, ln1_b3, l)

        routing = _route(xm2, wq_t, keys, l)
        x = _peer(xm2, u_b, vt_b, routing, x1, mods_all, mod_row_tt, ln2_g3, ln2_b3, l)

        kv = z[:t_ctx, ZK_BLK * 128:(ZV_BLK + 1) * 128].reshape(nb, seq, 2, N_KV, HEAD_DIM)
        new_k.append(kv[:, :, 0])
        new_v.append(kv[:, :, 1])
        fre, fim = _blocks_to_state(hf_c, nb)
        bre, bim = _blocks_to_state(hb_c, nb)
        new_re.append(jnp.stack([fre, bre], axis=1))
        new_im.append(jnp.stack([fim, bim], axis=1))

    return (x[:t_ctx].reshape(nb, seq, D_MODEL), x[t_ctx:].reshape(nd, lseq, D_MODEL),
            jnp.stack(new_k, axis=1), jnp.stack(new_v, axis=1),
            jnp.stack(new_re, axis=1), jnp.stack(new_im, axis=1))
```

```python
import functools
import math

import numpy as np
import jax
import jax.numpy as jnp
from jax import lax
from jax.experimental import pallas as pl
from jax.experimental.pallas import tpu as pltpu

F32 = jnp.float32
BF16 = jnp.bfloat16

D_MODEL = 1024
GRID_W = 64
N_BRANCH = 4
BRANCH_WIDTH = 512
FFT_GROUPS = 4
FFT_GROUP_CH = 128
CONV_K = 3
SSM_GROUPS = 32
SSM_CH = 16
SSM_STATE = 64
N_HEADS = 8
N_KV = 2
Q_PER_KV = N_HEADS // N_KV
HEAD_DIM = 64
WINDOW = 128
ATT_BLOCK = 128
ROPE_BASE = 10000.0
PEER_HEADS = 8
N_KEYS = 128
N_EXPERTS = N_KEYS * N_KEYS
PEER_TOPK = 16
KEY_DIM = 256
LN_EPS = 1e-5
NEG_INF = -1e30

Z_COLS = N_BRANCH * D_MODEL + 6 * BRANCH_WIDTH + 2 * N_KV * HEAD_DIM
ZG_BLK = 0
ZF_BLK, ZB_BLK, ZC_BLK, ZH_BLK, ZS_BLK, ZQ_BLK = 8, 9, 10, 11, 12, 13
ZK_BLK, ZV_BLK = 56, 57

V7X_VMEM_LIMIT_BYTES = 56 * 1024 * 1024
SSM_GBLK = 8
SSM_TCHUNK = 256
SSM_BROWS = 8
ROW_TILE = 256
PEER_TT = 512
PEER_EB = 2048
PEER_CHUNKS = 4
ROUTE_TT = 256


def _params(sem):
    return pltpu.CompilerParams(dimension_semantics=sem, vmem_limit_bytes=V7X_VMEM_LIMIT_BYTES)


def _gelu(x):
    return 0.5 * x * (1.0 + jnp.tanh(0.7978845608028654 * (x + 0.044715 * (x * x * x))))


def _layer_norm(h, g, b):
    mu = jnp.mean(h, axis=-1, keepdims=True)
    hc = h - mu
    var = jnp.mean(hc * hc, axis=-1, keepdims=True)
    return hc * lax.rsqrt(var + LN_EPS) * g + b


def _mod_kernel(c_ref, w_ref, b_ref, o_ref):
    cv = c_ref[...]
    s = (cv * jax.nn.sigmoid(cv)).astype(BF16)
    o_ref[0] = jnp.dot(s, w_ref[0].astype(BF16), preferred_element_type=F32) + b_ref[0]


def _modulation(cvecs, w_ada, b_ada):
    depth = w_ada.shape[0]
    nrow = cvecs.shape[0]
    return pl.pallas_call(
        _mod_kernel,
        grid=(depth, 6),
        in_specs=[
            pl.BlockSpec((nrow, D_MODEL), lambda l, j: (0, 0)),
            pl.BlockSpec((1, D_MODEL, D_MODEL), lambda l, j: (l, 0, j)),
            pl.BlockSpec((1, 1, D_MODEL), lambda l, j: (l, 0, j)),
        ],
        out_specs=pl.BlockSpec((1, nrow, D_MODEL), lambda l, j: (l, 0, j)),
        out_shape=jax.ShapeDtypeStruct((depth, nrow, 6 * D_MODEL), F32),
        compiler_params=_params(("parallel", "parallel")),
    )(cvecs, w_ada, b_ada.reshape(depth, 1, 6 * D_MODEL))


def _win_kernel(x_ref, mod_ref, w_ref, z_ref):
    sh = mod_ref[0, 0:1, :]
    sc = mod_ref[0, 1:2, :]
    xm = (x_ref[...] * (1.0 + sc) + sh).astype(BF16)
    z_ref[...] = jnp.dot(xm, w_ref[...], preferred_element_type=F32)


def _in_proj(x, mods, w_in, mod_row, l):
    t = x.shape[0]
    ncol = Z_COLS // 2
    return pl.pallas_call(
        _win_kernel,
        grid=(2, t // ROW_TILE),
        in_specs=[
            pl.BlockSpec((ROW_TILE, D_MODEL), lambda c, i: (i, 0)),
            pl.BlockSpec((None, 1, 6, D_MODEL), lambda c, i: (l, mod_row(i), 0, 0)),
            pl.BlockSpec((None, D_MODEL, ncol), lambda c, i: (l, 0, c)),
        ],
        out_specs=pl.BlockSpec((ROW_TILE, ncol), lambda c, i: (i, c)),
        out_shape=jax.ShapeDtypeStruct((t, Z_COLS), F32),
        compiler_params=_params(("parallel", "parallel")),
    )(x, mods, w_in)


def _fft_conv(zf_ref, zb_ref, zc_ref, zh_ref, cw_ref, dl_ref, dc_ref, yf_ref, yc_ref):
    length = zf_ref.shape[0]
    zf = zf_ref[...].astype(BF16)
    ab = jnp.dot(zf, dc_ref[...], preferred_element_type=F32)
    ab = jnp.concatenate([ab[:, :BRANCH_WIDTH], ab[:, BRANCH_WIDTH:]], axis=0).astype(BF16)
    yf_ref[...] = jnp.dot(dl_ref[...], ab, preferred_element_type=F32).astype(BF16)
    g = zc_ref[...] * zh_ref[...]
    row = lax.broadcasted_iota(jnp.int32, g.shape, 0)
    prev = jnp.where(row == 0, 0.0, pltpu.roll(g, 1, 0))
    nxt = jnp.where(row == length - 1, 0.0, pltpu.roll(g, length - 1, 0))
    conv = cw_ref[0:1, :] * prev + cw_ref[1:2, :] * g + cw_ref[2:3, :] * nxt
    yc_ref[...] = (zb_ref[...] * conv).astype(BF16)


def _softmax_pv(s, sink, v):
    m = jnp.maximum(jnp.max(s, axis=1, keepdims=True), sink)
    p = jnp.exp(s - m)
    den = jnp.sum(p, axis=1, keepdims=True) + jnp.exp(sink - m)
    return jnp.dot(p.astype(BF16), v, preferred_element_type=F32) / den


def _mixer_ctx_kernel(zf_ref, zb_ref, zc_ref, zh_ref, zq_ref, zk_ref, zv_ref, cw_ref, sink_ref,
                      dl_ref, dc_ref, yf_ref, yc_ref, ya_ref):
    _fft_conv(zf_ref, zb_ref, zc_ref, zh_ref, cw_ref, dl_ref, dc_ref, yf_ref, yc_ref)
    q = zq_ref[...] * (HEAD_DIM ** -0.5)
    k = zk_ref[...]
    v = zv_ref[...]
    outs = []
    for h in range(N_HEADS):
        g = h // Q_PER_KV
        qh = q[:, h * HEAD_DIM:(h + 1) * HEAD_DIM].astype(BF16)
        kg = k[:, g * HEAD_DIM:(g + 1) * HEAD_DIM].astype(BF16)
        vg = v[:, g * HEAD_DIM:(g + 1) * HEAD_DIM].astype(BF16)
        s = lax.dot_general(qh, kg, (((1,), (1,)), ((), ())), preferred_element_type=F32)
        outs.append(_softmax_pv(s, sink_ref[0:1, h:h + 1], vg))
    ya_ref[...] = jnp.concatenate(outs, axis=1).astype(BF16)


def _zspec(rows, width, row_fn, col_blk):
    return pl.BlockSpec((rows, width), lambda *a: (row_fn(*a), col_blk))


def _mixer_ctx(z, nb, seq, conv_w, sink, dft_l, dft_c, l):
    rf = lambda b: b
    full = lambda shape: pl.BlockSpec(shape, lambda b: (0,) * len(shape))
    layer = lambda shape: pl.BlockSpec((None,) + shape, lambda b: (l,) + (0,) * len(shape))
    out = jax.ShapeDtypeStruct((z.shape[0], BRANCH_WIDTH), BF16)
    ospec = pl.BlockSpec((seq, BRANCH_WIDTH), lambda b: (b, 0))
    return pl.pallas_call(
        _mixer_ctx_kernel,
        grid=(nb,),
        in_specs=[
            _zspec(seq, 512, rf, ZF_BLK), _zspec(seq, 512, rf, ZB_BLK), _zspec(seq, 512, rf, ZC_BLK),
            _zspec(seq, 512, rf, ZH_BLK), _zspec(seq, 512, rf, ZQ_BLK),
            _zspec(seq, 128, rf, ZK_BLK), _zspec(seq, 128, rf, ZV_BLK),
            layer((CONV_K, BRANCH_WIDTH)), layer((1, N_HEADS)),
            full((seq, 2 * seq)), full((BRANCH_WIDTH, 2 * BRANCH_WIDTH)),
        ],
        out_specs=[ospec, ospec, ospec],
        out_shape=[out, out, out],
        compiler_params=_params(("parallel",)),
    )(z, z, z, z, z, z, z, conv_w, sink, dft_l, dft_c)


def _fftconv_lat_kernel(zf_ref, zb_ref, zc_ref, zh_ref, cw_ref, dl_ref, dc_ref, yf_in, yc_in,
                        yf_ref, yc_ref):
    del yf_in, yc_in
    _fft_conv(zf_ref, zb_ref, zc_ref, zh_ref, cw_ref, dl_ref, dc_ref, yf_ref, yc_ref)


def _fftconv_lat(z, row0, nb, seq, conv_w, dft_l, dft_c, yf, yc, l):
    rf = lambda b: row0 // seq + b
    full = lambda shape: pl.BlockSpec(shape, lambda b: (0,) * len(shape))
    out = jax.ShapeDtypeStruct(yf.shape, BF16)
    ospec = pl.BlockSpec((seq, BRANCH_WIDTH), lambda b: (row0 // seq + b, 0))
    anyspec = pl.BlockSpec(memory_space=pl.ANY)
    return pl.pallas_call(
        _fftconv_lat_kernel,
        grid=(nb,),
        in_specs=[
            _zspec(seq, 512, rf, ZF_BLK), _zspec(seq, 512, rf, ZB_BLK), _zspec(seq, 512, rf, ZC_BLK),
            _zspec(seq, 512, rf, ZH_BLK),
            pl.BlockSpec((None, CONV_K, BRANCH_WIDTH), lambda b: (l, 0, 0)),
            full((seq, 2 * seq)), full((BRANCH_WIDTH, 2 * BRANCH_WIDTH)),
            anyspec, anyspec,
        ],
        out_specs=[ospec, ospec],
        out_shape=[out, out],
        input_output_aliases={7: 0, 8: 1},
        compiler_params=_params(("parallel",)),
    )(z, z, z, z, conv_w, dft_l, dft_c, yf, yc)


def _rope(x, cos, sin):
    lane = lax.broadcasted_iota(jnp.int32, (x.shape[0], 128), 1)
    first = (lane & 31) < 16
    parts = []
    for c in range(x.shape[1] // 128):
        xc = x[:, c * 128:(c + 1) * 128]
        swapped = jnp.where(first, pltpu.roll(xc, 112, 1), pltpu.roll(xc, 16, 1))
        parts.append(xc * cos[:, c * 128:(c + 1) * 128] + swapped * sin[:, c * 128:(c + 1) * 128])
    return parts[0] if len(parts) == 1 else jnp.concatenate(parts, axis=1)


def _attn_lat_kernel(zq_ref, zk_ref, zv_ref, ck_ref, cv_ref, cosq_ref, sinq_ref, cosk_ref, sinkk_ref,
                     sink_ref, ya_in, ya_ref):
    del ya_in
    n = pl.program_id(1)
    nblk = pl.num_programs(1)
    nwin = 3 * ATT_BLOCK
    q = _rope(zq_ref[...], cosq_ref[...], sinq_ref[...]) * (HEAD_DIM ** -0.5)
    ws = pl.multiple_of(jnp.clip(n - 1, 0, nblk - 3) * ATT_BLOCK, ATT_BLOCK)
    kw = _rope(zk_ref[pl.ds(ws, nwin), :], cosk_ref[pl.ds(ws, nwin), :], sinkk_ref[pl.ds(ws, nwin), :])
    vw = zv_ref[pl.ds(ws, nwin), :]
    k_all = jnp.concatenate([kw, ck_ref[...]], axis=0)
    v_all = jnp.concatenate([vw, cv_ref[...]], axis=0)
    nkey = k_all.shape[0]
    qpos = n * ATT_BLOCK + lax.broadcasted_iota(jnp.int32, (ATT_BLOCK, nkey), 0)
    col = lax.broadcasted_iota(jnp.int32, (ATT_BLOCK, nkey), 1)
    valid = (jnp.abs(qpos - (ws + col)) <= WINDOW) | (col >= nwin)
    outs = []
    for h in range(N_HEADS):
        g = h // Q_PER_KV
        qh = q[:, h * HEAD_DIM:(h + 1) * HEAD_DIM].astype(BF16)
        kg = k_all[:, g * HEAD_DIM:(g + 1) * HEAD_DIM].astype(BF16)
        vg = v_all[:, g * HEAD_DIM:(g + 1) * HEAD_DIM].astype(BF16)
        s = lax.dot_general(qh, kg, (((1,), (1,)), ((), ())), preferred_element_type=F32)
        s = jnp.where(valid, s, NEG_INF)
        outs.append(_softmax_pv(s, sink_ref[0:1, h:h + 1], vg))
    ya_ref[...] = jnp.concatenate(outs, axis=1).astype(BF16)


def _attn_lat(z, row0, nb, seq, ck, cv, cosq, sinq, cosk, sin_k, sink, ya, l):
    nblk = seq // ATT_BLOCK
    kvw = N_KV * HEAD_DIM
    past = ck.shape[2]
    full = lambda shape: pl.BlockSpec(shape, lambda b, n: (0,) * len(shape))
    cache = pl.BlockSpec((None, None, past, kvw), lambda b, n: (b, l, 0, 0))
    return pl.pallas_call(
        _attn_lat_kernel,
        grid=(nb, nblk),
        in_specs=[
            pl.BlockSpec((ATT_BLOCK, 512), lambda b, n: (row0 // ATT_BLOCK + b * nblk + n, ZQ_BLK)),
            pl.BlockSpec((seq, kvw), lambda b, n: (row0 // seq + b, ZK_BLK)),
            pl.BlockSpec((seq, kvw), lambda b, n: (row0 // seq + b, ZV_BLK)),
            cache, cache,
            pl.BlockSpec((ATT_BLOCK, 512), lambda b, n: (n, 0)),
            pl.BlockSpec((ATT_BLOCK, 512), lambda b, n: (n, 0)),
            full((seq, kvw)), full((seq, kvw)),
            pl.BlockSpec((None, 1, N_HEADS), lambda b, n: (l, 0, 0)),
            pl.BlockSpec(memory_space=pl.ANY),
        ],
        out_specs=pl.BlockSpec((ATT_BLOCK, 512), lambda b, n: (row0 // ATT_BLOCK + b * nblk + n, 0)),
        out_shape=jax.ShapeDtypeStruct(ya.shape, BF16),
        input_output_aliases={10: 0},
        compiler_params=_params(("parallel", "parallel")),
    )(z, z, z, ck, cv, cosq, sinq, cosk, sin_k, sink, ya)


def _s5_kernel(uf_ref, ub_ref, h0f_ref, h0b_ref, wbf_ref, wbb_ref, cf_ref, cb_ref, af_ref, ab_ref,
               yf_ref, yb_ref, hf_ref, hb_ref, buff, bufb, hst):
    c = pl.program_id(2)
    half = SSM_GBLK * SSM_STATE
    steps = uf_ref.shape[0] // SSM_BROWS

    @pl.when(c == 0)
    def _():
        hst[0] = h0f_ref[0]
        hst[1] = h0b_ref[0]

    buff[...] = jnp.dot(uf_ref[...].astype(BF16), wbf_ref[...], preferred_element_type=F32)
    bufb[...] = jnp.dot(ub_ref[...].astype(BF16), wbb_ref[...], preferred_element_type=F32)
    afr = jnp.broadcast_to(af_ref[0:1, :], (SSM_BROWS, half))
    afi = jnp.broadcast_to(af_ref[1:2, :], (SSM_BROWS, half))
    abr = jnp.broadcast_to(ab_ref[0:1, :], (SSM_BROWS, half))
    abi = jnp.broadcast_to(ab_ref[1:2, :], (SSM_BROWS, half))

    def step(t, carry):
        hfr, hfi, hbr, hbi = carry
        rf = pl.multiple_of(t * SSM_BROWS, SSM_BROWS)
        nfr = afr * hfr - afi * hfi + buff[pl.ds(rf, SSM_BROWS), 0:half]
        nfi = afr * hfi + afi * hfr + buff[pl.ds(rf, SSM_BROWS), half:2 * half]
        buff[pl.ds(rf, SSM_BROWS), 0:half] = nfr
        buff[pl.ds(rf, SSM_BROWS), half:2 * half] = nfi
        rb = pl.multiple_of((steps - 1 - t) * SSM_BROWS, SSM_BROWS)
        nbr = abr * hbr - abi * hbi + bufb[pl.ds(rb, SSM_BROWS), 0:half]
        nbi = abr * hbi + abi * hbr + bufb[pl.ds(rb, SSM_BROWS), half:2 * half]
        bufb[pl.ds(rb, SSM_BROWS), 0:half] = nbr
        bufb[pl.ds(rb, SSM_BROWS), half:2 * half] = nbi
        return nfr, nfi, nbr, nbi

    init = (hst[0, :, 0:half], hst[0, :, half:2 * half], hst[1, :, 0:half], hst[1, :, half:2 * half])
    hfr, hfi, hbr, hbi = lax.fori_loop(0, steps, step, init, unroll=4)
    hst[0, :, 0:half] = hfr
    hst[0, :, half:2 * half] = hfi
    hst[1, :, 0:half] = hbr
    hst[1, :, half:2 * half] = hbi
    yf_ref[...] = jnp.dot(buff[...].astype(BF16), cf_ref[...], preferred_element_type=F32)
    yb_ref[...] = jnp.dot(bufb[...].astype(BF16), cb_ref[...], preferred_element_type=F32)

    @pl.when(c == pl.num_programs(2) - 1)
    def _():
        hf_ref[0] = hst[0]
        hb_ref[0] = hst[1]


def _s5(u_tm, h0f, h0b, sp, nbb, nchunk, l):
    rows = SSM_TCHUNK * SSM_BROWS
    ngb = SSM_GROUPS // SSM_GBLK
    width = 2 * SSM_GBLK * SSM_STATE
    nbrow = nbb * SSM_BROWS
    cw = SSM_GBLK * SSM_CH
    fwd = lambda bb, j, c: (bb * nchunk + c, j)
    bwd = lambda bb, j, c: (bb * nchunk + nchunk - 1 - c, j)
    par = lambda shape, d: pl.BlockSpec((None, None, None) + shape, lambda bb, j, c: (l, d, j, 0, 0))
    st = pl.BlockSpec((1, SSM_BROWS, width), lambda bb, j, c: (j, bb, 0))
    wb, cm, a = sp
    ysh = jax.ShapeDtypeStruct(u_tm.shape, F32)
    hsh = jax.ShapeDtypeStruct((ngb, nbrow, width), F32)
    return pl.pallas_call(
        _s5_kernel,
        grid=(nbb, ngb, nchunk),
        in_specs=[
            pl.BlockSpec((rows, cw), fwd), pl.BlockSpec((rows, cw), bwd), st, st,
            par((cw, width), 0), par((cw, width), 1), par((width, cw), 0), par((width, cw), 1),
            par((2, width // 2), 0), par((2, width // 2), 1),
        ],
        out_specs=[pl.BlockSpec((rows, cw), fwd), pl.BlockSpec((rows, cw), bwd), st, st],
        out_shape=[ysh, ysh, hsh, hsh],
        scratch_shapes=[pltpu.VMEM((rows, width), F32), pltpu.VMEM((rows, width), F32),
                        pltpu.VMEM((2, SSM_BROWS, width), F32)],
        compiler_params=_params(("parallel", "parallel", "arbitrary")),
    )(u_tm, u_tm, h0f, h0b, wb, wb, cm, cm, a, a)


def _s5_params(lam_re, lam_im, log_step, b_re, b_im, c_re, c_im):
    lead = lam_re.shape[:-2]
    dt = jnp.exp(log_step)[..., None]
    mag = jnp.exp(lam_re * dt)
    ar = mag * jnp.cos(lam_im * dt)
    ai = mag * jnp.sin(lam_im * dt)
    den = lam_re * lam_re + lam_im * lam_im
    kr = ((ar - 1.0) * lam_re + ai * lam_im) / den
    ki = (ai * lam_re - (ar - 1.0) * lam_im) / den
    bbr = kr[..., None] * b_re - ki[..., None] * b_im
    bbi = kr[..., None] * b_im + ki[..., None] * b_re
    ngb = SSM_GROUPS // SSM_GBLK
    eye = jnp.eye(SSM_GBLK, dtype=F32)

    def blockdiag_in(m):
        m = m.reshape(lead + (ngb, SSM_GBLK, SSM_STATE, SSM_CH))
        m = jnp.einsum("...jgph,gk->...jghkp", m, eye)
        return m.reshape(lead + (ngb, SSM_GBLK * SSM_CH, SSM_GBLK * SSM_STATE))

    def blockdiag_out(m):
        m = m.reshape(lead + (ngb, SSM_GBLK, SSM_CH, SSM_STATE))
        m = jnp.einsum("...jghp,gk->...jgpkh", m, eye)
        return m.reshape(lead + (ngb, SSM_GBLK * SSM_STATE, SSM_GBLK * SSM_CH))

    wb = jnp.concatenate([blockdiag_in(bbr), blockdiag_in(bbi)], axis=-1).astype(BF16)
    cm = jnp.concatenate([blockdiag_out(c_re), -blockdiag_out(c_im)], axis=-2).astype(BF16)
    a = jnp.stack([ar.reshape(lead + (ngb, -1)), ai.reshape(lead + (ngb, -1))], axis=-2)
    return wb, cm, a


def _to_time_major(u, nb, seq):
    cdim = u.shape[1]
    nbp = -(-nb // SSM_BROWS) * SSM_BROWS
    u = u.reshape(nb, seq, cdim)
    if nbp != nb:
        u = jnp.pad(u, ((0, nbp - nb), (0, 0), (0, 0)))
    u = u.reshape(nbp // SSM_BROWS, SSM_BROWS, seq, cdim).transpose(0, 2, 1, 3)
    return u.reshape(nbp * seq, cdim), nbp


def _from_time_major(y, nb, nbp, seq):
    cdim = y.shape[1]
    y = y.reshape(nbp // SSM_BROWS, seq, SSM_BROWS, cdim).transpose(0, 2, 1, 3)
    return y.reshape(nbp, seq, cdim)[:nb].reshape(nb * seq, cdim)


def _state_to_blocks(re, im, nbp):
    nb = re.shape[0]
    ngb = SSM_GROUPS // SSM_GBLK
    def blk(x):
        return x.reshape(nb, ngb, SSM_GBLK * SSM_STATE).transpose(1, 0, 2)
    h = jnp.concatenate([blk(re), blk(im)], axis=2)
    if nbp != nb:
        h = jnp.pad(h, ((0, 0), (0, nbp - nb), (0, 0)))
    return h


def _blocks_to_state(h, nb):
    half = SSM_GBLK * SSM_STATE
    def unblk(x):
        return x[:, :nb].transpose(1, 0, 2).reshape(nb, SSM_GROUPS, SSM_STATE)
    return unblk(h[:, :, :half]), unblk(h[:, :, half:])


def _merge_kernel(x_ref, mod_ref, yf_ref, yc_ref, ya_ref, ysf_ref, ysb_ref, zs_ref,
                  zg0_ref, zg1_ref, zg2_ref, zg3_ref, d_ref, wglu_ref, wb_ref, wout_ref,
                  g_ref, b_ref, x1_ref, xm_ref):
    ys = ysf_ref[...] + ysb_ref[...] + d_ref[...] * zs_ref[...]
    ys = _gelu(ys)
    yssm = ys * jax.nn.sigmoid(jnp.dot(ys.astype(BF16), wglu_ref[...], preferred_element_type=F32))
    acc = jax.nn.sigmoid(zg0_ref[...]) * jnp.dot(yf_ref[...], wb_ref[0], preferred_element_type=F32)
    acc += jax.nn.sigmoid(zg1_ref[...]) * jnp.dot(yc_ref[...], wb_ref[1], preferred_element_type=F32)
    acc += jax.nn.sigmoid(zg2_ref[...]) * jnp.dot(yssm.astype(BF16), wb_ref[2], preferred_element_type=F32)
    acc += jax.nn.sigmoid(zg3_ref[...]) * jnp.dot(ya_ref[...], wb_ref[3], preferred_element_type=F32)
    mix = jnp.dot(acc.astype(BF16), wout_ref[...], preferred_element_type=F32)
    alpha = (2 * 4) ** 0.25
    x1 = _layer_norm(alpha * x_ref[...] + mod_ref[0, 2:3, :] * mix, g_ref[...], b_ref[...])
    x1_ref[...] = x1
    xm_ref[...] = (x1 * (1.0 + mod_ref[0, 4:5, :]) + mod_ref[0, 3:4, :]).astype(BF16)


def _merge(x, mods, mod_row, yf, yc, ya, ysf, ysb, z, ssm_d, w_glu, w_branch, w_out, ln_g, ln_b, l):
    t = x.shape[0]
    rf = lambda i: i
    row = lambda w: pl.BlockSpec((ROW_TILE, w), lambda i: (i, 0))
    full = lambda shape: pl.BlockSpec((None,) + shape, lambda i: (l,) + (0,) * len(shape))
    return pl.pallas_call(
        _merge_kernel,
        grid=(t // ROW_TILE,),
        in_specs=[
            row(D_MODEL), pl.BlockSpec((None, 1, 6, D_MODEL), lambda i: (l, mod_row(i), 0, 0)),
            row(512), row(512), row(512), row(512), row(512),
            _zspec(ROW_TILE, 512, rf, ZS_BLK),
            _zspec(ROW_TILE, 1024, rf, 0), _zspec(ROW_TILE, 1024, rf, 1),
            _zspec(ROW_TILE, 1024, rf, 2), _zspec(ROW_TILE, 1024, rf, 3),
            full((1, 512)), full((512, 512)), full((N_BRANCH, 512, D_MODEL)), full((D_MODEL, D_MODEL)),
            full((1, D_MODEL)), full((1, D_MODEL)),
        ],
        out_specs=[row(D_MODEL), row(D_MODEL)],
        out_shape=[jax.ShapeDtypeStruct((t, D_MODEL), F32), jax.ShapeDtypeStruct((t, D_MODEL), BF16)],
        compiler_params=_params(("parallel",)),
    )(x, mods, yf, yc, ya, ysf, ysb, z, z, z, z, z, ssm_d, w_glu, w_branch, w_out, ln_g, ln_b)


def _top16(s):
    n, w = s.shape
    iota = lax.broadcasted_iota(jnp.int32, (n, w), 0).astype(F32)
    kio = lax.broadcasted_iota(jnp.int32, (PEER_TOPK, w), 0)

    def body(k, carry):
        work, rank, vals, _ = carry
        m = jnp.max(work, axis=0, keepdims=True)
        pos = jnp.min(jnp.where(work == m, iota, float(n)), axis=0, keepdims=True)
        hit = iota == pos
        rank = jnp.where(hit, lax.convert_element_type(k, F32), rank)
        work = jnp.where(hit, -jnp.inf, work)
        vals = jnp.where(kio == k, m, vals)
        return work, rank, vals, pos

    init = (s, jnp.full((n, w), 1e9, F32), jnp.zeros((PEER_TOPK, w), F32), jnp.zeros((1, w), F32))
    _, rank, vals, pos = lax.fori_loop(0, PEER_TOPK, body, init)
    return vals, rank, pos


def _max16(s):
    w = s.shape[1]
    kio = lax.broadcasted_iota(jnp.int32, (PEER_TOPK, w), 0)

    def body(k, carry):
        work, vals = carry
        m = jnp.max(work, axis=0, keepdims=True)
        return jnp.where(work == m, -jnp.inf, work), jnp.where(kio == k, m, vals)

    _, vals = lax.fori_loop(0, PEER_TOPK, body, (s, jnp.zeros((PEER_TOPK, w), F32)))
    return vals


_STAIR = [(j, PEER_TOPK // (j + 1)) for j in range(PEER_TOPK)]
_STAIR_ROWS = -(-sum(k for _, k in _STAIR) // 8) * 8


def _stair_candidates(v1, v2):
    w = v1.shape[1]
    rows = [v1[j:j + 1] + v2[0:k] for j, k in _STAIR]
    npad = _STAIR_ROWS - sum(k for _, k in _STAIR)
    return jnp.concatenate(rows + [jnp.full((npad, w), -jnp.inf, F32)], axis=0)


def _stair_positions(w):
    rows = [float(PEER_TOPK * j) + lax.broadcasted_iota(jnp.int32, (k, w), 0).astype(F32) for j, k in _STAIR]
    npad = _STAIR_ROWS - sum(k for _, k in _STAIR)
    return jnp.concatenate(rows + [jnp.full((npad, w), 1e9, F32)], axis=0)


def _next_up(x):
    b = lax.bitcast_convert_type(x, jnp.int32)
    up = jnp.where(x > 0.0, b + 1, jnp.where(x < 0.0, b - 1, jnp.int32(0x00800000)))
    return lax.bitcast_convert_type(up, F32)


def _route_kernel(xm_ref, wq_ref, keys_ref, s1m_ref, qrow_ref, e1_ref, s2m_ref, pb_ref, e2_ref,
                  thr_ref, qs):
    qs[...] = lax.dot_general(wq_ref[...], xm_ref[...], (((1,), (1,)), ((), ())),
                              preferred_element_type=F32)
    w = xm_ref.shape[0]

    def count(mask):
        return jnp.sum(mask.astype(F32), axis=0, keepdims=True)

    def emit(h, s1, s2, in1, in2, m1, m2, vc, qrow, pb, thr_up, bad):
        z = jnp.sum(jnp.exp(vc - vc[0:1]), axis=0, keepdims=True)
        s1m_ref[h] = jnp.where(in1, s1, -jnp.inf)
        s2m_ref[h] = jnp.where(in2, s2, -jnp.inf)
        e1_ref[h] = jnp.where(in1, jnp.exp(s1 - m1), 0.0) / z
        e2_ref[h] = jnp.where(in2, jnp.exp(s2 - m2), 0.0)
        qrow_ref[h] = qrow
        pb_ref[h] = pb
        thr = vc[PEER_TOPK - 1:PEER_TOPK]
        thr_ref[h] = jnp.concatenate([thr, thr_up, bad, jnp.zeros((5, w), F32)], axis=0)

    def head(h, carry):
        base = pl.multiple_of(h * KEY_DIM, KEY_DIM)
        q1 = qs[pl.ds(base, N_KEYS), :].astype(BF16)
        q2 = qs[pl.ds(base + N_KEYS, N_KEYS), :].astype(BF16)
        s1 = jnp.dot(keys_ref[2 * h], q1, preferred_element_type=F32)
        s2 = jnp.dot(keys_ref[2 * h + 1], q2, preferred_element_type=F32)

        v1 = _max16(s1)
        v2 = _max16(s2)
        in1 = s1 >= v1[PEER_TOPK - 1:PEER_TOPK]
        in2 = s2 >= v2[PEER_TOPK - 1:PEER_TOPK]
        cand = _stair_candidates(v1, v2)
        vc = _max16(cand)
        thr = vc[PEER_TOPK - 1:PEER_TOPK]
        zero = jnp.zeros((N_KEYS, w), F32)
        k = float(PEER_TOPK)
        bad = jnp.abs(count(in1) - k) + jnp.abs(count(in2) - k) + jnp.abs(count(cand >= thr) - k)
        emit(h, s1, s2, in1, in2, v1[0:1], v2[0:1], vc, zero, zero, thr, bad)

        @pl.when(jnp.max(bad) > 0.0)
        def _():
            xv1, r1, _ = _top16(s1)
            xv2, r2, _ = _top16(s2)
            xcand = _stair_candidates(xv1, xv2)
            xvc, _, prow = _top16(xcand)
            riota = lax.broadcasted_iota(jnp.int32, xcand.shape, 0).astype(F32)
            pthr = jnp.sum(jnp.where(riota == prow, _stair_positions(w), 0.0), axis=0, keepdims=True)
            emit(h, s1, s2, r1 < 100.0, r2 < 100.0, xv1[0:1], xv2[0:1], xvc,
                 pthr - k * r1, r2, _next_up(xvc[PEER_TOPK - 1:PEER_TOPK]), bad)

        return carry

    lax.fori_loop(0, PEER_HEADS, head, 0)


def _route(xm, wq_t, keys, l):
    t = xm.shape[0]
    big = jax.ShapeDtypeStruct((PEER_HEADS, N_KEYS, t), F32)
    bspec = pl.BlockSpec((PEER_HEADS, N_KEYS, ROUTE_TT), lambda i: (0, 0, i))
    return pl.pallas_call(
        _route_kernel,
        grid=(t // ROUTE_TT,),
        in_specs=[
            pl.BlockSpec((ROUTE_TT, D_MODEL), lambda i: (i, 0)),
            pl.BlockSpec((None, PEER_HEADS * KEY_DIM, D_MODEL), lambda i: (l, 0, 0)),
            pl.BlockSpec((None, 2 * PEER_HEADS, N_KEYS, N_KEYS), lambda i: (l, 0, 0, 0)),
        ],
        out_specs=[bspec] * 6 + [pl.BlockSpec((PEER_HEADS, 8, ROUTE_TT), lambda i: (0, 0, i))],
        out_shape=[big] * 6 + [jax.ShapeDtypeStruct((PEER_HEADS, 8, t), F32)],
        scratch_shapes=[pltpu.VMEM((PEER_HEADS * KEY_DIM, ROUTE_TT), F32)],
        compiler_params=_params(("parallel",)),
    )(xm, wq_t, keys)


def _peer_kernel(flag_ref, xm_ref, u_ref, vt_ref, s1m_ref, qrow_ref, e1_ref, s2m_ref, pb_ref, e2_ref,
                 thr_ref, x1_ref, mod_ref, g_ref, b_ref, o_ref, ht, wacc, pt, acc):
    j = pl.program_id(1)
    nrow = PEER_EB // N_KEYS

    @pl.when(j == 0)
    def _():
        acc[...] = jnp.zeros_like(acc)

    heads_per_chunk = PEER_HEADS // PEER_CHUNKS
    rpass = 4

    def rows_of(tile, r0):
        return jnp.stack([jnp.broadcast_to(tile[r:r + 1, :], (8, 128)) for r in range(r0, r0 + rpass)])

    def chunk_work(c, exact):
        for hh in range(heads_per_chunk):
            h = c * heads_per_chunk + hh
            for lg in range(PEER_TT // 128):
                lanes = slice(lg * 128, (lg + 1) * 128)
                thr = thr_ref[h, 0:1, lanes]
                s1t = s1m_ref[h, :, lanes]
                e1t = e1_ref[h, :, lanes]
                if exact:
                    thr_up = thr_ref[h, 1:2, lanes]
                    qrt = qrow_ref[h, :, lanes]
                for rp in range(0, nrow, rpass):
                    s1r = rows_of(s1t, rp)
                    e1r = rows_of(e1t, rp)
                    if exact:
                        qr = rows_of(qrt, rp)
                    for v in range(N_KEYS // 8):
                        sub = slice(v * 8, (v + 1) * 8)
                        sc = s1r + s2m_ref[h, sub, lanes][None]
                        if exact:
                            first = pb_ref[h, sub, lanes][None] <= qr
                            sel = sc >= jnp.where(first, thr, thr_up)
                        else:
                            sel = sc >= thr
                        gate = jnp.where(sel, e1r * e2_ref[h, sub, lanes][None], 0.0)
                        if h == 0:
                            wacc[rp:rp + rpass, sub, lanes] = gate
                        else:
                            wacc[rp:rp + rpass, sub, lanes] += gate

    hfull = lax.dot_general(u_ref[...], xm_ref[...], (((1,), (1,)), ((), ())),
                            preferred_element_type=F32)
    for r in range(nrow):
        ht[r] = hfull[r * N_KEYS:(r + 1) * N_KEYS, :]
    for c in range(PEER_CHUNKS):
        needs_ties = flag_ref[pl.program_id(0), c] > 0
        pl.when(needs_ties)(functools.partial(chunk_work, c, True))
        pl.when(jnp.logical_not(needs_ties))(functools.partial(chunk_work, c, False))
    for lg in range(PEER_TT // 128):
        lanes = slice(lg * 128, (lg + 1) * 128)
        for r in range(nrow):
            pt[r * N_KEYS:(r + 1) * N_KEYS, lanes] = (
                wacc[r, :, lanes] * _gelu(ht[r, :, lanes])).astype(BF16)
    acc[...] += jnp.dot(vt_ref[...], pt[...], preferred_element_type=F32)

    @pl.when(j == pl.num_programs(1) - 1)
    def _():
        alpha = (2 * 4) ** 0.25
        ff = acc[...].T
        o_ref[...] = _layer_norm(alpha * x1_ref[...] + mod_ref[0, 5:6, :] * ff, g_ref[...], b_ref[...])


def _peer(xm, u_b, vt_b, routing, x1, mods, mod_row_tt, ln_g, ln_b, l):
    t = xm.shape[0]
    once = pl.Buffered(1)
    tok = lambda w: pl.BlockSpec((PEER_TT, w), lambda i, j: (i, 0), pipeline_mode=once)
    rspec = pl.BlockSpec((PEER_HEADS, N_KEYS, PEER_TT), lambda i, j: (0, 0, i), pipeline_mode=once)
    rowspec = pl.BlockSpec((PEER_HEADS, PEER_EB // N_KEYS, PEER_TT), lambda i, j: (0, j, i))
    full = lambda shape: pl.BlockSpec((None,) + shape, lambda i, j: (l,) + (0,) * len(shape))
    nblk = N_EXPERTS // PEER_EB
    bad = routing[6][:, 2, :].reshape(PEER_CHUNKS, PEER_HEADS // PEER_CHUNKS, t // PEER_TT, PEER_TT)
    flags = (jnp.max(bad, axis=(1, 3)) > 0.0).astype(jnp.int32).T
    return pl.pallas_call(
        _peer_kernel,
        grid=(t // PEER_TT, nblk),
        in_specs=[
            pl.BlockSpec(memory_space=pltpu.SMEM),
            tok(D_MODEL),
            pl.BlockSpec((None, PEER_EB, D_MODEL), lambda i, j: (l, j, 0)),
            pl.BlockSpec((None, D_MODEL, PEER_EB), lambda i, j: (l, 0, j)),
            rowspec, rowspec, rowspec, rspec, rspec, rspec,
            pl.BlockSpec((PEER_HEADS, 8, PEER_TT), lambda i, j: (0, 0, i)),
            tok(D_MODEL),
            pl.BlockSpec((None, 1, 6, D_MODEL), lambda i, j: (l, mod_row_tt(i), 0, 0)),
            full((1, D_MODEL)), full((1, D_MODEL)),
        ],
        out_specs=pl.BlockSpec((PEER_TT, D_MODEL), lambda i, j: (i, 0)),
        out_shape=jax.ShapeDtypeStruct((t, D_MODEL), F32),
        scratch_shapes=[pltpu.VMEM((PEER_EB // N_KEYS, N_KEYS, PEER_TT), F32),
                        pltpu.VMEM((PEER_EB // N_KEYS, N_KEYS, PEER_TT), F32),
                        pltpu.VMEM((PEER_EB, PEER_TT), BF16),
                        pltpu.VMEM((D_MODEL, PEER_TT), F32)],
        compiler_params=_params(("parallel", "arbitrary")),
    )(flags, xm, u_b, vt_b, *routing, x1, mods, ln_g, ln_b)


def _dft_tables(length):
    n = np.arange(length)
    ang = 2.0 * np.pi * ((n[:, None] * n[None, :]) % length) / length
    dl = np.concatenate([np.cos(ang), -np.sin(ang)], axis=1) / math.sqrt(length)
    c = np.arange(FFT_GROUP_CH)
    angc = 2.0 * np.pi * ((c[:, None] * c[None, :]) % FFT_GROUP_CH) / FFT_GROUP_CH
    eye = np.eye(FFT_GROUPS)
    dc = np.concatenate([np.kron(eye, np.cos(angc)), np.kron(eye, np.sin(angc))], axis=1)
    dc = dc / math.sqrt(FFT_GROUP_CH)
    return jnp.asarray(dl, BF16), jnp.asarray(dc, BF16)


def _rope_tables(length, nheads):
    t = np.arange(length)
    pos = np.stack([t // GRID_W, t % GRID_W], axis=1).astype(np.float32)
    n_freq = HEAD_DIM // 4
    inv = (1.0 / (ROPE_BASE ** (np.arange(n_freq, dtype=np.float32) / n_freq))).astype(np.float32)
    ang = pos[:, :, None] * inv[None, None, :]
    cos = np.repeat(np.cos(ang)[:, :, None, :], 2, axis=2).reshape(length, HEAD_DIM)
    sin = np.sin(ang)
    sin = np.stack([-sin, sin], axis=2).reshape(length, HEAD_DIM)
    return (jnp.asarray(np.tile(cos, (1, nheads)), F32), jnp.asarray(np.tile(sin, (1, nheads)), F32))


def kernel(x_prompt, x_sample, cache_k, cache_v, state_ssm_re, state_ssm_im, c, c_ctx, w_ada, b_ada, w_in, conv_w, ssm_lam_re, ssm_lam_im, ssm_log_step, ssm_b_re, ssm_b_im, ssm_c_re, ssm_c_im, ssm_d, ssm_w_glu, attn_sink, w_branch, w_out, ln1_g, ln1_b, ln2_g, ln2_b, peer_wq, peer_subkeys, peer_u, peer_v):
    nb, seq, _ = x_prompt.shape
    nd, lseq, _ = x_sample.shape
    depth = w_in.shape[0]
    t_ctx = nb * seq
    t_all = t_ctx + nd * lseq
    assert t_ctx % lseq == 0 and t_all % PEER_TT == 0 and (2 * seq) % PEER_TT == 0
    assert lseq % SSM_TCHUNK == 0 and seq == SSM_TCHUNK

    x = jnp.concatenate([x_prompt.reshape(t_ctx, D_MODEL), x_sample.reshape(nd * lseq, D_MODEL)], axis=0)

    nrow = -(-(1 + nd) // 8) * 8
    cvecs = jnp.concatenate([c_ctx[None, :], c, jnp.zeros((nrow - 1 - nd, D_MODEL), F32)], axis=0)
    mods_all = _modulation(cvecs, w_ada, b_ada).reshape(depth, nrow, 6, D_MODEL)

    def mod_row_for(tile):
        nctx = t_ctx // tile
        per = lseq // tile
        return lambda i: jnp.where(i < nctx, 0, 1 + (i - nctx) // per)

    mod_row = mod_row_for(ROW_TILE)
    mod_row_tt = mod_row_for(PEER_TT)

    dl_ctx, dft_c = _dft_tables(seq)
    dl_lat, _ = _dft_tables(lseq)
    cosq, sinq = _rope_tables(lseq, N_HEADS)
    cosk, sin_k = _rope_tables(lseq, N_KV)

    gate0 = sum((512,) * 6) + 2 * N_KV * HEAD_DIM
    w_in_b = jnp.concatenate([w_in[:, :, gate0:], w_in[:, :, :gate0]], axis=2).astype(BF16)
    w_glu_b = ssm_w_glu.astype(BF16)
    w_branch_b = w_branch.astype(BF16)
    w_out_b = w_out.astype(BF16)
    wq_t = peer_wq.transpose(0, 2, 1).astype(BF16)
    keys = peer_subkeys.reshape(depth, 2 * PEER_HEADS, N_KEYS, KEY_DIM // 2).astype(BF16)
    u_b = peer_u.astype(BF16)
    vt_b = peer_v.transpose(0, 2, 1).astype(BF16)
    sp = _s5_params(ssm_lam_re, ssm_lam_im, ssm_log_step, ssm_b_re, ssm_b_im, ssm_c_re, ssm_c_im)
    sink = attn_sink.reshape(depth, 1, N_HEADS)
    ssm_d3 = ssm_d.reshape(depth, 1, -1)
    ln1_g3, ln1_b3 = ln1_g.reshape(depth, 1, -1), ln1_b.reshape(depth, 1, -1)
    ln2_g3, ln2_b3 = ln2_g.reshape(depth, 1, -1), ln2_b.reshape(depth, 1, -1)
    ck = cache_k.reshape(nd, depth, -1, N_KV * HEAD_DIM)
    cv = cache_v.reshape(nd, depth, -1, N_KV * HEAD_DIM)
    nbp_c = -(-nb // SSM_BROWS) * SSM_BROWS
    nbp_l = -(-nd // SSM_BROWS) * SSM_BROWS
    zero_state = jnp.zeros((SSM_GROUPS // SSM_GBLK, nbp_c, 2 * SSM_GBLK * SSM_STATE), F32)

    new_k, new_v, new_re, new_im = [], [], [], []
    for l in range(depth):
        z = _in_proj(x, mods_all, w_in_b, mod_row, l)

        yf, yc, ya = _mixer_ctx(z, nb, seq, conv_w, sink, dl_ctx, dft_c, l)
        yf, yc = _fftconv_lat(z, t_ctx, nd, lseq, conv_w, dl_lat, dft_c, yf, yc, l)
        ya = _attn_lat(z, t_ctx, nd, lseq, ck, cv, cosq, sinq, cosk, sin_k, sink, ya, l)

        zs = z[:, ZS_BLK * 512:(ZS_BLK + 1) * 512]
        u_c, _ = _to_time_major(zs[:t_ctx], nb, seq)
        ysf_c, ysb_c, hf_c, hb_c = _s5(u_c, zero_state, zero_state, sp, nbp_c // SSM_BROWS,
                                       seq // SSM_TCHUNK, l)
        u_l, _ = _to_time_major(zs[t_ctx:], nd, lseq)
        h0f = _state_to_blocks(state_ssm_re[:, l, 0], state_ssm_im[:, l, 0], nbp_l)
        h0b = _state_to_blocks(state_ssm_re[:, l, 1], state_ssm_im[:, l, 1], nbp_l)
        ysf_l, ysb_l, _, _ = _s5(u_l, h0f, h0b, sp, nbp_l // SSM_BROWS, lseq // SSM_TCHUNK, l)
        ysf = jnp.concatenate([_from_time_major(ysf_c, nb, nbp_c, seq), _from_time_major(ysf_l, nd, nbp_l, lseq)], axis=0)
        ysb = jnp.concatenate([_from_time_major(ysb_c, nb, nbp_c, seq), _from_time_major(ysb_l, nd, nbp_l, lseq)], axis=0)

        x1, xm2 = _merge(x, mods_all, mod_row, yf, yc, ya, ysf, ysb, z,
                         ssm_d3, w_glu_b, w_branch_b, w_out_b, ln1_g3, ln1_b3, l)

        routing = _route(xm2, wq_t, keys, l)
        x = _peer(xm2, u_b, vt_b, routing, x1, mods_all, mod_row_tt, ln2_g3, ln2_b3, l)

        kv = z[:t_ctx, ZK_BLK * 128:(ZV_BLK + 1) * 128].reshape(nb, seq, 2, N_KV, HEAD_DIM)
        new_k.append(kv[:, :, 0])
        new_v.append(kv[:, :, 1])
        fre, fim = _blocks_to_state(hf_c, nb)
        bre, bim = _blocks_to_state(hb_c, nb)
        new_re.append(jnp.stack([fre, bre], axis=1))
        new_im.append(jnp.stack([fim, bim], axis=1))

    return (x[:t_ctx].reshape(nb, seq, D_MODEL), x[t_ctx:].reshape(nd, lseq, D_MODEL),
            jnp.stack(new_k, axis=1), jnp.stack(new_v, axis=1),
            jnp.stack(new_re, axis=1), jnp.stack(new_im, axis=1))
```

```python
import functools
import math

import numpy as np
import jax
import jax.numpy as jnp
from jax import lax
from jax.experimental import pallas as pl
from jax.experimental.pallas import tpu as pltpu

F32 = jnp.float32
BF16 = jnp.bfloat16

D_MODEL = 1024
GRID_W = 64
N_BRANCH = 4
BRANCH_WIDTH = 512
FFT_GROUPS = 4
FFT_GROUP_CH = 128
CONV_K = 3
SSM_GROUPS = 32
SSM_CH = 16
SSM_STATE = 64
N_HEADS = 8
N_KV = 2
Q_PER_KV = N_HEADS // N_KV
HEAD_DIM = 64
WINDOW = 128
ATT_BLOCK = 128
ROPE_BASE = 10000.0
PEER_HEADS = 8
N_KEYS = 128
N_EXPERTS = N_KEYS * N_KEYS
PEER_TOPK = 16
KEY_DIM = 256
LN_EPS = 1e-5
NEG_INF = -1e30

Z_COLS = N_BRANCH * D_MODEL + 6 * BRANCH_WIDTH + 2 * N_KV * HEAD_DIM
ZG_BLK = 0
ZF_BLK, ZB_BLK, ZC_BLK, ZH_BLK, ZS_BLK, ZQ_BLK = 8, 9, 10, 11, 12, 13
ZK_BLK, ZV_BLK = 56, 57

V7X_VMEM_LIMIT_BYTES = 56 * 1024 * 1024
SSM_GBLK = 8
SSM_TCHUNK = 256
SSM_BROWS = 8
ROW_TILE = 256
IN_TILE = 512
PEER_TT = 512
PEER_EB = 2048
PEER_CHUNKS = 4
ROUTE_TT = 256


def _params(sem):
    return pltpu.CompilerParams(dimension_semantics=sem, vmem_limit_bytes=V7X_VMEM_LIMIT_BYTES)


def _gelu(x):
    return 0.5 * x * (1.0 + jnp.tanh(0.7978845608028654 * (x + 0.044715 * (x * x * x))))


def _layer_norm(h, g, b):
    mu = jnp.mean(h, axis=-1, keepdims=True)
    hc = h - mu
    var = jnp.mean(hc * hc, axis=-1, keepdims=True)
    return hc * lax.rsqrt(var + LN_EPS) * g + b


def _mod_kernel(c_ref, w_ref, b_ref, o_ref):
    cv = c_ref[...]
    s = (cv * jax.nn.sigmoid(cv)).astype(BF16)
    o_ref[0] = jnp.dot(s, w_ref[0].astype(BF16), preferred_element_type=F32) + b_ref[0]


def _modulation(cvecs, w_ada, b_ada):
    depth = w_ada.shape[0]
    nrow = cvecs.shape[0]
    return pl.pallas_call(
        _mod_kernel,
        grid=(depth, 6),
        in_specs=[
            pl.BlockSpec((nrow, D_MODEL), lambda l, j: (0, 0)),
            pl.BlockSpec((1, D_MODEL, D_MODEL), lambda l, j: (l, 0, j)),
            pl.BlockSpec((1, 1, D_MODEL), lambda l, j: (l, 0, j)),
        ],
        out_specs=pl.BlockSpec((1, nrow, D_MODEL), lambda l, j: (l, 0, j)),
        out_shape=jax.ShapeDtypeStruct((depth, nrow, 6 * D_MODEL), F32),
        compiler_params=_params(("parallel", "parallel")),
    )(cvecs, w_ada, b_ada.reshape(depth, 1, 6 * D_MODEL))


def _win_kernel(x_ref, mod_ref, w_ref, z_ref):
    sh = mod_ref[0, 0:1, :]
    sc = mod_ref[0, 1:2, :]
    xm = (x_ref[...] * (1.0 + sc) + sh).astype(BF16)
    z_ref[...] = jnp.dot(xm, w_ref[...], preferred_element_type=F32)


def _in_proj(x, mods, w_in, mod_row, l):
    t = x.shape[0]
    ncol = Z_COLS // 2
    return pl.pallas_call(
        _win_kernel,
        grid=(2, t // IN_TILE),
        in_specs=[
            pl.BlockSpec((IN_TILE, D_MODEL), lambda c, i: (i, 0)),
            pl.BlockSpec((None, 1, 6, D_MODEL), lambda c, i: (l, mod_row(i), 0, 0)),
            pl.BlockSpec((None, D_MODEL, ncol), lambda c, i: (l, 0, c)),
        ],
        out_specs=pl.BlockSpec((IN_TILE, ncol), lambda c, i: (i, c)),
        out_shape=jax.ShapeDtypeStruct((t, Z_COLS), F32),
        compiler_params=_params(("parallel", "parallel")),
    )(x, mods, w_in)


def _fft_conv(zf_ref, zb_ref, zc_ref, zh_ref, cw_ref, dl_ref, dc_ref, yf_ref, yc_ref):
    length = zf_ref.shape[0]
    zf = zf_ref[...].astype(BF16)
    ab = jnp.dot(zf, dc_ref[...], preferred_element_type=F32)
    ab = jnp.concatenate([ab[:, :BRANCH_WIDTH], ab[:, BRANCH_WIDTH:]], axis=0).astype(BF16)
    yf_ref[...] = jnp.dot(dl_ref[...], ab, preferred_element_type=F32).astype(BF16)
    g = zc_ref[...] * zh_ref[...]
    row = lax.broadcasted_iota(jnp.int32, g.shape, 0)
    prev = jnp.where(row == 0, 0.0, pltpu.roll(g, 1, 0))
    nxt = jnp.where(row == length - 1, 0.0, pltpu.roll(g, length - 1, 0))
    conv = cw_ref[0:1, :] * prev + cw_ref[1:2, :] * g + cw_ref[2:3, :] * nxt
    yc_ref[...] = (zb_ref[...] * conv).astype(BF16)


def _softmax_pv(s, sink, v):
    m = jnp.maximum(jnp.max(s, axis=1, keepdims=True), sink)
    p = jnp.exp(s - m)
    den = jnp.sum(p, axis=1, keepdims=True) + jnp.exp(sink - m)
    return jnp.dot(p.astype(BF16), v, preferred_element_type=F32) / den


def _mixer_ctx_kernel(zf_ref, zb_ref, zc_ref, zh_ref, zq_ref, zk_ref, zv_ref, cw_ref, sink_ref,
                      dl_ref, dc_ref, yf_in, yc_in, ya_in, yf_ref, yc_ref, ya_ref):
    del yf_in, yc_in, ya_in
    _fft_conv(zf_ref, zb_ref, zc_ref, zh_ref, cw_ref, dl_ref, dc_ref, yf_ref, yc_ref)
    q = zq_ref[...] * (HEAD_DIM ** -0.5)
    k = zk_ref[...]
    v = zv_ref[...]
    outs = []
    for h in range(N_HEADS):
        g = h // Q_PER_KV
        qh = q[:, h * HEAD_DIM:(h + 1) * HEAD_DIM].astype(BF16)
        kg = k[:, g * HEAD_DIM:(g + 1) * HEAD_DIM].astype(BF16)
        vg = v[:, g * HEAD_DIM:(g + 1) * HEAD_DIM].astype(BF16)
        s = lax.dot_general(qh, kg, (((1,), (1,)), ((), ())), preferred_element_type=F32)
        outs.append(_softmax_pv(s, sink_ref[0:1, h:h + 1], vg))
    ya_ref[...] = jnp.concatenate(outs, axis=1).astype(BF16)


def _zspec(rows, width, row_fn, col_blk):
    return pl.BlockSpec((rows, width), lambda *a: (row_fn(*a), col_blk))


def _mixer_ctx(z, nb, seq, conv_w, sink, dft_l, dft_c, l):
    rf = lambda b: b
    full = lambda shape: pl.BlockSpec(shape, lambda b: (0,) * len(shape))
    layer = lambda shape: pl.BlockSpec((None,) + shape, lambda b: (l,) + (0,) * len(shape))
    out = jax.ShapeDtypeStruct((z.shape[0], BRANCH_WIDTH), BF16)
    ospec = pl.BlockSpec((seq, BRANCH_WIDTH), lambda b: (b, 0))
    anyspec = pl.BlockSpec(memory_space=pl.ANY)
    zeros = [jnp.zeros(out.shape, BF16) for _ in range(3)]
    return pl.pallas_call(
        _mixer_ctx_kernel,
        grid=(nb,),
        in_specs=[
            _zspec(seq, 512, rf, ZF_BLK), _zspec(seq, 512, rf, ZB_BLK), _zspec(seq, 512, rf, ZC_BLK),
            _zspec(seq, 512, rf, ZH_BLK), _zspec(seq, 512, rf, ZQ_BLK),
            _zspec(seq, 128, rf, ZK_BLK), _zspec(seq, 128, rf, ZV_BLK),
            layer((CONV_K, BRANCH_WIDTH)), layer((1, N_HEADS)),
            full((seq, 2 * seq)), full((BRANCH_WIDTH, 2 * BRANCH_WIDTH)),
            anyspec, anyspec, anyspec,
        ],
        out_specs=[ospec, ospec, ospec],
        out_shape=[out, out, out],
        input_output_aliases={11: 0, 12: 1, 13: 2},
        compiler_params=_params(("parallel",)),
    )(z, z, z, z, z, z, z, conv_w, sink, dft_l, dft_c, *zeros)


def _fftconv_lat_kernel(zf_ref, zb_ref, zc_ref, zh_ref, cw_ref, dl_ref, dc_ref, yf_in, yc_in,
                        yf_ref, yc_ref):
    del yf_in, yc_in
    _fft_conv(zf_ref, zb_ref, zc_ref, zh_ref, cw_ref, dl_ref, dc_ref, yf_ref, yc_ref)


def _fftconv_lat(z, row0, nb, seq, conv_w, dft_l, dft_c, yf, yc, l):
    rf = lambda b: row0 // seq + b
    full = lambda shape: pl.BlockSpec(shape, lambda b: (0,) * len(shape))
    out = jax.ShapeDtypeStruct(yf.shape, BF16)
    ospec = pl.BlockSpec((seq, BRANCH_WIDTH), lambda b: (row0 // seq + b, 0))
    anyspec = pl.BlockSpec(memory_space=pl.ANY)
    return pl.pallas_call(
        _fftconv_lat_kernel,
        grid=(nb,),
        in_specs=[
            _zspec(seq, 512, rf, ZF_BLK), _zspec(seq, 512, rf, ZB_BLK), _zspec(seq, 512, rf, ZC_BLK),
            _zspec(seq, 512, rf, ZH_BLK),
            pl.BlockSpec((None, CONV_K, BRANCH_WIDTH), lambda b: (l, 0, 0)),
            full((seq, 2 * seq)), full((BRANCH_WIDTH, 2 * BRANCH_WIDTH)),
            anyspec, anyspec,
        ],
        out_specs=[ospec, ospec],
        out_shape=[out, out],
        input_output_aliases={7: 0, 8: 1},
        compiler_params=_params(("parallel",)),
    )(z, z, z, z, conv_w, dft_l, dft_c, yf, yc)


def _rope(x, cos, sin):
    lane = lax.broadcasted_iota(jnp.int32, (x.shape[0], 128), 1)
    first = (lane & 31) < 16
    parts = []
    for c in range(x.shape[1] // 128):
        xc = x[:, c * 128:(c + 1) * 128]
        swapped = jnp.where(first, pltpu.roll(xc, 112, 1), pltpu.roll(xc, 16, 1))
        parts.append(xc * cos[:, c * 128:(c + 1) * 128] + swapped * sin[:, c * 128:(c + 1) * 128])
    return parts[0] if len(parts) == 1 else jnp.concatenate(parts, axis=1)


def _attn_lat_kernel(zq_ref, zk_ref, zv_ref, ck_ref, cv_ref, cosq_ref, sinq_ref, cosk_ref, sinkk_ref,
                     sink_ref, ya_in, ya_ref):
    del ya_in
    n = pl.program_id(1)
    nblk = pl.num_programs(1)
    nwin = 3 * ATT_BLOCK
    q = _rope(zq_ref[...], cosq_ref[...], sinq_ref[...]) * (HEAD_DIM ** -0.5)
    ws = pl.multiple_of(jnp.clip(n - 1, 0, nblk - 3) * ATT_BLOCK, ATT_BLOCK)
    kw = _rope(zk_ref[pl.ds(ws, nwin), :], cosk_ref[pl.ds(ws, nwin), :], sinkk_ref[pl.ds(ws, nwin), :])
    vw = zv_ref[pl.ds(ws, nwin), :]
    k_all = jnp.concatenate([kw, ck_ref[...]], axis=0)
    v_all = jnp.concatenate([vw, cv_ref[...]], axis=0)
    nkey = k_all.shape[0]
    qpos = n * ATT_BLOCK + lax.broadcasted_iota(jnp.int32, (ATT_BLOCK, nkey), 0)
    col = lax.broadcasted_iota(jnp.int32, (ATT_BLOCK, nkey), 1)
    valid = (jnp.abs(qpos - (ws + col)) <= WINDOW) | (col >= nwin)
    outs = []
    for h in range(N_HEADS):
        g = h // Q_PER_KV
        qh = q[:, h * HEAD_DIM:(h + 1) * HEAD_DIM].astype(BF16)
        kg = k_all[:, g * HEAD_DIM:(g + 1) * HEAD_DIM].astype(BF16)
        vg = v_all[:, g * HEAD_DIM:(g + 1) * HEAD_DIM].astype(BF16)
        s = lax.dot_general(qh, kg, (((1,), (1,)), ((), ())), preferred_element_type=F32)
        s = jnp.where(valid, s, NEG_INF)
        outs.append(_softmax_pv(s, sink_ref[0:1, h:h + 1], vg))
    ya_ref[...] = jnp.concatenate(outs, axis=1).astype(BF16)


def _attn_lat(z, row0, nb, seq, ck, cv, cosq, sinq, cosk, sin_k, sink, ya, l):
    nblk = seq // ATT_BLOCK
    kvw = N_KV * HEAD_DIM
    past = ck.shape[2]
    full = lambda shape: pl.BlockSpec(shape, lambda b, n: (0,) * len(shape))
    cache = pl.BlockSpec((None, None, past, kvw), lambda b, n: (b, l, 0, 0))
    return pl.pallas_call(
        _attn_lat_kernel,
        grid=(nb, nblk),
        in_specs=[
            pl.BlockSpec((ATT_BLOCK, 512), lambda b, n: (row0 // ATT_BLOCK + b * nblk + n, ZQ_BLK)),
            pl.BlockSpec((seq, kvw), lambda b, n: (row0 // seq + b, ZK_BLK)),
            pl.BlockSpec((seq, kvw), lambda b, n: (row0 // seq + b, ZV_BLK)),
            cache, cache,
            pl.BlockSpec((ATT_BLOCK, 512), lambda b, n: (n, 0)),
            pl.BlockSpec((ATT_BLOCK, 512), lambda b, n: (n, 0)),
            full((seq, kvw)), full((seq, kvw)),
            pl.BlockSpec((None, 1, N_HEADS), lambda b, n: (l, 0, 0)),
            pl.BlockSpec(memory_space=pl.ANY),
        ],
        out_specs=pl.BlockSpec((ATT_BLOCK, 512), lambda b, n: (row0 // ATT_BLOCK + b * nblk + n, 0)),
        out_shape=jax.ShapeDtypeStruct(ya.shape, BF16),
        input_output_aliases={10: 0},
        compiler_params=_params(("parallel", "parallel")),
    )(z, z, z, ck, cv, cosq, sinq, cosk, sin_k, sink, ya)


def _s5_kernel(uf_ref, ub_ref, h0f_ref, h0b_ref, wbf_ref, wbb_ref, cf_ref, cb_ref, af_ref, ab_ref,
               yf_ref, yb_ref, hf_ref, hb_ref, buff, bufb, hst):
    c = pl.program_id(2)
    half = SSM_GBLK * SSM_STATE
    steps = uf_ref.shape[0] // SSM_BROWS

    @pl.when(c == 0)
    def _():
        hst[0] = h0f_ref[0]
        hst[1] = h0b_ref[0]

    buff[...] = jnp.dot(uf_ref[...].astype(BF16), wbf_ref[...], preferred_element_type=F32)
    bufb[...] = jnp.dot(ub_ref[...].astype(BF16), wbb_ref[...], preferred_element_type=F32)
    afr = jnp.broadcast_to(af_ref[0:1, :], (SSM_BROWS, half))
    afi = jnp.broadcast_to(af_ref[1:2, :], (SSM_BROWS, half))
    abr = jnp.broadcast_to(ab_ref[0:1, :], (SSM_BROWS, half))
    abi = jnp.broadcast_to(ab_ref[1:2, :], (SSM_BROWS, half))

    def step(t, carry):
        hfr, hfi, hbr, hbi = carry
        rf = pl.multiple_of(t * SSM_BROWS, SSM_BROWS)
        nfr = afr * hfr - afi * hfi + buff[pl.ds(rf, SSM_BROWS), 0:half]
        nfi = afr * hfi + afi * hfr + buff[pl.ds(rf, SSM_BROWS), half:2 * half]
        buff[pl.ds(rf, SSM_BROWS), 0:half] = nfr
        buff[pl.ds(rf, SSM_BROWS), half:2 * half] = nfi
        rb = pl.multiple_of((steps - 1 - t) * SSM_BROWS, SSM_BROWS)
        nbr = abr * hbr - abi * hbi + bufb[pl.ds(rb, SSM_BROWS), 0:half]
        nbi = abr * hbi + abi * hbr + bufb[pl.ds(rb, SSM_BROWS), half:2 * half]
        bufb[pl.ds(rb, SSM_BROWS), 0:half] = nbr
        bufb[pl.ds(rb, SSM_BROWS), half:2 * half] = nbi
        return nfr, nfi, nbr, nbi

    init = (hst[0, :, 0:half], hst[0, :, half:2 * half], hst[1, :, 0:half], hst[1, :, half:2 * half])
    hfr, hfi, hbr, hbi = lax.fori_loop(0, steps, step, init, unroll=4)
    hst[0, :, 0:half] = hfr
    hst[0, :, half:2 * half] = hfi
    hst[1, :, 0:half] = hbr
    hst[1, :, half:2 * half] = hbi
    yf_ref[...] = jnp.dot(buff[...].astype(BF16), cf_ref[...], preferred_element_type=F32)
    yb_ref[...] = jnp.dot(bufb[...].astype(BF16), cb_ref[...], preferred_element_type=F32)

    @pl.when(c == pl.num_programs(2) - 1)
    def _():
        hf_ref[0] = hst[0]
        hb_ref[0] = hst[1]


def _s5(u_tm, h0f, h0b, sp, nbb, nchunk, l):
    rows = SSM_TCHUNK * SSM_BROWS
    ngb = SSM_GROUPS // SSM_GBLK
    width = 2 * SSM_GBLK * SSM_STATE
    nbrow = nbb * SSM_BROWS
    cw = SSM_GBLK * SSM_CH
    fwd = lambda bb, j, c: (bb * nchunk + c, j)
    bwd = lambda bb, j, c: (bb * nchunk + nchunk - 1 - c, j)
    par = lambda shape, d: pl.BlockSpec((None, None, None) + shape, lambda bb, j, c: (l, d, j, 0, 0))
    st = pl.BlockSpec((1, SSM_BROWS, width), lambda bb, j, c: (j, bb, 0))
    wb, cm, a = sp
    ysh = jax.ShapeDtypeStruct(u_tm.shape, F32)
    hsh = jax.ShapeDtypeStruct((ngb, nbrow, width), F32)
    return pl.pallas_call(
        _s5_kernel,
        grid=(nbb, ngb, nchunk),
        in_specs=[
            pl.BlockSpec((rows, cw), fwd), pl.BlockSpec((rows, cw), bwd), st, st,
            par((cw, width), 0), par((cw, width), 1), par((width, cw), 0), par((width, cw), 1),
            par((2, width // 2), 0), par((2, width // 2), 1),
        ],
        out_specs=[pl.BlockSpec((rows, cw), fwd), pl.BlockSpec((rows, cw), bwd), st, st],
        out_shape=[ysh, ysh, hsh, hsh],
        scratch_shapes=[pltpu.VMEM((rows, width), F32), pltpu.VMEM((rows, width), F32),
                        pltpu.VMEM((2, SSM_BROWS, width), F32)],
        compiler_params=_params(("parallel", "parallel", "arbitrary")),
    )(u_tm, u_tm, h0f, h0b, wb, wb, cm, cm, a, a)


def _s5_params(lam_re, lam_im, log_step, b_re, b_im, c_re, c_im):
    lead = lam_re.shape[:-2]
    dt = jnp.exp(log_step)[..., None]
    mag = jnp.exp(lam_re * dt)
    ar = mag * jnp.cos(lam_im * dt)
    ai = mag * jnp.sin(lam_im * dt)
    den = lam_re * lam_re + lam_im * lam_im
    kr = ((ar - 1.0) * lam_re + ai * lam_im) / den
    ki = (ai * lam_re - (ar - 1.0) * lam_im) / den
    bbr = kr[..., None] * b_re - ki[..., None] * b_im
    bbi = kr[..., None] * b_im + ki[..., None] * b_re
    ngb = SSM_GROUPS // SSM_GBLK
    eye = jnp.eye(SSM_GBLK, dtype=F32)

    def blockdiag_in(m):
        m = m.reshape(lead + (ngb, SSM_GBLK, SSM_STATE, SSM_CH))
        m = jnp.einsum("...jgph,gk->...jghkp", m, eye)
        return m.reshape(lead + (ngb, SSM_GBLK * SSM_CH, SSM_GBLK * SSM_STATE))

    def blockdiag_out(m):
        m = m.reshape(lead + (ngb, SSM_GBLK, SSM_CH, SSM_STATE))
        m = jnp.einsum("...jghp,gk->...jgpkh", m, eye)
        return m.reshape(lead + (ngb, SSM_GBLK * SSM_STATE, SSM_GBLK * SSM_CH))

    wb = jnp.concatenate([blockdiag_in(bbr), blockdiag_in(bbi)], axis=-1).astype(BF16)
    cm = jnp.concatenate([blockdiag_out(c_re), -blockdiag_out(c_im)], axis=-2).astype(BF16)
    a = jnp.stack([ar.reshape(lead + (ngb, -1)), ai.reshape(lead + (ngb, -1))], axis=-2)
    return wb, cm, a


def _to_time_major(u, nb, seq):
    cdim = u.shape[1]
    nbp = -(-nb // SSM_BROWS) * SSM_BROWS
    u = u.reshape(nb, seq, cdim)
    if nbp != nb:
        u = jnp.pad(u, ((0, nbp - nb), (0, 0), (0, 0)))
    u = u.reshape(nbp // SSM_BROWS, SSM_BROWS, seq, cdim).transpose(0, 2, 1, 3)
    return u.reshape(nbp * seq, cdim), nbp


def _from_time_major(y, nb, nbp, seq):
    cdim = y.shape[1]
    y = y.reshape(nbp // SSM_BROWS, seq, SSM_BROWS, cdim).transpose(0, 2, 1, 3)
    return y.reshape(nbp, seq, cdim)[:nb].reshape(nb * seq, cdim)


def _state_to_blocks(re, im, nbp):
    nb = re.shape[0]
    ngb = SSM_GROUPS // SSM_GBLK
    def blk(x):
        return x.reshape(nb, ngb, SSM_GBLK * SSM_STATE).transpose(1, 0, 2)
    h = jnp.concatenate([blk(re), blk(im)], axis=2)
    if nbp != nb:
        h = jnp.pad(h, ((0, 0), (0, nbp - nb), (0, 0)))
    return h


def _blocks_to_state(h, nb):
    half = SSM_GBLK * SSM_STATE
    def unblk(x):
        return x[:, :nb].transpose(1, 0, 2).reshape(nb, SSM_GROUPS, SSM_STATE)
    return unblk(h[:, :, :half]), unblk(h[:, :, half:])


def _merge_kernel(x_ref, mod_ref, yf_ref, yc_ref, ya_ref, ysf_ref, ysb_ref, zs_ref,
                  zg0_ref, zg1_ref, zg2_ref, zg3_ref, d_ref, wglu_ref, wb_ref, wout_ref,
                  g_ref, b_ref, x1_ref, xm_ref):
    ys = ysf_ref[...] + ysb_ref[...] + d_ref[...] * zs_ref[...]
    ys = _gelu(ys)
    yssm = ys * jax.nn.sigmoid(jnp.dot(ys.astype(BF16), wglu_ref[...], preferred_element_type=F32))
    acc = jax.nn.sigmoid(zg0_ref[...]) * jnp.dot(yf_ref[...], wb_ref[0], preferred_element_type=F32)
    acc += jax.nn.sigmoid(zg1_ref[...]) * jnp.dot(yc_ref[...], wb_ref[1], preferred_element_type=F32)
    acc += jax.nn.sigmoid(zg2_ref[...]) * jnp.dot(yssm.astype(BF16), wb_ref[2], preferred_element_type=F32)
    acc += jax.nn.sigmoid(zg3_ref[...]) * jnp.dot(ya_ref[...], wb_ref[3], preferred_element_type=F32)
    mix = jnp.dot(acc.astype(BF16), wout_ref[...], preferred_element_type=F32)
    alpha = (2 * 4) ** 0.25
    x1 = _layer_norm(alpha * x_ref[...] + mod_ref[0, 2:3, :] * mix, g_ref[...], b_ref[...])
    x1_ref[...] = x1
    xm_ref[...] = (x1 * (1.0 + mod_ref[0, 4:5, :]) + mod_ref[0, 3:4, :]).astype(BF16)


def _merge(x, mods, mod_row, yf, yc, ya, ysf, ysb, z, ssm_d, w_glu, w_branch, w_out, ln_g, ln_b, l):
    t = x.shape[0]
    rf = lambda i: i
    row = lambda w: pl.BlockSpec((ROW_TILE, w), lambda i: (i, 0))
    full = lambda shape: pl.BlockSpec((None,) + shape, lambda i: (l,) + (0,) * len(shape))
    return pl.pallas_call(
        _merge_kernel,
        grid=(t // ROW_TILE,),
        in_specs=[
            row(D_MODEL), pl.BlockSpec((None, 1, 6, D_MODEL), lambda i: (l, mod_row(i), 0, 0)),
            row(512), row(512), row(512), row(512), row(512),
            _zspec(ROW_TILE, 512, rf, ZS_BLK),
            _zspec(ROW_TILE, 1024, rf, 0), _zspec(ROW_TILE, 1024, rf, 1),
            _zspec(ROW_TILE, 1024, rf, 2), _zspec(ROW_TILE, 1024, rf, 3),
            full((1, 512)), full((512, 512)), full((N_BRANCH, 512, D_MODEL)), full((D_MODEL, D_MODEL)),
            full((1, D_MODEL)), full((1, D_MODEL)),
        ],
        out_specs=[row(D_MODEL), row(D_MODEL)],
        out_shape=[jax.ShapeDtypeStruct((t, D_MODEL), F32), jax.ShapeDtypeStruct((t, D_MODEL), BF16)],
        compiler_params=_params(("parallel",)),
    )(x, mods, yf, yc, ya, ysf, ysb, z, z, z, z, z, ssm_d, w_glu, w_branch, w_out, ln_g, ln_b)


def _top16(s):
    n, w = s.shape
    iota = lax.broadcasted_iota(jnp.int32, (n, w), 0).astype(F32)
    kio = lax.broadcasted_iota(jnp.int32, (PEER_TOPK, w), 0)

    def body(k, carry):
        work, rank, vals, _ = carry
        m = jnp.max(work, axis=0, keepdims=True)
        pos = jnp.min(jnp.where(work == m, iota, float(n)), axis=0, keepdims=True)
        hit = iota == pos
        rank = jnp.where(hit, lax.convert_element_type(k, F32), rank)
        work = jnp.where(hit, -jnp.inf, work)
        vals = jnp.where(kio == k, m, vals)
        return work, rank, vals, pos

    init = (s, jnp.full((n, w), 1e9, F32), jnp.zeros((PEER_TOPK, w), F32), jnp.zeros((1, w), F32))
    _, rank, vals, pos = lax.fori_loop(0, PEER_TOPK, body, init)
    return vals, rank, pos


def _max16(s):
    w = s.shape[1]
    kio = lax.broadcasted_iota(jnp.int32, (PEER_TOPK, w), 0)

    def body(k, carry):
        work, vals = carry
        m = jnp.max(work, axis=0, keepdims=True)
        return jnp.where(work == m, -jnp.inf, work), jnp.where(kio == k, m, vals)

    _, vals = lax.fori_loop(0, PEER_TOPK, body, (s, jnp.zeros((PEER_TOPK, w), F32)))
    return vals


_STAIR = [(j, PEER_TOPK // (j + 1)) for j in range(PEER_TOPK)]
_STAIR_ROWS = -(-sum(k for _, k in _STAIR) // 8) * 8


def _stair_candidates(v1, v2):
    w = v1.shape[1]
    rows = [v1[j:j + 1] + v2[0:k] for j, k in _STAIR]
    npad = _STAIR_ROWS - sum(k for _, k in _STAIR)
    return jnp.concatenate(rows + [jnp.full((npad, w), -jnp.inf, F32)], axis=0)


def _stair_positions(w):
    rows = [float(PEER_TOPK * j) + lax.broadcasted_iota(jnp.int32, (k, w), 0).astype(F32) for j, k in _STAIR]
    npad = _STAIR_ROWS - sum(k for _, k in _STAIR)
    return jnp.concatenate(rows + [jnp.full((npad, w), 1e9, F32)], axis=0)


def _next_up(x):
    b = lax.bitcast_convert_type(x, jnp.int32)
    up = jnp.where(x > 0.0, b + 1, jnp.where(x < 0.0, b - 1, jnp.int32(0x00800000)))
    return lax.bitcast_convert_type(up, F32)


def _route_kernel(xm_ref, wq_ref, keys_ref, s1m_ref, qrow_ref, e1_ref, s2m_ref, pb_ref, e2_ref,
                  thr_ref, qs):
    qs[...] = lax.dot_general(wq_ref[...], xm_ref[...], (((1,), (1,)), ((), ())),
                              preferred_element_type=F32)
    w = xm_ref.shape[0]

    def count(mask):
        return jnp.sum(mask.astype(F32), axis=0, keepdims=True)

    def emit(h, s1, s2, in1, in2, m1, m2, vc, qrow, pb, thr_up, bad):
        z = jnp.sum(jnp.exp(vc - vc[0:1]), axis=0, keepdims=True)
        s1m_ref[h] = jnp.where(in1, s1, -jnp.inf)
        s2m_ref[h] = jnp.where(in2, s2, -jnp.inf)
        e1_ref[h] = jnp.where(in1, jnp.exp(s1 - m1), 0.0) / z
        e2_ref[h] = jnp.where(in2, jnp.exp(s2 - m2), 0.0)
        qrow_ref[h] = qrow
        pb_ref[h] = pb
        thr = vc[PEER_TOPK - 1:PEER_TOPK]
        thr_ref[h] = jnp.concatenate([thr, thr_up, bad, jnp.zeros((5, w), F32)], axis=0)

    def head(h, carry):
        base = pl.multiple_of(h * KEY_DIM, KEY_DIM)
        q1 = qs[pl.ds(base, N_KEYS), :].astype(BF16)
        q2 = qs[pl.ds(base + N_KEYS, N_KEYS), :].astype(BF16)
        s1 = jnp.dot(keys_ref[2 * h], q1, preferred_element_type=F32)
        s2 = jnp.dot(keys_ref[2 * h + 1], q2, preferred_element_type=F32)

        v1 = _max16(s1)
        v2 = _max16(s2)
        in1 = s1 >= v1[PEER_TOPK - 1:PEER_TOPK]
        in2 = s2 >= v2[PEER_TOPK - 1:PEER_TOPK]
        cand = _stair_candidates(v1, v2)
        vc = _max16(cand)
        thr = vc[PEER_TOPK - 1:PEER_TOPK]
        zero = jnp.zeros((N_KEYS, w), F32)
        k = float(PEER_TOPK)
        bad = jnp.abs(count(in1) - k) + jnp.abs(count(in2) - k) + jnp.abs(count(cand >= thr) - k)
        emit(h, s1, s2, in1, in2, v1[0:1], v2[0:1], vc, zero, zero, thr, bad)

        @pl.when(jnp.max(bad) > 0.0)
        def _():
            xv1, r1, _ = _top16(s1)
            xv2, r2, _ = _top16(s2)
            xcand = _stair_candidates(xv1, xv2)
            xvc, _, prow = _top16(xcand)
            riota = lax.broadcasted_iota(jnp.int32, xcand.shape, 0).astype(F32)
            pthr = jnp.sum(jnp.where(riota == prow, _stair_positions(w), 0.0), axis=0, keepdims=True)
            emit(h, s1, s2, r1 < 100.0, r2 < 100.0, xv1[0:1], xv2[0:1], xvc,
                 pthr - k * r1, r2, _next_up(xvc[PEER_TOPK - 1:PEER_TOPK]), bad)

        return carry

    lax.fori_loop(0, PEER_HEADS, head, 0)


def _route(xm, wq_t, keys, l):
    t = xm.shape[0]
    big = jax.ShapeDtypeStruct((PEER_HEADS, N_KEYS, t), F32)
    bspec = pl.BlockSpec((PEER_HEADS, N_KEYS, ROUTE_TT), lambda i: (0, 0, i))
    return pl.pallas_call(
        _route_kernel,
        grid=(t // ROUTE_TT,),
        in_specs=[
            pl.BlockSpec((ROUTE_TT, D_MODEL), lambda i: (i, 0)),
            pl.BlockSpec((None, PEER_HEADS * KEY_DIM, D_MODEL), lambda i: (l, 0, 0)),
            pl.BlockSpec((None, 2 * PEER_HEADS, N_KEYS, N_KEYS), lambda i: (l, 0, 0, 0)),
        ],
        out_specs=[bspec] * 6 + [pl.BlockSpec((PEER_HEADS, 8, ROUTE_TT), lambda i: (0, 0, i))],
        out_shape=[big] * 6 + [jax.ShapeDtypeStruct((PEER_HEADS, 8, t), F32)],
        scratch_shapes=[pltpu.VMEM((PEER_HEADS * KEY_DIM, ROUTE_TT), F32)],
        compiler_params=_params(("parallel",)),
    )(xm, wq_t, keys)


def _peer_kernel(flag_ref, xm_ref, u_ref, vt_ref, s1m_ref, qrow_ref, e1_ref, s2m_ref, pb_ref, e2_ref,
                 thr_ref, x1_ref, mod_ref, g_ref, b_ref, o_ref, ht, wacc, pt, acc):
    j = pl.program_id(1)
    nrow = PEER_EB // N_KEYS

    @pl.when(j == 0)
    def _():
        acc[...] = jnp.zeros_like(acc)

    heads_per_chunk = PEER_HEADS // PEER_CHUNKS
    rpass = 4

    def rows_of(tile, r0):
        return jnp.stack([jnp.broadcast_to(tile[r:r + 1, :], (8, 128)) for r in range(r0, r0 + rpass)])

    def chunk_work(c, exact):
        for hh in range(heads_per_chunk):
            h = c * heads_per_chunk + hh
            for lg in range(PEER_TT // 128):
                lanes = slice(lg * 128, (lg + 1) * 128)
                thr = thr_ref[h, 0:1, lanes]
                s1t = s1m_ref[h, :, lanes]
                e1t = e1_ref[h, :, lanes]
                if exact:
                    thr_up = thr_ref[h, 1:2, lanes]
                    qrt = qrow_ref[h, :, lanes]
                for rp in range(0, nrow, rpass):
                    s1r = rows_of(s1t, rp)
                    e1r = rows_of(e1t, rp)
                    if exact:
                        qr = rows_of(qrt, rp)
                    for v in range(N_KEYS // 8):
                        sub = slice(v * 8, (v + 1) * 8)
                        sc = s1r + s2m_ref[h, sub, lanes][None]
                        if exact:
                            first = pb_ref[h, sub, lanes][None] <= qr
                            sel = sc >= jnp.where(first, thr, thr_up)
                        else:
                            sel = sc >= thr
                        gate = jnp.where(sel, e1r * e2_ref[h, sub, lanes][None], 0.0)
                        if h == 0:
                            wacc[rp:rp + rpass, sub, lanes] = gate
                        else:
                            wacc[rp:rp + rpass, sub, lanes] += gate

    hfull = lax.dot_general(u_ref[...], xm_ref[...], (((1,), (1,)), ((), ())),
                            preferred_element_type=F32)
    for r in range(nrow):
        ht[r] = hfull[r * N_KEYS:(r + 1) * N_KEYS, :]
    for c in range(PEER_CHUNKS):
        needs_ties = flag_ref[pl.program_id(0), c] > 0
        pl.when(needs_ties)(functools.partial(chunk_work, c, True))
        pl.when(jnp.logical_not(needs_ties))(functools.partial(chunk_work, c, False))
    for lg in range(PEER_TT // 128):
        lanes = slice(lg * 128, (lg + 1) * 128)
        for r in range(nrow):
            pt[r * N_KEYS:(r + 1) * N_KEYS, lanes] = (
                wacc[r, :, lanes] * _gelu(ht[r, :, lanes])).astype(BF16)
    acc[...] += jnp.dot(vt_ref[...], pt[...], preferred_element_type=F32)

    @pl.when(j == pl.num_programs(1) - 1)
    def _():
        alpha = (2 * 4) ** 0.25
        ff = acc[...].T
        o_ref[...] = _layer_norm(alpha * x1_ref[...] + mod_ref[0, 5:6, :] * ff, g_ref[...], b_ref[...])


def _peer(xm, u_b, vt_b, routing, x1, mods, mod_row_tt, ln_g, ln_b, l):
    t = xm.shape[0]
    once = pl.Buffered(1)
    tok = lambda w: pl.BlockSpec((PEER_TT, w), lambda i, j: (i, 0), pipeline_mode=once)
    rspec = pl.BlockSpec((PEER_HEADS, N_KEYS, PEER_TT), lambda i, j: (0, 0, i), pipeline_mode=once)
    rowspec = pl.BlockSpec((PEER_HEADS, PEER_EB // N_KEYS, PEER_TT), lambda i, j: (0, j, i))
    full = lambda shape: pl.BlockSpec((None,) + shape, lambda i, j: (l,) + (0,) * len(shape))
    nblk = N_EXPERTS // PEER_EB
    bad = routing[6][:, 2, :].reshape(PEER_CHUNKS, PEER_HEADS // PEER_CHUNKS, t // PEER_TT, PEER_TT)
    flags = (jnp.max(bad, axis=(1, 3)) > 0.0).astype(jnp.int32).T
    return pl.pallas_call(
        _peer_kernel,
        grid=(t // PEER_TT, nblk),
        in_specs=[
            pl.BlockSpec(memory_space=pltpu.SMEM),
            tok(D_MODEL),
            pl.BlockSpec((None, PEER_EB, D_MODEL), lambda i, j: (l, j, 0)),
            pl.BlockSpec((None, D_MODEL, PEER_EB), lambda i, j: (l, 0, j)),
            rowspec, rowspec, rowspec, rspec, rspec, rspec,
            pl.BlockSpec((PEER_HEADS, 8, PEER_TT), lambda i, j: (0, 0, i)),
            tok(D_MODEL),
            pl.BlockSpec((None, 1, 6, D_MODEL), lambda i, j: (l, mod_row_tt(i), 0, 0)),
            full((1, D_MODEL)), full((1, D_MODEL)),
        ],
        out_specs=pl.BlockSpec((PEER_TT, D_MODEL), lambda i, j: (i, 0)),
        out_shape=jax.ShapeDtypeStruct((t, D_MODEL), F32),
        scratch_shapes=[pltpu.VMEM((PEER_EB // N_KEYS, N_KEYS, PEER_TT), F32),
                        pltpu.VMEM((PEER_EB // N_KEYS, N_KEYS, PEER_TT), F32),
                        pltpu.VMEM((PEER_EB, PEER_TT), BF16),
                        pltpu.VMEM((D_MODEL, PEER_TT), F32)],
        compiler_params=_params(("parallel", "arbitrary")),
    )(flags, xm, u_b, vt_b, *routing, x1, mods, ln_g, ln_b)


def _dft_tables(length):
    n = np.arange(length)
    ang = 2.0 * np.pi * ((n[:, None] * n[None, :]) % length) / length
    dl = np.concatenate([np.cos(ang), -np.sin(ang)], axis=1) / math.sqrt(length)
    c = np.arange(FFT_GROUP_CH)
    angc = 2.0 * np.pi * ((c[:, None] * c[None, :]) % FFT_GROUP_CH) / FFT_GROUP_CH
    eye = np.eye(FFT_GROUPS)
    dc = np.concatenate([np.kron(eye, np.cos(angc)), np.kron(eye, np.sin(angc))], axis=1)
    dc = dc / math.sqrt(FFT_GROUP_CH)
    return jnp.asarray(dl, BF16), jnp.asarray(dc, BF16)


def _rope_tables(length, nheads):
    t = np.arange(length)
    pos = np.stack([t // GRID_W, t % GRID_W], axis=1).astype(np.float32)
    n_freq = HEAD_DIM // 4
    inv = (1.0 / (ROPE_BASE ** (np.arange(n_freq, dtype=np.float32) / n_freq))).astype(np.float32)
    ang = pos[:, :, None] * inv[None, None, :]
    cos = np.repeat(np.cos(ang)[:, :, None, :], 2, axis=2).reshape(length, HEAD_DIM)
    sin = np.sin(ang)
    sin = np.stack([-sin, sin], axis=2).reshape(length, HEAD_DIM)
    return (jnp.asarray(np.tile(cos, (1, nheads)), F32), jnp.asarray(np.tile(sin, (1, nheads)), F32))


def kernel(x_prompt, x_sample, cache_k, cache_v, state_ssm_re, state_ssm_im, c, c_ctx, w_ada, b_ada, w_in, conv_w, ssm_lam_re, ssm_lam_im, ssm_log_step, ssm_b_re, ssm_b_im, ssm_c_re, ssm_c_im, ssm_d, ssm_w_glu, attn_sink, w_branch, w_out, ln1_g, ln1_b, ln2_g, ln2_b, peer_wq, peer_subkeys, peer_u, peer_v):
    nb, seq, _ = x_prompt.shape
    nd, lseq, _ = x_sample.shape
    depth = w_in.shape[0]
    t_ctx = nb * seq
    t_all = t_ctx + nd * lseq
    assert t_ctx % lseq == 0 and t_all % PEER_TT == 0 and (2 * seq) % PEER_TT == 0
    assert lseq % SSM_TCHUNK == 0 and seq == SSM_TCHUNK

    x = jnp.concatenate([x_prompt.reshape(t_ctx, D_MODEL), x_sample.reshape(nd * lseq, D_MODEL)], axis=0)

    nrow = -(-(1 + nd) // 8) * 8
    cvecs = jnp.concatenate([c_ctx[None, :], c, jnp.zeros((nrow - 1 - nd, D_MODEL), F32)], axis=0)
    mods_all = _modulation(cvecs, w_ada, b_ada).reshape(depth, nrow, 6, D_MODEL)

    def mod_row_for(tile):
        nctx = t_ctx // tile
        per = lseq // tile
        return lambda i: jnp.where(i < nctx, 0, 1 + (i - nctx) // per)

    mod_row = mod_row_for(ROW_TILE)
    mod_row_tt = mod_row_for(PEER_TT)

    dl_ctx, dft_c = _dft_tables(seq)
    dl_lat, _ = _dft_tables(lseq)
    cosq, sinq = _rope_tables(lseq, N_HEADS)
    cosk, sin_k = _rope_tables(lseq, N_KV)

    gate0 = sum((512,) * 6) + 2 * N_KV * HEAD_DIM
    w_in_b = jnp.concatenate([w_in[:, :, gate0:], w_in[:, :, :gate0]], axis=2).astype(BF16)
    w_glu_b = ssm_w_glu.astype(BF16)
    w_branch_b = w_branch.astype(BF16)
    w_out_b = w_out.astype(BF16)
    wq_t = peer_wq.transpose(0, 2, 1).astype(BF16)
    keys = peer_subkeys.reshape(depth, 2 * PEER_HEADS, N_KEYS, KEY_DIM // 2).astype(BF16)
    u_b = peer_u.astype(BF16)
    vt_b = peer_v.transpose(0, 2, 1).astype(BF16)
    sp = _s5_params(ssm_lam_re, ssm_lam_im, ssm_log_step, ssm_b_re, ssm_b_im, ssm_c_re, ssm_c_im)
    sink = attn_sink.reshape(depth, 1, N_HEADS)
    ssm_d3 = ssm_d.reshape(depth, 1, -1)
    ln1_g3, ln1_b3 = ln1_g.reshape(depth, 1, -1), ln1_b.reshape(depth, 1, -1)
    ln2_g3, ln2_b3 = ln2_g.reshape(depth, 1, -1), ln2_b.reshape(depth, 1, -1)
    ck = cache_k.reshape(nd, depth, -1, N_KV * HEAD_DIM)
    cv = cache_v.reshape(nd, depth, -1, N_KV * HEAD_DIM)
    nbp_c = -(-nb // SSM_BROWS) * SSM_BROWS
    nbp_l = -(-nd // SSM_BROWS) * SSM_BROWS
    zero_state = jnp.zeros((SSM_GROUPS // SSM_GBLK, nbp_c, 2 * SSM_GBLK * SSM_STATE), F32)

    new_k, new_v, new_re, new_im = [], [], [], []
    for l in range(depth):
        z = _in_proj(x, mods_all, w_in_b, mod_row_for(IN_TILE), l)

        yf, yc, ya = _mixer_ctx(z, nb, seq, conv_w, sink, dl_ctx, dft_c, l)
        yf, yc = _fftconv_lat(z, t_ctx, nd, lseq, conv_w, dl_lat, dft_c, yf, yc, l)
        ya = _attn_lat(z, t_ctx, nd, lseq, ck, cv, cosq, sinq, cosk, sin_k, sink, ya, l)

        zs = z[:, ZS_BLK * 512:(ZS_BLK + 1) * 512]
        u_c, _ = _to_time_major(zs[:t_ctx], nb, seq)
        ysf_c, ysb_c, hf_c, hb_c = _s5(u_c, zero_state, zero_state, sp, nbp_c // SSM_BROWS,
                                       seq // SSM_TCHUNK, l)
        u_l, _ = _to_time_major(zs[t_ctx:], nd, lseq)
        h0f = _state_to_blocks(state_ssm_re[:, l, 0], state_ssm_im[:, l, 0], nbp_l)
        h0b = _state_to_blocks(state_ssm_re[:, l, 1], state_ssm_im[:, l, 1], nbp_l)
        ysf_l, ysb_l, _, _ = _s5(u_l, h0f, h0b, sp, nbp_l // SSM_BROWS, lseq // SSM_TCHUNK, l)
        ysf = jnp.concatenate([_from_time_major(ysf_c, nb, nbp_c, seq), _from_time_major(ysf_l, nd, nbp_l, lseq)], axis=0)
        ysb = jnp.concatenate([_from_time_major(ysb_c, nb, nbp_c, seq), _from_time_major(ysb_l, nd, nbp_l, lseq)], axis=0)

        x1, xm2 = _merge(x, mods_all, mod_row, yf, yc, ya, ysf, ysb, z,
                         ssm_d3, w_glu_b, w_branch_b, w_out_b, ln1_g3, ln1_b3, l)

        routing = _route(xm2, wq_t, keys, l)
        x = _peer(xm2, u_b, vt_b, routing, x1, mods_all, mod_row_tt, ln2_g3, ln2_b3, l)

        kv = z[:t_ctx, ZK_BLK * 128:(ZV_BLK + 1) * 128].reshape(nb, seq, 2, N_KV, HEAD_DIM)
        new_k.append(kv[:, :, 0])
        new_v.append(kv[:, :, 1])
        fre, fim = _blocks_to_state(hf_c, nb)
        bre, bim = _blocks_to_state(hb_c, nb)
        new_re.append(jnp.stack([fre, bre], axis=1))
        new_im.append(jnp.stack([fim, bim], axis=1))

    return (x[:t_ctx].reshape(nb, seq, D_MODEL), x[t_ctx:].reshape(nd, lseq, D_MODEL),
            jnp.stack(new_k, axis=1), jnp.stack(new_v, axis=1),
            jnp.stack(new_re, axis=1), jnp.stack(new_im, axis=1))
```

```python
import functools
import math

import numpy as np
import jax
import jax.numpy as jnp
from jax import lax
from jax.experimental import pallas as pl
from jax.experimental.pallas import tpu as pltpu

F32 = jnp.float32
BF16 = jnp.bfloat16

D_MODEL = 1024
GRID_W = 64
N_BRANCH = 4
BRANCH_WIDTH = 512
FFT_GROUPS = 4
FFT_GROUP_CH = 128
CONV_K = 3
SSM_GROUPS = 32
SSM_CH = 16
SSM_STATE = 64
N_HEADS = 8
N_KV = 2
Q_PER_KV = N_HEADS // N_KV
HEAD_DIM = 64
WINDOW = 128
ATT_BLOCK = 128
ROPE_BASE = 10000.0
PEER_HEADS = 8
N_KEYS = 128
N_EXPERTS = N_KEYS * N_KEYS
PEER_TOPK = 16
KEY_DIM = 256
LN_EPS = 1e-5
NEG_INF = -1e30

Z_COLS = N_BRANCH * D_MODEL + 6 * BRANCH_WIDTH + 2 * N_KV * HEAD_DIM
ZG_BLK = 0
ZF_BLK, ZB_BLK, ZC_BLK, ZH_BLK, ZS_BLK, ZQ_BLK = 8, 9, 10, 11, 12, 13
ZK_BLK, ZV_BLK = 56, 57

V7X_VMEM_LIMIT_BYTES = 56 * 1024 * 1024
SSM_GBLK = 8
SSM_TCHUNK = 256
SSM_BROWS = 8
ROW_TILE = 256
IN_TILE = 512
PEER_TT = 512
PEER_EB = 2048
ROUTE_TT = 256


def _params(sem):
    return pltpu.CompilerParams(dimension_semantics=sem, vmem_limit_bytes=V7X_VMEM_LIMIT_BYTES)


def _gelu(x):
    return 0.5 * x * (1.0 + jnp.tanh(0.7978845608028654 * (x + 0.044715 * (x * x * x))))


def _layer_norm(h, g, b):
    mu = jnp.mean(h, axis=-1, keepdims=True)
    hc = h - mu
    var = jnp.mean(hc * hc, axis=-1, keepdims=True)
    return hc * lax.rsqrt(var + LN_EPS) * g + b


def _mod_kernel(c_ref, w_ref, b_ref, o_ref):
    cv = c_ref[...]
    s = (cv * jax.nn.sigmoid(cv)).astype(BF16)
    o_ref[0] = jnp.dot(s, w_ref[0].astype(BF16), preferred_element_type=F32) + b_ref[0]


def _modulation(cvecs, w_ada, b_ada):
    depth = w_ada.shape[0]
    nrow = cvecs.shape[0]
    return pl.pallas_call(
        _mod_kernel,
        grid=(depth, 6),
        in_specs=[
            pl.BlockSpec((nrow, D_MODEL), lambda l, j: (0, 0)),
            pl.BlockSpec((1, D_MODEL, D_MODEL), lambda l, j: (l, 0, j)),
            pl.BlockSpec((1, 1, D_MODEL), lambda l, j: (l, 0, j)),
        ],
        out_specs=pl.BlockSpec((1, nrow, D_MODEL), lambda l, j: (l, 0, j)),
        out_shape=jax.ShapeDtypeStruct((depth, nrow, 6 * D_MODEL), F32),
        compiler_params=_params(("parallel", "parallel")),
    )(cvecs, w_ada, b_ada.reshape(depth, 1, 6 * D_MODEL))


def _win_kernel(x_ref, mod_ref, w_ref, z_ref):
    sh = mod_ref[0, 0:1, :]
    sc = mod_ref[0, 1:2, :]
    xm = (x_ref[...] * (1.0 + sc) + sh).astype(BF16)
    z_ref[...] = jnp.dot(xm, w_ref[...], preferred_element_type=F32)


def _in_proj(x, mods, w_in, mod_row, l):
    t = x.shape[0]
    ncol = Z_COLS // 2
    return pl.pallas_call(
        _win_kernel,
        grid=(2, t // IN_TILE),
        in_specs=[
            pl.BlockSpec((IN_TILE, D_MODEL), lambda c, i: (i, 0)),
            pl.BlockSpec((None, 1, 6, D_MODEL), lambda c, i: (l, mod_row(i), 0, 0)),
            pl.BlockSpec((None, D_MODEL, ncol), lambda c, i: (l, 0, c)),
        ],
        out_specs=pl.BlockSpec((IN_TILE, ncol), lambda c, i: (i, c)),
        out_shape=jax.ShapeDtypeStruct((t, Z_COLS), F32),
        compiler_params=_params(("parallel", "parallel")),
    )(x, mods, w_in)


def _fft_conv(zf_ref, zb_ref, zc_ref, zh_ref, cw_ref, dl_ref, dc_ref, yf_ref, yc_ref):
    length = zf_ref.shape[0]
    zf = zf_ref[...].astype(BF16)
    ab = jnp.dot(zf, dc_ref[...], preferred_element_type=F32)
    ab = jnp.concatenate([ab[:, :BRANCH_WIDTH], ab[:, BRANCH_WIDTH:]], axis=0).astype(BF16)
    yf_ref[...] = jnp.dot(dl_ref[...], ab, preferred_element_type=F32).astype(BF16)
    g = zc_ref[...] * zh_ref[...]
    row = lax.broadcasted_iota(jnp.int32, g.shape, 0)
    prev = jnp.where(row == 0, 0.0, pltpu.roll(g, 1, 0))
    nxt = jnp.where(row == length - 1, 0.0, pltpu.roll(g, length - 1, 0))
    conv = cw_ref[0:1, :] * prev + cw_ref[1:2, :] * g + cw_ref[2:3, :] * nxt
    yc_ref[...] = (zb_ref[...] * conv).astype(BF16)


def _softmax_pv(s, sink, v):
    m = jnp.maximum(jnp.max(s, axis=1, keepdims=True), sink)
    p = jnp.exp(s - m)
    den = jnp.sum(p, axis=1, keepdims=True) + jnp.exp(sink - m)
    return jnp.dot(p.astype(BF16), v, preferred_element_type=F32) / den


def _mixer_ctx_kernel(zf_ref, zb_ref, zc_ref, zh_ref, zq_ref, zk_ref, zv_ref, cw_ref, sink_ref,
                      dl_ref, dc_ref, yf_in, yc_in, ya_in, yf_ref, yc_ref, ya_ref):
    del yf_in, yc_in, ya_in
    _fft_conv(zf_ref, zb_ref, zc_ref, zh_ref, cw_ref, dl_ref, dc_ref, yf_ref, yc_ref)
    q = zq_ref[...] * (HEAD_DIM ** -0.5)
    k = zk_ref[...]
    v = zv_ref[...]
    outs = []
    for h in range(N_HEADS):
        g = h // Q_PER_KV
        qh = q[:, h * HEAD_DIM:(h + 1) * HEAD_DIM].astype(BF16)
        kg = k[:, g * HEAD_DIM:(g + 1) * HEAD_DIM].astype(BF16)
        vg = v[:, g * HEAD_DIM:(g + 1) * HEAD_DIM].astype(BF16)
        s = lax.dot_general(qh, kg, (((1,), (1,)), ((), ())), preferred_element_type=F32)
        outs.append(_softmax_pv(s, sink_ref[0:1, h:h + 1], vg))
    ya_ref[...] = jnp.concatenate(outs, axis=1).astype(BF16)


def _zspec(rows, width, row_fn, col_blk):
    return pl.BlockSpec((rows, width), lambda *a: (row_fn(*a), col_blk))


def _mixer_ctx(z, nb, seq, conv_w, sink, dft_l, dft_c, l):
    rf = lambda b: b
    full = lambda shape: pl.BlockSpec(shape, lambda b: (0,) * len(shape))
    layer = lambda shape: pl.BlockSpec((None,) + shape, lambda b: (l,) + (0,) * len(shape))
    out = jax.ShapeDtypeStruct((z.shape[0], BRANCH_WIDTH), BF16)
    ospec = pl.BlockSpec((seq, BRANCH_WIDTH), lambda b: (b, 0))
    anyspec = pl.BlockSpec(memory_space=pl.ANY)
    zeros = [jnp.zeros(out.shape, BF16) for _ in range(3)]
    return pl.pallas_call(
        _mixer_ctx_kernel,
        grid=(nb,),
        in_specs=[
            _zspec(seq, 512, rf, ZF_BLK), _zspec(seq, 512, rf, ZB_BLK), _zspec(seq, 512, rf, ZC_BLK),
            _zspec(seq, 512, rf, ZH_BLK), _zspec(seq, 512, rf, ZQ_BLK),
            _zspec(seq, 128, rf, ZK_BLK), _zspec(seq, 128, rf, ZV_BLK),
            layer((CONV_K, BRANCH_WIDTH)), layer((1, N_HEADS)),
            full((seq, 2 * seq)), full((BRANCH_WIDTH, 2 * BRANCH_WIDTH)),
            anyspec, anyspec, anyspec,
        ],
        out_specs=[ospec, ospec, ospec],
        out_shape=[out, out, out],
        input_output_aliases={11: 0, 12: 1, 13: 2},
        compiler_params=_params(("parallel",)),
    )(z, z, z, z, z, z, z, conv_w, sink, dft_l, dft_c, *zeros)


def _fftconv_lat_kernel(zf_ref, zb_ref, zc_ref, zh_ref, cw_ref, dl_ref, dc_ref, yf_in, yc_in,
                        yf_ref, yc_ref):
    del yf_in, yc_in
    _fft_conv(zf_ref, zb_ref, zc_ref, zh_ref, cw_ref, dl_ref, dc_ref, yf_ref, yc_ref)


def _fftconv_lat(z, row0, nb, seq, conv_w, dft_l, dft_c, yf, yc, l):
    rf = lambda b: row0 // seq + b
    full = lambda shape: pl.BlockSpec(shape, lambda b: (0,) * len(shape))
    out = jax.ShapeDtypeStruct(yf.shape, BF16)
    ospec = pl.BlockSpec((seq, BRANCH_WIDTH), lambda b: (row0 // seq + b, 0))
    anyspec = pl.BlockSpec(memory_space=pl.ANY)
    return pl.pallas_call(
        _fftconv_lat_kernel,
        grid=(nb,),
        in_specs=[
            _zspec(seq, 512, rf, ZF_BLK), _zspec(seq, 512, rf, ZB_BLK), _zspec(seq, 512, rf, ZC_BLK),
            _zspec(seq, 512, rf, ZH_BLK),
            pl.BlockSpec((None, CONV_K, BRANCH_WIDTH), lambda b: (l, 0, 0)),
            full((seq, 2 * seq)), full((BRANCH_WIDTH, 2 * BRANCH_WIDTH)),
            anyspec, anyspec,
        ],
        out_specs=[ospec, ospec],
        out_shape=[out, out],
        input_output_aliases={7: 0, 8: 1},
        compiler_params=_params(("parallel",)),
    )(z, z, z, z, conv_w, dft_l, dft_c, yf, yc)


def _rope(x, cos, sin):
    lane = lax.broadcasted_iota(jnp.int32, (x.shape[0], 128), 1)
    first = (lane & 31) < 16
    parts = []
    for c in range(x.shape[1] // 128):
        xc = x[:, c * 128:(c + 1) * 128]
        swapped = jnp.where(first, pltpu.roll(xc, 112, 1), pltpu.roll(xc, 16, 1))
        parts.append(xc * cos[:, c * 128:(c + 1) * 128] + swapped * sin[:, c * 128:(c + 1) * 128])
    return parts[0] if len(parts) == 1 else jnp.concatenate(parts, axis=1)


def _attn_lat_kernel(zq_ref, zk_ref, zv_ref, ck_ref, cv_ref, cosq_ref, sinq_ref, cosk_ref, sinkk_ref,
                     sink_ref, ya_in, ya_ref):
    del ya_in
    n = pl.program_id(1)
    nblk = pl.num_programs(1)
    nwin = 3 * ATT_BLOCK
    q = _rope(zq_ref[...], cosq_ref[...], sinq_ref[...]) * (HEAD_DIM ** -0.5)
    ws = pl.multiple_of(jnp.clip(n - 1, 0, nblk - 3) * ATT_BLOCK, ATT_BLOCK)
    kw = _rope(zk_ref[pl.ds(ws, nwin), :], cosk_ref[pl.ds(ws, nwin), :], sinkk_ref[pl.ds(ws, nwin), :])
    vw = zv_ref[pl.ds(ws, nwin), :]
    k_all = jnp.concatenate([kw, ck_ref[...]], axis=0)
    v_all = jnp.concatenate([vw, cv_ref[...]], axis=0)
    nkey = k_all.shape[0]
    qpos = n * ATT_BLOCK + lax.broadcasted_iota(jnp.int32, (ATT_BLOCK, nkey), 0)
    col = lax.broadcasted_iota(jnp.int32, (ATT_BLOCK, nkey), 1)
    valid = (jnp.abs(qpos - (ws + col)) <= WINDOW) | (col >= nwin)
    outs = []
    for h in range(N_HEADS):
        g = h // Q_PER_KV
        qh = q[:, h * HEAD_DIM:(h + 1) * HEAD_DIM].astype(BF16)
        kg = k_all[:, g * HEAD_DIM:(g + 1) * HEAD_DIM].astype(BF16)
        vg = v_all[:, g * HEAD_DIM:(g + 1) * HEAD_DIM].astype(BF16)
        s = lax.dot_general(qh, kg, (((1,), (1,)), ((), ())), preferred_element_type=F32)
        s = jnp.where(valid, s, NEG_INF)
        outs.append(_softmax_pv(s, sink_ref[0:1, h:h + 1], vg))
    ya_ref[...] = jnp.concatenate(outs, axis=1).astype(BF16)


def _attn_lat(z, row0, nb, seq, ck, cv, cosq, sinq, cosk, sin_k, sink, ya, l):
    nblk = seq // ATT_BLOCK
    kvw = N_KV * HEAD_DIM
    past = ck.shape[2]
    full = lambda shape: pl.BlockSpec(shape, lambda b, n: (0,) * len(shape))
    cache = pl.BlockSpec((None, None, past, kvw), lambda b, n: (b, l, 0, 0))
    return pl.pallas_call(
        _attn_lat_kernel,
        grid=(nb, nblk),
        in_specs=[
            pl.BlockSpec((ATT_BLOCK, 512), lambda b, n: (row0 // ATT_BLOCK + b * nblk + n, ZQ_BLK)),
            pl.BlockSpec((seq, kvw), lambda b, n: (row0 // seq + b, ZK_BLK)),
            pl.BlockSpec((seq, kvw), lambda b, n: (row0 // seq + b, ZV_BLK)),
            cache, cache,
            pl.BlockSpec((ATT_BLOCK, 512), lambda b, n: (n, 0)),
            pl.BlockSpec((ATT_BLOCK, 512), lambda b, n: (n, 0)),
            full((seq, kvw)), full((seq, kvw)),
            pl.BlockSpec((None, 1, N_HEADS), lambda b, n: (l, 0, 0)),
            pl.BlockSpec(memory_space=pl.ANY),
        ],
        out_specs=pl.BlockSpec((ATT_BLOCK, 512), lambda b, n: (row0 // ATT_BLOCK + b * nblk + n, 0)),
        out_shape=jax.ShapeDtypeStruct(ya.shape, BF16),
        input_output_aliases={10: 0},
        compiler_params=_params(("parallel", "parallel")),
    )(z, z, z, ck, cv, cosq, sinq, cosk, sin_k, sink, ya)


def _s5_kernel(uf_ref, ub_ref, h0f_ref, h0b_ref, wbf_ref, wbb_ref, cf_ref, cb_ref, af_ref, ab_ref,
               yf_ref, yb_ref, hf_ref, hb_ref, buff, bufb, hst):
    c = pl.program_id(2)
    half = SSM_GBLK * SSM_STATE
    steps = uf_ref.shape[0] // SSM_BROWS

    @pl.when(c == 0)
    def _():
        hst[0] = h0f_ref[0]
        hst[1] = h0b_ref[0]

    buff[...] = jnp.dot(uf_ref[...].astype(BF16), wbf_ref[...], preferred_element_type=F32)
    bufb[...] = jnp.dot(ub_ref[...].astype(BF16), wbb_ref[...], preferred_element_type=F32)
    afr = jnp.broadcast_to(af_ref[0:1, :], (SSM_BROWS, half))
    afi = jnp.broadcast_to(af_ref[1:2, :], (SSM_BROWS, half))
    abr = jnp.broadcast_to(ab_ref[0:1, :], (SSM_BROWS, half))
    abi = jnp.broadcast_to(ab_ref[1:2, :], (SSM_BROWS, half))

    def step(t, carry):
        hfr, hfi, hbr, hbi = carry
        rf = pl.multiple_of(t * SSM_BROWS, SSM_BROWS)
        nfr = afr * hfr - afi * hfi + buff[pl.ds(rf, SSM_BROWS), 0:half]
        nfi = afr * hfi + afi * hfr + buff[pl.ds(rf, SSM_BROWS), half:2 * half]
        buff[pl.ds(rf, SSM_BROWS), 0:half] = nfr
        buff[pl.ds(rf, SSM_BROWS), half:2 * half] = nfi
        rb = pl.multiple_of((steps - 1 - t) * SSM_BROWS, SSM_BROWS)
        nbr = abr * hbr - abi * hbi + bufb[pl.ds(rb, SSM_BROWS), 0:half]
        nbi = abr * hbi + abi * hbr + bufb[pl.ds(rb, SSM_BROWS), half:2 * half]
        bufb[pl.ds(rb, SSM_BROWS), 0:half] = nbr
        bufb[pl.ds(rb, SSM_BROWS), half:2 * half] = nbi
        return nfr, nfi, nbr, nbi

    init = (hst[0, :, 0:half], hst[0, :, half:2 * half], hst[1, :, 0:half], hst[1, :, half:2 * half])
    hfr, hfi, hbr, hbi = lax.fori_loop(0, steps, step, init, unroll=4)
    hst[0, :, 0:half] = hfr
    hst[0, :, half:2 * half] = hfi
    hst[1, :, 0:half] = hbr
    hst[1, :, half:2 * half] = hbi
    yf_ref[...] = jnp.dot(buff[...].astype(BF16), cf_ref[...], preferred_element_type=F32)
    yb_ref[...] = jnp.dot(bufb[...].astype(BF16), cb_ref[...], preferred_element_type=F32)

    @pl.when(c == pl.num_programs(2) - 1)
    def _():
        hf_ref[0] = hst[0]
        hb_ref[0] = hst[1]


def _s5(u_tm, h0f, h0b, sp, nbb, nchunk, l):
    rows = SSM_TCHUNK * SSM_BROWS
    ngb = SSM_GROUPS // SSM_GBLK
    width = 2 * SSM_GBLK * SSM_STATE
    nbrow = nbb * SSM_BROWS
    cw = SSM_GBLK * SSM_CH
    fwd = lambda bb, j, c: (bb * nchunk + c, j)
    bwd = lambda bb, j, c: (bb * nchunk + nchunk - 1 - c, j)
    par = lambda shape, d: pl.BlockSpec((None, None, None) + shape, lambda bb, j, c: (l, d, j, 0, 0))
    st = pl.BlockSpec((1, SSM_BROWS, width), lambda bb, j, c: (j, bb, 0))
    wb, cm, a = sp
    ysh = jax.ShapeDtypeStruct(u_tm.shape, F32)
    hsh = jax.ShapeDtypeStruct((ngb, nbrow, width), F32)
    return pl.pallas_call(
        _s5_kernel,
        grid=(nbb, ngb, nchunk),
        in_specs=[
            pl.BlockSpec((rows, cw), fwd), pl.BlockSpec((rows, cw), bwd), st, st,
            par((cw, width), 0), par((cw, width), 1), par((width, cw), 0), par((width, cw), 1),
            par((2, width // 2), 0), par((2, width // 2), 1),
        ],
        out_specs=[pl.BlockSpec((rows, cw), fwd), pl.BlockSpec((rows, cw), bwd), st, st],
        out_shape=[ysh, ysh, hsh, hsh],
        scratch_shapes=[pltpu.VMEM((rows, width), F32), pltpu.VMEM((rows, width), F32),
                        pltpu.VMEM((2, SSM_BROWS, width), F32)],
        compiler_params=_params(("parallel", "parallel", "arbitrary")),
    )(u_tm, u_tm, h0f, h0b, wb, wb, cm, cm, a, a)


def _s5_params(lam_re, lam_im, log_step, b_re, b_im, c_re, c_im):
    lead = lam_re.shape[:-2]
    dt = jnp.exp(log_step)[..., None]
    mag = jnp.exp(lam_re * dt)
    ar = mag * jnp.cos(lam_im * dt)
    ai = mag * jnp.sin(lam_im * dt)
    den = lam_re * lam_re + lam_im * lam_im
    kr = ((ar - 1.0) * lam_re + ai * lam_im) / den
    ki = (ai * lam_re - (ar - 1.0) * lam_im) / den
    bbr = kr[..., None] * b_re - ki[..., None] * b_im
    bbi = kr[..., None] * b_im + ki[..., None] * b_re
    ngb = SSM_GROUPS // SSM_GBLK
    eye = jnp.eye(SSM_GBLK, dtype=F32)

    def blockdiag_in(m):
        m = m.reshape(lead + (ngb, SSM_GBLK, SSM_STATE, SSM_CH))
        m = jnp.einsum("...jgph,gk->...jghkp", m, eye)
        return m.reshape(lead + (ngb, SSM_GBLK * SSM_CH, SSM_GBLK * SSM_STATE))

    def blockdiag_out(m):
        m = m.reshape(lead + (ngb, SSM_GBLK, SSM_CH, SSM_STATE))
        m = jnp.einsum("...jghp,gk->...jgpkh", m, eye)
        return m.reshape(lead + (ngb, SSM_GBLK * SSM_STATE, SSM_GBLK * SSM_CH))

    wb = jnp.concatenate([blockdiag_in(bbr), blockdiag_in(bbi)], axis=-1).astype(BF16)
    cm = jnp.concatenate([blockdiag_out(c_re), -blockdiag_out(c_im)], axis=-2).astype(BF16)
    a = jnp.stack([ar.reshape(lead + (ngb, -1)), ai.reshape(lead + (ngb, -1))], axis=-2)
    return wb, cm, a


def _to_time_major(u, nb, seq):
    cdim = u.shape[1]
    nbp = -(-nb // SSM_BROWS) * SSM_BROWS
    u = u.reshape(nb, seq, cdim)
    if nbp != nb:
        u = jnp.pad(u, ((0, nbp - nb), (0, 0), (0, 0)))
    u = u.reshape(nbp // SSM_BROWS, SSM_BROWS, seq, cdim).transpose(0, 2, 1, 3)
    return u.reshape(nbp * seq, cdim), nbp


def _from_time_major(y, nb, nbp, seq):
    cdim = y.shape[1]
    y = y.reshape(nbp // SSM_BROWS, seq, SSM_BROWS, cdim).transpose(0, 2, 1, 3)
    return y.reshape(nbp, seq, cdim)[:nb].reshape(nb * seq, cdim)


def _state_to_blocks(re, im, nbp):
    nb = re.shape[0]
    ngb = SSM_GROUPS // SSM_GBLK
    def blk(x):
        return x.reshape(nb, ngb, SSM_GBLK * SSM_STATE).transpose(1, 0, 2)
    h = jnp.concatenate([blk(re), blk(im)], axis=2)
    if nbp != nb:
        h = jnp.pad(h, ((0, 0), (0, nbp - nb), (0, 0)))
    return h


def _blocks_to_state(h, nb):
    half = SSM_GBLK * SSM_STATE
    def unblk(x):
        return x[:, :nb].transpose(1, 0, 2).reshape(nb, SSM_GROUPS, SSM_STATE)
    return unblk(h[:, :, :half]), unblk(h[:, :, half:])


def _merge_kernel(x_ref, mod_ref, yf_ref, yc_ref, ya_ref, ysf_ref, ysb_ref, zs_ref,
                  zg0_ref, zg1_ref, zg2_ref, zg3_ref, d_ref, wglu_ref, wb_ref, wout_ref,
                  g_ref, b_ref, x1_ref, xm_ref):
    ys = ysf_ref[...] + ysb_ref[...] + d_ref[...] * zs_ref[...]
    ys = _gelu(ys)
    yssm = ys * jax.nn.sigmoid(jnp.dot(ys.astype(BF16), wglu_ref[...], preferred_element_type=F32))
    acc = jax.nn.sigmoid(zg0_ref[...]) * jnp.dot(yf_ref[...], wb_ref[0], preferred_element_type=F32)
    acc += jax.nn.sigmoid(zg1_ref[...]) * jnp.dot(yc_ref[...], wb_ref[1], preferred_element_type=F32)
    acc += jax.nn.sigmoid(zg2_ref[...]) * jnp.dot(yssm.astype(BF16), wb_ref[2], preferred_element_type=F32)
    acc += jax.nn.sigmoid(zg3_ref[...]) * jnp.dot(ya_ref[...], wb_ref[3], preferred_element_type=F32)
    mix = jnp.dot(acc.astype(BF16), wout_ref[...], preferred_element_type=F32)
    alpha = (2 * 4) ** 0.25
    x1 = _layer_norm(alpha * x_ref[...] + mod_ref[0, 2:3, :] * mix, g_ref[...], b_ref[...])
    x1_ref[...] = x1
    xm_ref[...] = (x1 * (1.0 + mod_ref[0, 4:5, :]) + mod_ref[0, 3:4, :]).astype(BF16)


def _merge(x, mods, mod_row, yf, yc, ya, ysf, ysb, z, ssm_d, w_glu, w_branch, w_out, ln_g, ln_b, l):
    t = x.shape[0]
    rf = lambda i: i
    row = lambda w: pl.BlockSpec((ROW_TILE, w), lambda i: (i, 0))
    full = lambda shape: pl.BlockSpec((None,) + shape, lambda i: (l,) + (0,) * len(shape))
    return pl.pallas_call(
        _merge_kernel,
        grid=(t // ROW_TILE,),
        in_specs=[
            row(D_MODEL), pl.BlockSpec((None, 1, 6, D_MODEL), lambda i: (l, mod_row(i), 0, 0)),
            row(512), row(512), row(512), row(512), row(512),
            _zspec(ROW_TILE, 512, rf, ZS_BLK),
            _zspec(ROW_TILE, 1024, rf, 0), _zspec(ROW_TILE, 1024, rf, 1),
            _zspec(ROW_TILE, 1024, rf, 2), _zspec(ROW_TILE, 1024, rf, 3),
            full((1, 512)), full((512, 512)), full((N_BRANCH, 512, D_MODEL)), full((D_MODEL, D_MODEL)),
            full((1, D_MODEL)), full((1, D_MODEL)),
        ],
        out_specs=[row(D_MODEL), row(D_MODEL)],
        out_shape=[jax.ShapeDtypeStruct((t, D_MODEL), F32), jax.ShapeDtypeStruct((t, D_MODEL), BF16)],
        compiler_params=_params(("parallel",)),
    )(x, mods, yf, yc, ya, ysf, ysb, z, z, z, z, z, ssm_d, w_glu, w_branch, w_out, ln_g, ln_b)


def _top16(s):
    n, w = s.shape
    iota = lax.broadcasted_iota(jnp.int32, (n, w), 0).astype(F32)
    kio = lax.broadcasted_iota(jnp.int32, (PEER_TOPK, w), 0)

    def body(k, carry):
        work, rank, vals, _ = carry
        m = jnp.max(work, axis=0, keepdims=True)
        pos = jnp.min(jnp.where(work == m, iota, float(n)), axis=0, keepdims=True)
        hit = iota == pos
        rank = jnp.where(hit, lax.convert_element_type(k, F32), rank)
        work = jnp.where(hit, -jnp.inf, work)
        vals = jnp.where(kio == k, m, vals)
        return work, rank, vals, pos

    init = (s, jnp.full((n, w), 1e9, F32), jnp.zeros((PEER_TOPK, w), F32), jnp.zeros((1, w), F32))
    _, rank, vals, pos = lax.fori_loop(0, PEER_TOPK, body, init)
    return vals, rank, pos


def _max16(s):
    w = s.shape[1]
    kio = lax.broadcasted_iota(jnp.int32, (PEER_TOPK, w), 0)

    def body(k, carry):
        work, vals = carry
        m = jnp.max(work, axis=0, keepdims=True)
        return jnp.where(work == m, -jnp.inf, work), jnp.where(kio == k, m, vals)

    _, vals = lax.fori_loop(0, PEER_TOPK, body, (s, jnp.zeros((PEER_TOPK, w), F32)))
    return vals


_STAIR = [(j, PEER_TOPK // (j + 1)) for j in range(PEER_TOPK)]
_STAIR_ROWS = -(-sum(k for _, k in _STAIR) // 8) * 8


def _stair_candidates(v1, v2):
    w = v1.shape[1]
    rows = [v1[j:j + 1] + v2[0:k] for j, k in _STAIR]
    npad = _STAIR_ROWS - sum(k for _, k in _STAIR)
    return jnp.concatenate(rows + [jnp.full((npad, w), -jnp.inf, F32)], axis=0)


def _stair_positions(w):
    rows = [float(PEER_TOPK * j) + lax.broadcasted_iota(jnp.int32, (k, w), 0).astype(F32) for j, k in _STAIR]
    npad = _STAIR_ROWS - sum(k for _, k in _STAIR)
    return jnp.concatenate(rows + [jnp.full((npad, w), 1e9, F32)], axis=0)


def _next_up(x):
    b = lax.bitcast_convert_type(x, jnp.int32)
    up = jnp.where(x > 0.0, b + 1, jnp.where(x < 0.0, b - 1, jnp.int32(0x00800000)))
    return lax.bitcast_convert_type(up, F32)


def _route_kernel(xm_ref, wq_ref, keys_ref, s1m_ref, qrow_ref, e1_ref, s2m_ref, pb_ref, e2_ref,
                  thr_ref, qs):
    qs[...] = lax.dot_general(wq_ref[...], xm_ref[...], (((1,), (1,)), ((), ())),
                              preferred_element_type=F32)
    w = xm_ref.shape[0]

    def count(mask):
        return jnp.sum(mask.astype(F32), axis=0, keepdims=True)

    def emit(h, s1, s2, in1, in2, m1, m2, vc, qrow, pb, thr_up, bad, thr_low):
        z = jnp.sum(jnp.exp(vc - vc[0:1]), axis=0, keepdims=True)
        s1m_ref[h] = jnp.where(in1, s1, -jnp.inf)
        s2m_ref[h] = jnp.where(in2, s2, -jnp.inf)
        e1_ref[h] = jnp.where(in1, jnp.exp(s1 - m1), 0.0) / z
        e2_ref[h] = jnp.where(in2, jnp.exp(s2 - m2), 0.0)
        qrow_ref[h] = qrow
        pb_ref[h] = pb
        thr = vc[PEER_TOPK - 1:PEER_TOPK]
        thr_ref[h] = jnp.concatenate([thr, thr_up, bad, thr_low, jnp.zeros((4, w), F32)], axis=0)

    def head(h, carry):
        base = pl.multiple_of(h * KEY_DIM, KEY_DIM)
        q1 = qs[pl.ds(base, N_KEYS), :].astype(BF16)
        q2 = qs[pl.ds(base + N_KEYS, N_KEYS), :].astype(BF16)
        s1 = jnp.dot(keys_ref[2 * h], q1, preferred_element_type=F32)
        s2 = jnp.dot(keys_ref[2 * h + 1], q2, preferred_element_type=F32)

        v1 = _max16(s1)
        v2 = _max16(s2)
        in1 = s1 >= v1[PEER_TOPK - 1:PEER_TOPK]
        in2 = s2 >= v2[PEER_TOPK - 1:PEER_TOPK]
        cand = _stair_candidates(v1, v2)
        vc = _max16(cand)
        thr = vc[PEER_TOPK - 1:PEER_TOPK]
        zero = jnp.zeros((N_KEYS, w), F32)
        k = float(PEER_TOPK)
        bad = jnp.abs(count(in1) - k) + jnp.abs(count(in2) - k) + jnp.abs(count(cand >= thr) - k)
        top = lambda v: jnp.maximum(jnp.abs(v[0:1]), jnp.abs(v[PEER_TOPK - 1:PEER_TOPK]))
        delta = (top(v1) + top(v2)) * (2.0 ** -20)
        lo = thr - 2.0 * delta
        flag = bad
        for jrow in range(PEER_TOPK):
            csum = v1[jrow:jrow + 1] + v2
            flag = flag + count((csum < thr) & (csum >= lo))
        emit(h, s1, s2, in1, in2, v1[0:1], v2[0:1], vc, zero, zero, thr, flag, thr - delta)

        @pl.when(jnp.max(bad) > 0.0)
        def _():
            xv1, r1, _ = _top16(s1)
            xv2, r2, _ = _top16(s2)
            xcand = _stair_candidates(xv1, xv2)
            xvc, _, prow = _top16(xcand)
            riota = lax.broadcasted_iota(jnp.int32, xcand.shape, 0).astype(F32)
            pthr = jnp.sum(jnp.where(riota == prow, _stair_positions(w), 0.0), axis=0, keepdims=True)
            emit(h, s1, s2, r1 < 100.0, r2 < 100.0, xv1[0:1], xv2[0:1], xvc,
                 pthr - k * r1, r2, _next_up(xvc[PEER_TOPK - 1:PEER_TOPK]), flag, thr - delta)

        return carry

    lax.fori_loop(0, PEER_HEADS, head, 0)


def _route(xm, wq_t, keys, l):
    t = xm.shape[0]
    big = jax.ShapeDtypeStruct((PEER_HEADS, N_KEYS, t), F32)
    bspec = pl.BlockSpec((PEER_HEADS, N_KEYS, ROUTE_TT), lambda i: (0, 0, i))
    return pl.pallas_call(
        _route_kernel,
        grid=(t // ROUTE_TT,),
        in_specs=[
            pl.BlockSpec((ROUTE_TT, D_MODEL), lambda i: (i, 0)),
            pl.BlockSpec((None, PEER_HEADS * KEY_DIM, D_MODEL), lambda i: (l, 0, 0)),
            pl.BlockSpec((None, 2 * PEER_HEADS, N_KEYS, N_KEYS), lambda i: (l, 0, 0, 0)),
        ],
        out_specs=[bspec] * 6 + [pl.BlockSpec((PEER_HEADS, 8, ROUTE_TT), lambda i: (0, 0, i))],
        out_shape=[big] * 6 + [jax.ShapeDtypeStruct((PEER_HEADS, 8, t), F32)],
        scratch_shapes=[pltpu.VMEM((PEER_HEADS * KEY_DIM, ROUTE_TT), F32)],
        compiler_params=_params(("parallel",)),
    )(xm, wq_t, keys)


def _peer_kernel(flag_ref, xm_ref, u_ref, vt_ref, s1m_ref, qrow_ref, e1_ref, s2m_ref, pb_ref, e2_ref,
                 thr_ref, x1_ref, mod_ref, g_ref, b_ref, o_ref, ht, wacc, pt, acc):
    j = pl.program_id(1)
    nrow = PEER_EB // N_KEYS

    @pl.when(j == 0)
    def _():
        acc[...] = jnp.zeros_like(acc)

    rpass = 4
    nlg = PEER_TT // 128

    def rows_of(tile, r0):
        return jnp.stack([jnp.broadcast_to(tile[r:r + 1, :], (8, 128)) for r in range(r0, r0 + rpass)])

    def gate_work(hp, lg, exact):
        lanes = slice(lg * 128, (lg + 1) * 128)
        heads = (2 * hp, 2 * hp + 1)
        for rp in range(0, nrow, rpass):
            e1r = [rows_of(e1_ref[h, :, lanes], rp) for h in heads]
            if exact:
                s1r = [rows_of(s1m_ref[h, :, lanes], rp) for h in heads]
                qr = [rows_of(qrow_ref[h, :, lanes], rp) for h in heads]
            else:
                need = [thr_ref[h, 3:4, lanes] - rows_of(s1m_ref[h, :, lanes], rp) for h in heads]
            for v in range(N_KEYS // 8):
                sub = slice(v * 8, (v + 1) * 8)
                gate = None
                for i, h in enumerate(heads):
                    if exact:
                        first = pb_ref[h, sub, lanes][None] <= qr[i]
                        limit = jnp.where(first, thr_ref[h, 0:1, lanes], thr_ref[h, 1:2, lanes])
                        sel = s1r[i] + s2m_ref[h, sub, lanes][None] >= limit
                    else:
                        sel = s2m_ref[h, sub, lanes][None] >= need[i]
                    g = jnp.where(sel, e1r[i] * e2_ref[h, sub, lanes][None], 0.0)
                    gate = g if gate is None else gate + g
                if hp == 0:
                    wacc[rp:rp + rpass, sub, lanes] = gate
                else:
                    wacc[rp:rp + rpass, sub, lanes] += gate

    hfull = lax.dot_general(u_ref[...], xm_ref[...], (((1,), (1,)), ((), ())),
                            preferred_element_type=F32)
    for r in range(nrow):
        ht[r] = hfull[r * N_KEYS:(r + 1) * N_KEYS, :]
    for hp in range(PEER_HEADS // 2):
        for lg in range(nlg):
            row = pl.program_id(0) * nlg + lg
            needs_sum = (flag_ref[row, 2 * hp] + flag_ref[row, 2 * hp + 1]) > 0
            pl.when(needs_sum)(functools.partial(gate_work, hp, lg, True))
            pl.when(jnp.logical_not(needs_sum))(functools.partial(gate_work, hp, lg, False))
    for lg in range(PEER_TT // 128):
        lanes = slice(lg * 128, (lg + 1) * 128)
        for r in range(nrow):
            pt[r * N_KEYS:(r + 1) * N_KEYS, lanes] = (
                wacc[r, :, lanes] * _gelu(ht[r, :, lanes])).astype(BF16)
    acc[...] += jnp.dot(vt_ref[...], pt[...], preferred_element_type=F32)

    @pl.when(j == pl.num_programs(1) - 1)
    def _():
        alpha = (2 * 4) ** 0.25
        ff = acc[...].T
        o_ref[...] = _layer_norm(alpha * x1_ref[...] + mod_ref[0, 5:6, :] * ff, g_ref[...], b_ref[...])


def _peer(xm, u_b, vt_b, routing, x1, mods, mod_row_tt, ln_g, ln_b, l):
    t = xm.shape[0]
    once = pl.Buffered(1)
    tok = lambda w: pl.BlockSpec((PEER_TT, w), lambda i, j: (i, 0), pipeline_mode=once)
    rspec = pl.BlockSpec((PEER_HEADS, N_KEYS, PEER_TT), lambda i, j: (0, 0, i), pipeline_mode=once)
    rowspec = pl.BlockSpec((PEER_HEADS, PEER_EB // N_KEYS, PEER_TT), lambda i, j: (0, j, i))
    full = lambda shape: pl.BlockSpec((None,) + shape, lambda i, j: (l,) + (0,) * len(shape))
    nblk = N_EXPERTS // PEER_EB
    bad = routing[6][:, 2, :].reshape(PEER_HEADS, t // 128, 128)
    flags = (jnp.max(bad, axis=2) > 0.0).astype(jnp.int32).T
    return pl.pallas_call(
        _peer_kernel,
        grid=(t // PEER_TT, nblk),
        in_specs=[
            pl.BlockSpec(memory_space=pltpu.SMEM),
            tok(D_MODEL),
            pl.BlockSpec((None, PEER_EB, D_MODEL), lambda i, j: (l, j, 0)),
            pl.BlockSpec((None, D_MODEL, PEER_EB), lambda i, j: (l, 0, j)),
            rowspec, rowspec, rowspec, rspec, rspec, rspec,
            pl.BlockSpec((PEER_HEADS, 8, PEER_TT), lambda i, j: (0, 0, i)),
            tok(D_MODEL),
            pl.BlockSpec((None, 1, 6, D_MODEL), lambda i, j: (l, mod_row_tt(i), 0, 0)),
            full((1, D_MODEL)), full((1, D_MODEL)),
        ],
        out_specs=pl.BlockSpec((PEER_TT, D_MODEL), lambda i, j: (i, 0)),
        out_shape=jax.ShapeDtypeStruct((t, D_MODEL), F32),
        scratch_shapes=[pltpu.VMEM((PEER_EB // N_KEYS, N_KEYS, PEER_TT), F32),
                        pltpu.VMEM((PEER_EB // N_KEYS, N_KEYS, PEER_TT), F32),
                        pltpu.VMEM((PEER_EB, PEER_TT), BF16),
                        pltpu.VMEM((D_MODEL, PEER_TT), F32)],
        compiler_params=_params(("parallel", "arbitrary")),
    )(flags, xm, u_b, vt_b, *routing, x1, mods, ln_g, ln_b)


def _dft_tables(length):
    n = np.arange(length)
    ang = 2.0 * np.pi * ((n[:, None] * n[None, :]) % length) / length
    dl = np.concatenate([np.cos(ang), -np.sin(ang)], axis=1) / math.sqrt(length)
    c = np.arange(FFT_GROUP_CH)
    angc = 2.0 * np.pi * ((c[:, None] * c[None, :]) % FFT_GROUP_CH) / FFT_GROUP_CH
    eye = np.eye(FFT_GROUPS)
    dc = np.concatenate([np.kron(eye, np.cos(angc)), np.kron(eye, np.sin(angc))], axis=1)
    dc = dc / math.sqrt(FFT_GROUP_CH)
    return jnp.asarray(dl, BF16), jnp.asarray(dc, BF16)


def _rope_tables(length, nheads):
    t = np.arange(length)
    pos = np.stack([t // GRID_W, t % GRID_W], axis=1).astype(np.float32)
    n_freq = HEAD_DIM // 4
    inv = (1.0 / (ROPE_BASE ** (np.arange(n_freq, dtype=np.float32) / n_freq))).astype(np.float32)
    ang = pos[:, :, None] * inv[None, None, :]
    cos = np.repeat(np.cos(ang)[:, :, None, :], 2, axis=2).reshape(length, HEAD_DIM)
    sin = np.sin(ang)
    sin = np.stack([-sin, sin], axis=2).reshape(length, HEAD_DIM)
    return (jnp.asarray(np.tile(cos, (1, nheads)), F32), jnp.asarray(np.tile(sin, (1, nheads)), F32))


def kernel(x_prompt, x_sample, cache_k, cache_v, state_ssm_re, state_ssm_im, c, c_ctx, w_ada, b_ada, w_in, conv_w, ssm_lam_re, ssm_lam_im, ssm_log_step, ssm_b_re, ssm_b_im, ssm_c_re, ssm_c_im, ssm_d, ssm_w_glu, attn_sink, w_branch, w_out, ln1_g, ln1_b, ln2_g, ln2_b, peer_wq, peer_subkeys, peer_u, peer_v):
    nb, seq, _ = x_prompt.shape
    nd, lseq, _ = x_sample.shape
    depth = w_in.shape[0]
    t_ctx = nb * seq
    t_all = t_ctx + nd * lseq
    assert t_ctx % lseq == 0 and t_all % PEER_TT == 0 and (2 * seq) % PEER_TT == 0
    assert lseq % SSM_TCHUNK == 0 and seq == SSM_TCHUNK

    x = jnp.concatenate([x_prompt.reshape(t_ctx, D_MODEL), x_sample.reshape(nd * lseq, D_MODEL)], axis=0)

    nrow = -(-(1 + nd) // 8) * 8
    cvecs = jnp.concatenate([c_ctx[None, :], c, jnp.zeros((nrow - 1 - nd, D_MODEL), F32)], axis=0)
    mods_all = _modulation(cvecs, w_ada, b_ada).reshape(depth, nrow, 6, D_MODEL)

    def mod_row_for(tile):
        nctx = t_ctx // tile
        per = lseq // tile
        return lambda i: jnp.where(i < nctx, 0, 1 + (i - nctx) // per)

    mod_row = mod_row_for(ROW_TILE)
    mod_row_tt = mod_row_for(PEER_TT)

    dl_ctx, dft_c = _dft_tables(seq)
    dl_lat, _ = _dft_tables(lseq)
    cosq, sinq = _rope_tables(lseq, N_HEADS)
    cosk, sin_k = _rope_tables(lseq, N_KV)

    gate0 = sum((512,) * 6) + 2 * N_KV * HEAD_DIM
    w_in_b = jnp.concatenate([w_in[:, :, gate0:], w_in[:, :, :gate0]], axis=2).astype(BF16)
    w_glu_b = ssm_w_glu.astype(BF16)
    w_branch_b = w_branch.astype(BF16)
    w_out_b = w_out.astype(BF16)
    wq_t = peer_wq.transpose(0, 2, 1).astype(BF16)
    keys = peer_subkeys.reshape(depth, 2 * PEER_HEADS, N_KEYS, KEY_DIM // 2).astype(BF16)
    u_b = peer_u.astype(BF16)
    vt_b = peer_v.transpose(0, 2, 1).astype(BF16)
    sp = _s5_params(ssm_lam_re, ssm_lam_im, ssm_log_step, ssm_b_re, ssm_b_im, ssm_c_re, ssm_c_im)
    sink = attn_sink.reshape(depth, 1, N_HEADS)
    ssm_d3 = ssm_d.reshape(depth, 1, -1)
    ln1_g3, ln1_b3 = ln1_g.reshape(depth, 1, -1), ln1_b.reshape(depth, 1, -1)
    ln2_g3, ln2_b3 = ln2_g.reshape(depth, 1, -1), ln2_b.reshape(depth, 1, -1)
    ck = cache_k.reshape(nd, depth, -1, N_KV * HEAD_DIM)
    cv = cache_v.reshape(nd, depth, -1, N_KV * HEAD_DIM)
    nbp_c = -(-nb // SSM_BROWS) * SSM_BROWS
    nbp_l = -(-nd // SSM_BROWS) * SSM_BROWS
    zero_state = jnp.zeros((SSM_GROUPS // SSM_GBLK, nbp_c, 2 * SSM_GBLK * SSM_STATE), F32)

    new_k, new_v, new_re, new_im = [], [], [], []
    for l in range(depth):
        z = _in_proj(x, mods_all, w_in_b, mod_row_for(IN_TILE), l)

        yf, yc, ya = _mixer_ctx(z, nb, seq, conv_w, sink, dl_ctx, dft_c, l)
        yf, yc = _fftconv_lat(z, t_ctx, nd, lseq, conv_w, dl_lat, dft_c, yf, yc, l)
        ya = _attn_lat(z, t_ctx, nd, lseq, ck, cv, cosq, sinq, cosk, sin_k, sink, ya, l)

        zs = z[:, ZS_BLK * 512:(ZS_BLK + 1) * 512]
        u_c, _ = _to_time_major(zs[:t_ctx], nb, seq)
        ysf_c, ysb_c, hf_c, hb_c = _s5(u_c, zero_state, zero_state, sp, nbp_c // SSM_BROWS,
                                       seq // SSM_TCHUNK, l)
        u_l, _ = _to_time_major(zs[t_ctx:], nd, lseq)
        h0f = _state_to_blocks(state_ssm_re[:, l, 0], state_ssm_im[:, l, 0], nbp_l)
        h0b = _state_to_blocks(state_ssm_re[:, l, 1], state_ssm_im[:, l, 1], nbp_l)
        ysf_l, ysb_l, _, _ = _s5(u_l, h0f, h0b, sp, nbp_l // SSM_BROWS, lseq // SSM_TCHUNK, l)
        ysf = jnp.concatenate([_from_time_major(ysf_c, nb, nbp_c, seq), _from_time_major(ysf_l, nd, nbp_l, lseq)], axis=0)
        ysb = jnp.concatenate([_from_time_major(ysb_c, nb, nbp_c, seq), _from_time_major(ysb_l, nd, nbp_l, lseq)], axis=0)

        x1, xm2 = _merge(x, mods_all, mod_row, yf, yc, ya, ysf, ysb, z,
                         ssm_d3, w_glu_b, w_branch_b, w_out_b, ln1_g3, ln1_b3, l)

        routing = _route(xm2, wq_t, keys, l)
        x = _peer(xm2, u_b, vt_b, routing, x1, mods_all, mod_row_tt, ln2_g3, ln2_b3, l)

        kv = z[:t_ctx, ZK_BLK * 128:(ZV_BLK + 1) * 128].reshape(nb, seq, 2, N_KV, HEAD_DIM)
        new_k.append(kv[:, :, 0])
        new_v.append(kv[:, :, 1])
        fre, fim = _blocks_to_state(hf_c, nb)
        bre, bim = _blocks_to_state(hb_c, nb)
        new_re.append(jnp.stack([fre, bre], axis=1))
        new_im.append(jnp.stack([fim, bim], axis=1))

    return (x[:t_ctx].reshape(nb, seq, D_MODEL), x[t_ctx:].reshape(nd, lseq, D_MODEL),
            jnp.stack(new_k, axis=1), jnp.stack(new_v, axis=1),
            jnp.stack(new_re, axis=1), jnp.stack(new_im, axis=1))
```

```python
import functools
import math

import numpy as np
import jax
import jax.numpy as jnp
from jax import lax
from jax.experimental import pallas as pl
from jax.experimental.pallas import tpu as pltpu

F32 = jnp.float32
BF16 = jnp.bfloat16

D_MODEL = 1024
GRID_W = 64
N_BRANCH = 4
BRANCH_WIDTH = 512
FFT_GROUPS = 4
FFT_GROUP_CH = 128
CONV_K = 3
SSM_GROUPS = 32
SSM_CH = 16
SSM_STATE = 64
N_HEADS = 8
N_KV = 2
Q_PER_KV = N_HEADS // N_KV
HEAD_DIM = 64
WINDOW = 128
ATT_BLOCK = 128
ROPE_BASE = 10000.0
PEER_HEADS = 8
N_KEYS = 128
N_EXPERTS = N_KEYS * N_KEYS
PEER_TOPK = 16
KEY_DIM = 256
LN_EPS = 1e-5
NEG_INF = -1e30

Z_COLS = N_BRANCH * D_MODEL + 6 * BRANCH_WIDTH + 2 * N_KV * HEAD_DIM
ZG_BLK = 0
ZF_BLK, ZB_BLK, ZC_BLK, ZH_BLK, ZS_BLK, ZQ_BLK = 8, 9, 10, 11, 12, 13
ZK_BLK, ZV_BLK = 56, 57

V7X_VMEM_LIMIT_BYTES = 56 * 1024 * 1024
SSM_GBLK = 8
SSM_TCHUNK = 256
SSM_BROWS = 8
ROW_TILE = 256
IN_TILE = 512
PEER_TT = 512
PEER_EB = 2048
ROUTE_TT = 256


def _params(sem):
    return pltpu.CompilerParams(dimension_semantics=sem, vmem_limit_bytes=V7X_VMEM_LIMIT_BYTES)


def _gelu(x):
    return 0.5 * x * (1.0 + jnp.tanh(0.7978845608028654 * (x + 0.044715 * (x * x * x))))


def _layer_norm(h, g, b):
    mu = jnp.mean(h, axis=-1, keepdims=True)
    hc = h - mu
    var = jnp.mean(hc * hc, axis=-1, keepdims=True)
    return hc * lax.rsqrt(var + LN_EPS) * g + b


def _mod_kernel(c_ref, w_ref, b_ref, o_ref):
    cv = c_ref[...]
    s = (cv * jax.nn.sigmoid(cv)).astype(BF16)
    o_ref[0] = jnp.dot(s, w_ref[0].astype(BF16), preferred_element_type=F32) + b_ref[0]


def _modulation(cvecs, w_ada, b_ada):
    depth = w_ada.shape[0]
    nrow = cvecs.shape[0]
    return pl.pallas_call(
        _mod_kernel,
        grid=(depth, 6),
        in_specs=[
            pl.BlockSpec((nrow, D_MODEL), lambda l, j: (0, 0)),
            pl.BlockSpec((1, D_MODEL, D_MODEL), lambda l, j: (l, 0, j)),
            pl.BlockSpec((1, 1, D_MODEL), lambda l, j: (l, 0, j)),
        ],
        out_specs=pl.BlockSpec((1, nrow, D_MODEL), lambda l, j: (l, 0, j)),
        out_shape=jax.ShapeDtypeStruct((depth, nrow, 6 * D_MODEL), F32),
        compiler_params=_params(("parallel", "parallel")),
    )(cvecs, w_ada, b_ada.reshape(depth, 1, 6 * D_MODEL))


def _win_kernel(x_ref, mod_ref, w_ref, z_ref):
    sh = mod_ref[0, 0:1, :]
    sc = mod_ref[0, 1:2, :]
    xm = (x_ref[...] * (1.0 + sc) + sh).astype(BF16)
    z_ref[...] = jnp.dot(xm, w_ref[...], preferred_element_type=F32)


def _in_proj(x, mods, w_in, mod_row, l):
    t = x.shape[0]
    ncol = Z_COLS // 2
    return pl.pallas_call(
        _win_kernel,
        grid=(2, t // IN_TILE),
        in_specs=[
            pl.BlockSpec((IN_TILE, D_MODEL), lambda c, i: (i, 0)),
            pl.BlockSpec((None, 1, 6, D_MODEL), lambda c, i: (l, mod_row(i), 0, 0)),
            pl.BlockSpec((None, D_MODEL, ncol), lambda c, i: (l, 0, c)),
        ],
        out_specs=pl.BlockSpec((IN_TILE, ncol), lambda c, i: (i, c)),
        out_shape=jax.ShapeDtypeStruct((t, Z_COLS), F32),
        compiler_params=_params(("parallel", "parallel")),
    )(x, mods, w_in)


def _fft_conv(zf_ref, zb_ref, zc_ref, zh_ref, cw_ref, dl_ref, dc_ref, yf_ref, yc_ref):
    length = zf_ref.shape[0]
    zf = zf_ref[...].astype(BF16)
    ab = jnp.dot(zf, dc_ref[...], preferred_element_type=F32)
    ab = jnp.concatenate([ab[:, :BRANCH_WIDTH], ab[:, BRANCH_WIDTH:]], axis=0).astype(BF16)
    yf_ref[...] = jnp.dot(dl_ref[...], ab, preferred_element_type=F32).astype(BF16)
    g = zc_ref[...] * zh_ref[...]
    row = lax.broadcasted_iota(jnp.int32, g.shape, 0)
    prev = jnp.where(row == 0, 0.0, pltpu.roll(g, 1, 0))
    nxt = jnp.where(row == length - 1, 0.0, pltpu.roll(g, length - 1, 0))
    conv = cw_ref[0:1, :] * prev + cw_ref[1:2, :] * g + cw_ref[2:3, :] * nxt
    yc_ref[...] = (zb_ref[...] * conv).astype(BF16)


def _softmax_pv(s, sink, v):
    m = jnp.maximum(jnp.max(s, axis=1, keepdims=True), sink)
    p = jnp.exp(s - m)
    den = jnp.sum(p, axis=1, keepdims=True) + jnp.exp(sink - m)
    return jnp.dot(p.astype(BF16), v, preferred_element_type=F32) / den


def _mixer_ctx_kernel(zf_ref, zb_ref, zc_ref, zh_ref, zq_ref, zk_ref, zv_ref, cw_ref, sink_ref,
                      dl_ref, dc_ref, yf_in, yc_in, ya_in, yf_ref, yc_ref, ya_ref):
    del yf_in, yc_in, ya_in
    _fft_conv(zf_ref, zb_ref, zc_ref, zh_ref, cw_ref, dl_ref, dc_ref, yf_ref, yc_ref)
    q = zq_ref[...] * (HEAD_DIM ** -0.5)
    k = zk_ref[...]
    v = zv_ref[...]
    outs = []
    for h in range(N_HEADS):
        g = h // Q_PER_KV
        qh = q[:, h * HEAD_DIM:(h + 1) * HEAD_DIM].astype(BF16)
        kg = k[:, g * HEAD_DIM:(g + 1) * HEAD_DIM].astype(BF16)
        vg = v[:, g * HEAD_DIM:(g + 1) * HEAD_DIM].astype(BF16)
        s = lax.dot_general(qh, kg, (((1,), (1,)), ((), ())), preferred_element_type=F32)
        outs.append(_softmax_pv(s, sink_ref[0:1, h:h + 1], vg))
    ya_ref[...] = jnp.concatenate(outs, axis=1).astype(BF16)


def _zspec(rows, width, row_fn, col_blk):
    return pl.BlockSpec((rows, width), lambda *a: (row_fn(*a), col_blk))


def _mixer_ctx(z, nb, seq, conv_w, sink, dft_l, dft_c, l):
    rf = lambda b: b
    full = lambda shape: pl.BlockSpec(shape, lambda b: (0,) * len(shape))
    layer = lambda shape: pl.BlockSpec((None,) + shape, lambda b: (l,) + (0,) * len(shape))
    out = jax.ShapeDtypeStruct((z.shape[0], BRANCH_WIDTH), BF16)
    ospec = pl.BlockSpec((seq, BRANCH_WIDTH), lambda b: (b, 0))
    anyspec = pl.BlockSpec(memory_space=pl.ANY)
    zeros = [jnp.zeros(out.shape, BF16) for _ in range(3)]
    return pl.pallas_call(
        _mixer_ctx_kernel,
        grid=(nb,),
        in_specs=[
            _zspec(seq, 512, rf, ZF_BLK), _zspec(seq, 512, rf, ZB_BLK), _zspec(seq, 512, rf, ZC_BLK),
            _zspec(seq, 512, rf, ZH_BLK), _zspec(seq, 512, rf, ZQ_BLK),
            _zspec(seq, 128, rf, ZK_BLK), _zspec(seq, 128, rf, ZV_BLK),
            layer((CONV_K, BRANCH_WIDTH)), layer((1, N_HEADS)),
            full((seq, 2 * seq)), full((BRANCH_WIDTH, 2 * BRANCH_WIDTH)),
            anyspec, anyspec, anyspec,
        ],
        out_specs=[ospec, ospec, ospec],
        out_shape=[out, out, out],
        input_output_aliases={11: 0, 12: 1, 13: 2},
        compiler_params=_params(("parallel",)),
    )(z, z, z, z, z, z, z, conv_w, sink, dft_l, dft_c, *zeros)


def _fftconv_lat_kernel(zf_ref, zb_ref, zc_ref, zh_ref, cw_ref, dl_ref, dc_ref, yf_in, yc_in,
                        yf_ref, yc_ref):
    del yf_in, yc_in
    _fft_conv(zf_ref, zb_ref, zc_ref, zh_ref, cw_ref, dl_ref, dc_ref, yf_ref, yc_ref)


def _fftconv_lat(z, row0, nb, seq, conv_w, dft_l, dft_c, yf, yc, l):
    rf = lambda b: row0 // seq + b
    full = lambda shape: pl.BlockSpec(shape, lambda b: (0,) * len(shape))
    out = jax.ShapeDtypeStruct(yf.shape, BF16)
    ospec = pl.BlockSpec((seq, BRANCH_WIDTH), lambda b: (row0 // seq + b, 0))
    anyspec = pl.BlockSpec(memory_space=pl.ANY)
    return pl.pallas_call(
        _fftconv_lat_kernel,
        grid=(nb,),
        in_specs=[
            _zspec(seq, 512, rf, ZF_BLK), _zspec(seq, 512, rf, ZB_BLK), _zspec(seq, 512, rf, ZC_BLK),
            _zspec(seq, 512, rf, ZH_BLK),
            pl.BlockSpec((None, CONV_K, BRANCH_WIDTH), lambda b: (l, 0, 0)),
            full((seq, 2 * seq)), full((BRANCH_WIDTH, 2 * BRANCH_WIDTH)),
            anyspec, anyspec,
        ],
        out_specs=[ospec, ospec],
        out_shape=[out, out],
        input_output_aliases={7: 0, 8: 1},
        compiler_params=_params(("parallel",)),
    )(z, z, z, z, conv_w, dft_l, dft_c, yf, yc)


def _rope(x, cos, sin):
    lane = lax.broadcasted_iota(jnp.int32, (x.shape[0], 128), 1)
    first = (lane & 31) < 16
    parts = []
    for c in range(x.shape[1] // 128):
        xc = x[:, c * 128:(c + 1) * 128]
        swapped = jnp.where(first, pltpu.roll(xc, 112, 1), pltpu.roll(xc, 16, 1))
        parts.append(xc * cos[:, c * 128:(c + 1) * 128] + swapped * sin[:, c * 128:(c + 1) * 128])
    return parts[0] if len(parts) == 1 else jnp.concatenate(parts, axis=1)


def _attn_lat_kernel(zq_ref, zk_ref, zv_ref, ck_ref, cv_ref, cosq_ref, sinq_ref, cosk_ref, sinkk_ref,
                     sink_ref, ya_in, ya_ref):
    del ya_in
    n = pl.program_id(1)
    nblk = pl.num_programs(1)
    nwin = 3 * ATT_BLOCK
    q = _rope(zq_ref[...], cosq_ref[...], sinq_ref[...]) * (HEAD_DIM ** -0.5)
    ws = pl.multiple_of(jnp.clip(n - 1, 0, nblk - 3) * ATT_BLOCK, ATT_BLOCK)
    kw = _rope(zk_ref[pl.ds(ws, nwin), :], cosk_ref[pl.ds(ws, nwin), :], sinkk_ref[pl.ds(ws, nwin), :])
    vw = zv_ref[pl.ds(ws, nwin), :]
    k_all = jnp.concatenate([kw, ck_ref[...]], axis=0)
    v_all = jnp.concatenate([vw, cv_ref[...]], axis=0)
    nkey = k_all.shape[0]
    qpos = n * ATT_BLOCK + lax.broadcasted_iota(jnp.int32, (ATT_BLOCK, nkey), 0)
    col = lax.broadcasted_iota(jnp.int32, (ATT_BLOCK, nkey), 1)
    valid = (jnp.abs(qpos - (ws + col)) <= WINDOW) | (col >= nwin)
    outs = []
    for h in range(N_HEADS):
        g = h // Q_PER_KV
        qh = q[:, h * HEAD_DIM:(h + 1) * HEAD_DIM].astype(BF16)
        kg = k_all[:, g * HEAD_DIM:(g + 1) * HEAD_DIM].astype(BF16)
        vg = v_all[:, g * HEAD_DIM:(g + 1) * HEAD_DIM].astype(BF16)
        s = lax.dot_general(qh, kg, (((1,), (1,)), ((), ())), preferred_element_type=F32)
        s = jnp.where(valid, s, NEG_INF)
        outs.append(_softmax_pv(s, sink_ref[0:1, h:h + 1], vg))
    ya_ref[...] = jnp.concatenate(outs, axis=1).astype(BF16)


def _attn_lat(z, row0, nb, seq, ck, cv, cosq, sinq, cosk, sin_k, sink, ya, l):
    nblk = seq // ATT_BLOCK
    kvw = N_KV * HEAD_DIM
    past = ck.shape[2]
    full = lambda shape: pl.BlockSpec(shape, lambda b, n: (0,) * len(shape))
    cache = pl.BlockSpec((None, None, past, kvw), lambda b, n: (b, l, 0, 0))
    return pl.pallas_call(
        _attn_lat_kernel,
        grid=(nb, nblk),
        in_specs=[
            pl.BlockSpec((ATT_BLOCK, 512), lambda b, n: (row0 // ATT_BLOCK + b * nblk + n, ZQ_BLK)),
            pl.BlockSpec((seq, kvw), lambda b, n: (row0 // seq + b, ZK_BLK)),
            pl.BlockSpec((seq, kvw), lambda b, n: (row0 // seq + b, ZV_BLK)),
            cache, cache,
            pl.BlockSpec((ATT_BLOCK, 512), lambda b, n: (n, 0)),
            pl.BlockSpec((ATT_BLOCK, 512), lambda b, n: (n, 0)),
            full((seq, kvw)), full((seq, kvw)),
            pl.BlockSpec((None, 1, N_HEADS), lambda b, n: (l, 0, 0)),
            pl.BlockSpec(memory_space=pl.ANY),
        ],
        out_specs=pl.BlockSpec((ATT_BLOCK, 512), lambda b, n: (row0 // ATT_BLOCK + b * nblk + n, 0)),
        out_shape=jax.ShapeDtypeStruct(ya.shape, BF16),
        input_output_aliases={10: 0},
        compiler_params=_params(("parallel", "parallel")),
    )(z, z, z, ck, cv, cosq, sinq, cosk, sin_k, sink, ya)


def _s5_kernel(uf_ref, ub_ref, h0f_ref, h0b_ref, wbf_ref, wbb_ref, cf_ref, cb_ref, af_ref, ab_ref,
               yf_ref, yb_ref, hf_ref, hb_ref, buff, bufb, hst):
    c = pl.program_id(2)
    half = SSM_GBLK * SSM_STATE
    steps = uf_ref.shape[0] // SSM_BROWS

    @pl.when(c == 0)
    def _():
        hst[0] = h0f_ref[0]
        hst[1] = h0b_ref[0]

    buff[...] = jnp.dot(uf_ref[...].astype(BF16), wbf_ref[...], preferred_element_type=F32)
    bufb[...] = jnp.dot(ub_ref[...].astype(BF16), wbb_ref[...], preferred_element_type=F32)
    afr = jnp.broadcast_to(af_ref[0:1, :], (SSM_BROWS, half))
    afi = jnp.broadcast_to(af_ref[1:2, :], (SSM_BROWS, half))
    abr = jnp.broadcast_to(ab_ref[0:1, :], (SSM_BROWS, half))
    abi = jnp.broadcast_to(ab_ref[1:2, :], (SSM_BROWS, half))

    def step(t, carry):
        hfr, hfi, hbr, hbi = carry
        rf = pl.multiple_of(t * SSM_BROWS, SSM_BROWS)
        nfr = afr * hfr - afi * hfi + buff[pl.ds(rf, SSM_BROWS), 0:half]
        nfi = afr * hfi + afi * hfr + buff[pl.ds(rf, SSM_BROWS), half:2 * half]
        buff[pl.ds(rf, SSM_BROWS), 0:half] = nfr
        buff[pl.ds(rf, SSM_BROWS), half:2 * half] = nfi
        rb = pl.multiple_of((steps - 1 - t) * SSM_BROWS, SSM_BROWS)
        nbr = abr * hbr - abi * hbi + bufb[pl.ds(rb, SSM_BROWS), 0:half]
        nbi = abr * hbi + abi * hbr + bufb[pl.ds(rb, SSM_BROWS), half:2 * half]
        bufb[pl.ds(rb, SSM_BROWS), 0:half] = nbr
        bufb[pl.ds(rb, SSM_BROWS), half:2 * half] = nbi
        return nfr, nfi, nbr, nbi

    init = (hst[0, :, 0:half], hst[0, :, half:2 * half], hst[1, :, 0:half], hst[1, :, half:2 * half])
    hfr, hfi, hbr, hbi = lax.fori_loop(0, steps, step, init, unroll=4)
    hst[0, :, 0:half] = hfr
    hst[0, :, half:2 * half] = hfi
    hst[1, :, 0:half] = hbr
    hst[1, :, half:2 * half] = hbi
    yf_ref[...] = jnp.dot(buff[...].astype(BF16), cf_ref[...], preferred_element_type=F32)
    yb_ref[...] = jnp.dot(bufb[...].astype(BF16), cb_ref[...], preferred_element_type=F32)

    @pl.when(c == pl.num_programs(2) - 1)
    def _():
        hf_ref[0] = hst[0]
        hb_ref[0] = hst[1]


def _s5(u_tm, h0f, h0b, sp, nbb, nchunk, l):
    rows = SSM_TCHUNK * SSM_BROWS
    ngb = SSM_GROUPS // SSM_GBLK
    width = 2 * SSM_GBLK * SSM_STATE
    nbrow = nbb * SSM_BROWS
    cw = SSM_GBLK * SSM_CH
    fwd = lambda bb, j, c: (bb * nchunk + c, j)
    bwd = lambda bb, j, c: (bb * nchunk + nchunk - 1 - c, j)
    par = lambda shape, d: pl.BlockSpec((None, None, None) + shape, lambda bb, j, c: (l, d, j, 0, 0))
    st = pl.BlockSpec((1, SSM_BROWS, width), lambda bb, j, c: (j, bb, 0))
    wb, cm, a = sp
    ysh = jax.ShapeDtypeStruct(u_tm.shape, F32)
    hsh = jax.ShapeDtypeStruct((ngb, nbrow, width), F32)
    return pl.pallas_call(
        _s5_kernel,
        grid=(nbb, ngb, nchunk),
        in_specs=[
            pl.BlockSpec((rows, cw), fwd), pl.BlockSpec((rows, cw), bwd), st, st,
            par((cw, width), 0), par((cw, width), 1), par((width, cw), 0), par((width, cw), 1),
            par((2, width // 2), 0), par((2, width // 2), 1),
        ],
        out_specs=[pl.BlockSpec((rows, cw), fwd), pl.BlockSpec((rows, cw), bwd), st, st],
        out_shape=[ysh, ysh, hsh, hsh],
        scratch_shapes=[pltpu.VMEM((rows, width), F32), pltpu.VMEM((rows, width), F32),
                        pltpu.VMEM((2, SSM_BROWS, width), F32)],
        compiler_params=_params(("parallel", "parallel", "arbitrary")),
    )(u_tm, u_tm, h0f, h0b, wb, wb, cm, cm, a, a)


def _s5_params(lam_re, lam_im, log_step, b_re, b_im, c_re, c_im):
    lead = lam_re.shape[:-2]
    dt = jnp.exp(log_step)[..., None]
    mag = jnp.exp(lam_re * dt)
    ar = mag * jnp.cos(lam_im * dt)
    ai = mag * jnp.sin(lam_im * dt)
    den = lam_re * lam_re + lam_im * lam_im
    kr = ((ar - 1.0) * lam_re + ai * lam_im) / den
    ki = (ai * lam_re - (ar - 1.0) * lam_im) / den
    bbr = kr[..., None] * b_re - ki[..., None] * b_im
    bbi = kr[..., None] * b_im + ki[..., None] * b_re
    ngb = SSM_GROUPS // SSM_GBLK
    eye = jnp.eye(SSM_GBLK, dtype=F32)

    def blockdiag_in(m):
        m = m.reshape(lead + (ngb, SSM_GBLK, SSM_STATE, SSM_CH))
        m = jnp.einsum("...jgph,gk->...jghkp", m, eye)
        return m.reshape(lead + (ngb, SSM_GBLK * SSM_CH, SSM_GBLK * SSM_STATE))

    def blockdiag_out(m):
        m = m.reshape(lead + (ngb, SSM_GBLK, SSM_CH, SSM_STATE))
        m = jnp.einsum("...jghp,gk->...jgpkh", m, eye)
        return m.reshape(lead + (ngb, SSM_GBLK * SSM_STATE, SSM_GBLK * SSM_CH))

    wb = jnp.concatenate([blockdiag_in(bbr), blockdiag_in(bbi)], axis=-1).astype(BF16)
    cm = jnp.concatenate([blockdiag_out(c_re), -blockdiag_out(c_im)], axis=-2).astype(BF16)
    a = jnp.stack([ar.reshape(lead + (ngb, -1)), ai.reshape(lead + (ngb, -1))], axis=-2)
    return wb, cm, a


def _to_time_major(u, nb, seq):
    cdim = u.shape[1]
    nbp = -(-nb // SSM_BROWS) * SSM_BROWS
    u = u.reshape(nb, seq, cdim)
    if nbp != nb:
        u = jnp.pad(u, ((0, nbp - nb), (0, 0), (0, 0)))
    u = u.reshape(nbp // SSM_BROWS, SSM_BROWS, seq, cdim).transpose(0, 2, 1, 3)
    return u.reshape(nbp * seq, cdim), nbp


def _from_time_major(y, nb, nbp, seq):
    cdim = y.shape[1]
    y = y.reshape(nbp // SSM_BROWS, seq, SSM_BROWS, cdim).transpose(0, 2, 1, 3)
    return y.reshape(nbp, seq, cdim)[:nb].reshape(nb * seq, cdim)


def _state_to_blocks(re, im, nbp):
    nb = re.shape[0]
    ngb = SSM_GROUPS // SSM_GBLK
    def blk(x):
        return x.reshape(nb, ngb, SSM_GBLK * SSM_STATE).transpose(1, 0, 2)
    h = jnp.concatenate([blk(re), blk(im)], axis=2)
    if nbp != nb:
        h = jnp.pad(h, ((0, 0), (0, nbp - nb), (0, 0)))
    return h


def _blocks_to_state(h, nb):
    half = SSM_GBLK * SSM_STATE
    def unblk(x):
        return x[:, :nb].transpose(1, 0, 2).reshape(nb, SSM_GROUPS, SSM_STATE)
    return unblk(h[:, :, :half]), unblk(h[:, :, half:])


def _merge_kernel(x_ref, mod_ref, yf_ref, yc_ref, ya_ref, ysf_ref, ysb_ref, zs_ref,
                  zg0_ref, zg1_ref, zg2_ref, zg3_ref, d_ref, wglu_ref, wb_ref, wout_ref,
                  g_ref, b_ref, x1_ref, xm_ref):
    ys = ysf_ref[...] + ysb_ref[...] + d_ref[...] * zs_ref[...]
    ys = _gelu(ys)
    yssm = ys * jax.nn.sigmoid(jnp.dot(ys.astype(BF16), wglu_ref[...], preferred_element_type=F32))
    acc = jax.nn.sigmoid(zg0_ref[...]) * jnp.dot(yf_ref[...], wb_ref[0], preferred_element_type=F32)
    acc += jax.nn.sigmoid(zg1_ref[...]) * jnp.dot(yc_ref[...], wb_ref[1], preferred_element_type=F32)
    acc += jax.nn.sigmoid(zg2_ref[...]) * jnp.dot(yssm.astype(BF16), wb_ref[2], preferred_element_type=F32)
    acc += jax.nn.sigmoid(zg3_ref[...]) * jnp.dot(ya_ref[...], wb_ref[3], preferred_element_type=F32)
    mix = jnp.dot(acc.astype(BF16), wout_ref[...], preferred_element_type=F32)
    alpha = (2 * 4) ** 0.25
    x1 = _layer_norm(alpha * x_ref[...] + mod_ref[0, 2:3, :] * mix, g_ref[...], b_ref[...])
    x1_ref[...] = x1
    xm_ref[...] = (x1 * (1.0 + mod_ref[0, 4:5, :]) + mod_ref[0, 3:4, :]).astype(BF16)


def _merge(x, mods, mod_row, yf, yc, ya, ysf, ysb, z, ssm_d, w_glu, w_branch, w_out, ln_g, ln_b, l):
    t = x.shape[0]
    rf = lambda i: i
    row = lambda w: pl.BlockSpec((ROW_TILE, w), lambda i: (i, 0))
    full = lambda shape: pl.BlockSpec((None,) + shape, lambda i: (l,) + (0,) * len(shape))
    return pl.pallas_call(
        _merge_kernel,
        grid=(t // ROW_TILE,),
        in_specs=[
            row(D_MODEL), pl.BlockSpec((None, 1, 6, D_MODEL), lambda i: (l, mod_row(i), 0, 0)),
            row(512), row(512), row(512), row(512), row(512),
            _zspec(ROW_TILE, 512, rf, ZS_BLK),
            _zspec(ROW_TILE, 1024, rf, 0), _zspec(ROW_TILE, 1024, rf, 1),
            _zspec(ROW_TILE, 1024, rf, 2), _zspec(ROW_TILE, 1024, rf, 3),
            full((1, 512)), full((512, 512)), full((N_BRANCH, 512, D_MODEL)), full((D_MODEL, D_MODEL)),
            full((1, D_MODEL)), full((1, D_MODEL)),
        ],
        out_specs=[row(D_MODEL), row(D_MODEL)],
        out_shape=[jax.ShapeDtypeStruct((t, D_MODEL), F32), jax.ShapeDtypeStruct((t, D_MODEL), BF16)],
        compiler_params=_params(("parallel",)),
    )(x, mods, yf, yc, ya, ysf, ysb, z, z, z, z, z, ssm_d, w_glu, w_branch, w_out, ln_g, ln_b)


def _top16(s):
    n, w = s.shape
    iota = lax.broadcasted_iota(jnp.int32, (n, w), 0).astype(F32)
    kio = lax.broadcasted_iota(jnp.int32, (PEER_TOPK, w), 0)

    def body(k, carry):
        work, rank, vals, _ = carry
        m = jnp.max(work, axis=0, keepdims=True)
        pos = jnp.min(jnp.where(work == m, iota, float(n)), axis=0, keepdims=True)
        hit = iota == pos
        rank = jnp.where(hit, lax.convert_element_type(k, F32), rank)
        work = jnp.where(hit, -jnp.inf, work)
        vals = jnp.where(kio == k, m, vals)
        return work, rank, vals, pos

    init = (s, jnp.full((n, w), 1e9, F32), jnp.zeros((PEER_TOPK, w), F32), jnp.zeros((1, w), F32))
    _, rank, vals, pos = lax.fori_loop(0, PEER_TOPK, body, init)
    return vals, rank, pos


def _max16(s):
    w = s.shape[1]
    kio = lax.broadcasted_iota(jnp.int32, (PEER_TOPK, w), 0)

    def body(k, carry):
        work, vals = carry
        m = jnp.max(work, axis=0, keepdims=True)
        return jnp.where(work == m, -jnp.inf, work), jnp.where(kio == k, m, vals)

    _, vals = lax.fori_loop(0, PEER_TOPK, body, (s, jnp.zeros((PEER_TOPK, w), F32)))
    return vals


def _bitonic_desc(xs, first_k):
    n = len(xs)
    k = first_k
    while k <= n:
        j = k // 2
        while j >= 1:
            for i in range(n):
                p = i ^ j
                if p > i:
                    hi, lo = jnp.maximum(xs[i], xs[p]), jnp.minimum(xs[i], xs[p])
                    xs[i], xs[p] = (hi, lo) if (i & k) == 0 else (lo, hi)
            j //= 2
        k *= 2
    return xs


def _sorted_top16(s):
    n = s.shape[0] // 8
    xs = _bitonic_desc([s[8 * a:8 * a + 8, :] for a in range(n)], 2)
    shift = 4
    while shift >= 1:
        ys = [pltpu.roll(x, shift, 0) for x in xs]
        xs = _bitonic_desc([jnp.maximum(xs[i], ys[n - 1 - i]) for i in range(n)], n)
        shift //= 2
    return jnp.concatenate([x[0:1, :] for x in xs], axis=0)


_STAIR = [(j, PEER_TOPK // (j + 1)) for j in range(PEER_TOPK)]
_STAIR_ROWS = -(-sum(k for _, k in _STAIR) // 8) * 8


def _stair_candidates(v1, v2):
    w = v1.shape[1]
    rows = [v1[j:j + 1] + v2[0:k] for j, k in _STAIR]
    npad = _STAIR_ROWS - sum(k for _, k in _STAIR)
    return jnp.concatenate(rows + [jnp.full((npad, w), -jnp.inf, F32)], axis=0)


def _stair_positions(w):
    rows = [float(PEER_TOPK * j) + lax.broadcasted_iota(jnp.int32, (k, w), 0).astype(F32) for j, k in _STAIR]
    npad = _STAIR_ROWS - sum(k for _, k in _STAIR)
    return jnp.concatenate(rows + [jnp.full((npad, w), 1e9, F32)], axis=0)


def _next_up(x):
    b = lax.bitcast_convert_type(x, jnp.int32)
    up = jnp.where(x > 0.0, b + 1, jnp.where(x < 0.0, b - 1, jnp.int32(0x00800000)))
    return lax.bitcast_convert_type(up, F32)


def _route_kernel(xm_ref, wq_ref, keys_ref, s1m_ref, qrow_ref, e1_ref, s2m_ref, pb_ref, e2_ref,
                  thr_ref, qs):
    qs[...] = lax.dot_general(wq_ref[...], xm_ref[...], (((1,), (1,)), ((), ())),
                              preferred_element_type=F32)
    w = xm_ref.shape[0]

    def count(mask):
        return jnp.sum(mask.astype(F32), axis=0, keepdims=True)

    def emit(h, s1, s2, in1, in2, m1, m2, vc, qrow, pb, thr_up, bad, thr_low):
        z = jnp.sum(jnp.exp(vc - vc[0:1]), axis=0, keepdims=True)
        s1m_ref[h] = jnp.where(in1, s1, -jnp.inf)
        s2m_ref[h] = jnp.where(in2, s2, -jnp.inf)
        e1_ref[h] = jnp.where(in1, jnp.exp(s1 - m1), 0.0) / z
        e2_ref[h] = jnp.where(in2, jnp.exp(s2 - m2), 0.0)
        qrow_ref[h] = qrow
        pb_ref[h] = pb
        thr = vc[PEER_TOPK - 1:PEER_TOPK]
        thr_ref[h] = jnp.concatenate([thr, thr_up, bad, thr_low, jnp.zeros((4, w), F32)], axis=0)

    def head(h, carry):
        base = pl.multiple_of(h * KEY_DIM, KEY_DIM)
        q1 = qs[pl.ds(base, N_KEYS), :].astype(BF16)
        q2 = qs[pl.ds(base + N_KEYS, N_KEYS), :].astype(BF16)
        s1 = jnp.dot(keys_ref[2 * h], q1, preferred_element_type=F32)
        s2 = jnp.dot(keys_ref[2 * h + 1], q2, preferred_element_type=F32)

        v1 = _sorted_top16(s1)
        v2 = _sorted_top16(s2)
        in1 = s1 >= v1[PEER_TOPK - 1:PEER_TOPK]
        in2 = s2 >= v2[PEER_TOPK - 1:PEER_TOPK]
        cand = _stair_candidates(v1, v2)
        vc = _max16(cand)
        thr = vc[PEER_TOPK - 1:PEER_TOPK]
        zero = jnp.zeros((N_KEYS, w), F32)
        k = float(PEER_TOPK)
        bad = jnp.abs(count(in1) - k) + jnp.abs(count(in2) - k) + jnp.abs(count(cand >= thr) - k)
        top = lambda v: jnp.maximum(jnp.abs(v[0:1]), jnp.abs(v[PEER_TOPK - 1:PEER_TOPK]))
        delta = (top(v1) + top(v2)) * (2.0 ** -21)
        lo = thr - 2.0 * delta
        flag = bad
        for jrow in range(PEER_TOPK):
            csum = v1[jrow:jrow + 1] + v2
            flag = flag + count((csum < thr) & (csum >= lo))
        emit(h, s1, s2, in1, in2, v1[0:1], v2[0:1], vc, zero, zero, thr, flag, thr - delta)

        @pl.when(jnp.max(bad) > 0.0)
        def _():
            xv1, r1, _ = _top16(s1)
            xv2, r2, _ = _top16(s2)
            xcand = _stair_candidates(xv1, xv2)
            xvc, _, prow = _top16(xcand)
            riota = lax.broadcasted_iota(jnp.int32, xcand.shape, 0).astype(F32)
            pthr = jnp.sum(jnp.where(riota == prow, _stair_positions(w), 0.0), axis=0, keepdims=True)
            emit(h, s1, s2, r1 < 100.0, r2 < 100.0, xv1[0:1], xv2[0:1], xvc,
                 pthr - k * r1, r2, _next_up(xvc[PEER_TOPK - 1:PEER_TOPK]), flag, thr - delta)

        return carry

    lax.fori_loop(0, PEER_HEADS, head, 0)


def _route(xm, wq_t, keys, l):
    t = xm.shape[0]
    big = jax.ShapeDtypeStruct((PEER_HEADS, N_KEYS, t), F32)
    bspec = pl.BlockSpec((PEER_HEADS, N_KEYS, ROUTE_TT), lambda i: (0, 0, i))
    return pl.pallas_call(
        _route_kernel,
        grid=(t // ROUTE_TT,),
        in_specs=[
            pl.BlockSpec((ROUTE_TT, D_MODEL), lambda i: (i, 0)),
            pl.BlockSpec((None, PEER_HEADS * KEY_DIM, D_MODEL), lambda i: (l, 0, 0)),
            pl.BlockSpec((None, 2 * PEER_HEADS, N_KEYS, N_KEYS), lambda i: (l, 0, 0, 0)),
        ],
        out_specs=[bspec] * 6 + [pl.BlockSpec((PEER_HEADS, 8, ROUTE_TT), lambda i: (0, 0, i))],
        out_shape=[big] * 6 + [jax.ShapeDtypeStruct((PEER_HEADS, 8, t), F32)],
        scratch_shapes=[pltpu.VMEM((PEER_HEADS * KEY_DIM, ROUTE_TT), F32)],
        compiler_params=_params(("parallel",)),
    )(xm, wq_t, keys)


def _peer_kernel(flag_ref, xm_ref, u_ref, vt_ref, s1m_ref, qrow_ref, e1_ref, s2m_ref, pb_ref, e2_ref,
                 thr_ref, x1_ref, mod_ref, g_ref, b_ref, o_ref, ht, wacc, pt, acc):
    j = pl.program_id(1)
    nrow = PEER_EB // N_KEYS

    @pl.when(j == 0)
    def _():
        acc[...] = jnp.zeros_like(acc)

    rpass = 4
    nlg = PEER_TT // 128

    def rows_of(tile, r0):
        return jnp.stack([jnp.broadcast_to(tile[r:r + 1, :], (8, 128)) for r in range(r0, r0 + rpass)])

    def gate_work(hp, lg, exact):
        lanes = slice(lg * 128, (lg + 1) * 128)
        heads = (2 * hp, 2 * hp + 1)
        for rp in range(0, nrow, rpass):
            e1r = [rows_of(e1_ref[h, :, lanes], rp) for h in heads]
            if exact:
                s1r = [rows_of(s1m_ref[h, :, lanes], rp) for h in heads]
                qr = [rows_of(qrow_ref[h, :, lanes], rp) for h in heads]
            else:
                need = [thr_ref[h, 3:4, lanes] - rows_of(s1m_ref[h, :, lanes], rp) for h in heads]
            for v in range(N_KEYS // 8):
                sub = slice(v * 8, (v + 1) * 8)
                gate = None
                for i, h in enumerate(heads):
                    if exact:
                        first = pb_ref[h, sub, lanes][None] <= qr[i]
                        limit = jnp.where(first, thr_ref[h, 0:1, lanes], thr_ref[h, 1:2, lanes])
                        sel = s1r[i] + s2m_ref[h, sub, lanes][None] >= limit
                    else:
                        sel = s2m_ref[h, sub, lanes][None] >= need[i]
                    g = jnp.where(sel, e1r[i] * e2_ref[h, sub, lanes][None], 0.0)
                    gate = g if gate is None else gate + g
                if hp == 0:
                    wacc[rp:rp + rpass, sub, lanes] = gate
                else:
                    wacc[rp:rp + rpass, sub, lanes] += gate

    hfull = lax.dot_general(u_ref[...], xm_ref[...], (((1,), (1,)), ((), ())),
                            preferred_element_type=F32)
    for r in range(nrow):
        ht[r] = hfull[r * N_KEYS:(r + 1) * N_KEYS, :]
    for hp in range(PEER_HEADS // 2):
        for lg in range(nlg):
            row = pl.program_id(0) * nlg + lg
            needs_sum = (flag_ref[row, 2 * hp] + flag_ref[row, 2 * hp + 1]) > 0
            pl.when(needs_sum)(functools.partial(gate_work, hp, lg, True))
            pl.when(jnp.logical_not(needs_sum))(functools.partial(gate_work, hp, lg, False))
    for lg in range(PEER_TT // 128):
        lanes = slice(lg * 128, (lg + 1) * 128)
        for r in range(nrow):
            pt[r * N_KEYS:(r + 1) * N_KEYS, lanes] = (
                wacc[r, :, lanes] * _gelu(ht[r, :, lanes])).astype(BF16)
    acc[...] += jnp.dot(vt_ref[...], pt[...], preferred_element_type=F32)

    @pl.when(j == pl.num_programs(1) - 1)
    def _():
        alpha = (2 * 4) ** 0.25
        ff = acc[...].T
        o_ref[...] = _layer_norm(alpha * x1_ref[...] + mod_ref[0, 5:6, :] * ff, g_ref[...], b_ref[...])


def _peer(xm, u_b, vt_b, routing, x1, mods, mod_row_tt, ln_g, ln_b, l):
    t = xm.shape[0]
    once = pl.Buffered(1)
    tok = lambda w: pl.BlockSpec((PEER_TT, w), lambda i, j: (i, 0), pipeline_mode=once)
    rspec = pl.BlockSpec((PEER_HEADS, N_KEYS, PEER_TT), lambda i, j: (0, 0, i), pipeline_mode=once)
    rowspec = pl.BlockSpec((PEER_HEADS, PEER_EB // N_KEYS, PEER_TT), lambda i, j: (0, j, i))
    full = lambda shape: pl.BlockSpec((None,) + shape, lambda i, j: (l,) + (0,) * len(shape))
    nblk = N_EXPERTS // PEER_EB
    bad = routing[6][:, 2, :].reshape(PEER_HEADS, t // 128, 128)
    flags = (jnp.max(bad, axis=2) > 0.0).astype(jnp.int32).T
    return pl.pallas_call(
        _peer_kernel,
        grid=(t // PEER_TT, nblk),
        in_specs=[
            pl.BlockSpec(memory_space=pltpu.SMEM),
            tok(D_MODEL),
            pl.BlockSpec((None, PEER_EB, D_MODEL), lambda i, j: (l, j, 0)),
            pl.BlockSpec((None, D_MODEL, PEER_EB), lambda i, j: (l, 0, j)),
            rowspec, rowspec, rowspec, rspec, rspec, rspec,
            pl.BlockSpec((PEER_HEADS, 8, PEER_TT), lambda i, j: (0, 0, i)),
            tok(D_MODEL),
            pl.BlockSpec((None, 1, 6, D_MODEL), lambda i, j: (l, mod_row_tt(i), 0, 0)),
            full((1, D_MODEL)), full((1, D_MODEL)),
        ],
        out_specs=pl.BlockSpec((PEER_TT, D_MODEL), lambda i, j: (i, 0)),
        out_shape=jax.ShapeDtypeStruct((t, D_MODEL), F32),
        scratch_shapes=[pltpu.VMEM((PEER_EB // N_KEYS, N_KEYS, PEER_TT), F32),
                        pltpu.VMEM((PEER_EB // N_KEYS, N_KEYS, PEER_TT), F32),
                        pltpu.VMEM((PEER_EB, PEER_TT), BF16),
                        pltpu.VMEM((D_MODEL, PEER_TT), F32)],
        compiler_params=_params(("parallel", "arbitrary")),
    )(flags, xm, u_b, vt_b, *routing, x1, mods, ln_g, ln_b)


def _dft_tables(length):
    n = np.arange(length)
    ang = 2.0 * np.pi * ((n[:, None] * n[None, :]) % length) / length
    dl = np.concatenate([np.cos(ang), -np.sin(ang)], axis=1) / math.sqrt(length)
    c = np.arange(FFT_GROUP_CH)
    angc = 2.0 * np.pi * ((c[:, None] * c[None, :]) % FFT_GROUP_CH) / FFT_GROUP_CH
    eye = np.eye(FFT_GROUPS)
    dc = np.concatenate([np.kron(eye, np.cos(angc)), np.kron(eye, np.sin(angc))], axis=1)
    dc = dc / math.sqrt(FFT_GROUP_CH)
    return jnp.asarray(dl, BF16), jnp.asarray(dc, BF16)


def _rope_tables(length, nheads):
    t = np.arange(length)
    pos = np.stack([t // GRID_W, t % GRID_W], axis=1).astype(np.float32)
    n_freq = HEAD_DIM // 4
    inv = (1.0 / (ROPE_BASE ** (np.arange(n_freq, dtype=np.float32) / n_freq))).astype(np.float32)
    ang = pos[:, :, None] * inv[None, None, :]
    cos = np.repeat(np.cos(ang)[:, :, None, :], 2, axis=2).reshape(length, HEAD_DIM)
    sin = np.sin(ang)
    sin = np.stack([-sin, sin], axis=2).reshape(length, HEAD_DIM)
    return (jnp.asarray(np.tile(cos, (1, nheads)), F32), jnp.asarray(np.tile(sin, (1, nheads)), F32))


def kernel(x_prompt, x_sample, cache_k, cache_v, state_ssm_re, state_ssm_im, c, c_ctx, w_ada, b_ada, w_in, conv_w, ssm_lam_re, ssm_lam_im, ssm_log_step, ssm_b_re, ssm_b_im, ssm_c_re, ssm_c_im, ssm_d, ssm_w_glu, attn_sink, w_branch, w_out, ln1_g, ln1_b, ln2_g, ln2_b, peer_wq, peer_subkeys, peer_u, peer_v):
    nb, seq, _ = x_prompt.shape
    nd, lseq, _ = x_sample.shape
    depth = w_in.shape[0]
    t_ctx = nb * seq
    t_all = t_ctx + nd * lseq
    assert t_ctx % lseq == 0 and t_all % PEER_TT == 0 and (2 * seq) % PEER_TT == 0
    assert lseq % SSM_TCHUNK == 0 and seq == SSM_TCHUNK

    x = jnp.concatenate([x_prompt.reshape(t_ctx, D_MODEL), x_sample.reshape(nd * lseq, D_MODEL)], axis=0)

    nrow = -(-(1 + nd) // 8) * 8
    cvecs = jnp.concatenate([c_ctx[None, :], c, jnp.zeros((nrow - 1 - nd, D_MODEL), F32)], axis=0)
    mods_all = _modulation(cvecs, w_ada, b_ada).reshape(depth, nrow, 6, D_MODEL)

    def mod_row_for(tile):
        nctx = t_ctx // tile
        per = lseq // tile
        return lambda i: jnp.where(i < nctx, 0, 1 + (i - nctx) // per)

    mod_row = mod_row_for(ROW_TILE)
    mod_row_tt = mod_row_for(PEER_TT)

    dl_ctx, dft_c = _dft_tables(seq)
    dl_lat, _ = _dft_tables(lseq)
    cosq, sinq = _rope_tables(lseq, N_HEADS)
    cosk, sin_k = _rope_tables(lseq, N_KV)

    gate0 = sum((512,) * 6) + 2 * N_KV * HEAD_DIM
    w_in_b = jnp.concatenate([w_in[:, :, gate0:], w_in[:, :, :gate0]], axis=2).astype(BF16)
    w_glu_b = ssm_w_glu.astype(BF16)
    w_branch_b = w_branch.astype(BF16)
    w_out_b = w_out.astype(BF16)
    wq_t = peer_wq.transpose(0, 2, 1).astype(BF16)
    keys = peer_subkeys.reshape(depth, 2 * PEER_HEADS, N_KEYS, KEY_DIM // 2).astype(BF16)
    u_b = peer_u.astype(BF16)
    vt_b = peer_v.transpose(0, 2, 1).astype(BF16)
    sp = _s5_params(ssm_lam_re, ssm_lam_im, ssm_log_step, ssm_b_re, ssm_b_im, ssm_c_re, ssm_c_im)
    sink = attn_sink.reshape(depth, 1, N_HEADS)
    ssm_d3 = ssm_d.reshape(depth, 1, -1)
    ln1_g3, ln1_b3 = ln1_g.reshape(depth, 1, -1), ln1_b.reshape(depth, 1, -1)
    ln2_g3, ln2_b3 = ln2_g.reshape(depth, 1, -1), ln2_b.reshape(depth, 1, -1)
    ck = cache_k.reshape(nd, depth, -1, N_KV * HEAD_DIM)
    cv = cache_v.reshape(nd, depth, -1, N_KV * HEAD_DIM)
    nbp_c = -(-nb // SSM_BROWS) * SSM_BROWS
    nbp_l = -(-nd // SSM_BROWS) * SSM_BROWS
    zero_state = jnp.zeros((SSM_GROUPS // SSM_GBLK, nbp_c, 2 * SSM_GBLK * SSM_STATE), F32)

    new_k, new_v, new_re, new_im = [], [], [], []
    for l in range(depth):
        z = _in_proj(x, mods_all, w_in_b, mod_row_for(IN_TILE), l)

        yf, yc, ya = _mixer_ctx(z, nb, seq, conv_w, sink, dl_ctx, dft_c, l)
        yf, yc = _fftconv_lat(z, t_ctx, nd, lseq, conv_w, dl_lat, dft_c, yf, yc, l)
        ya = _attn_lat(z, t_ctx, nd, lseq, ck, cv, cosq, sinq, cosk, sin_k, sink, ya, l)

        zs = z[:, ZS_BLK * 512:(ZS_BLK + 1) * 512]
        u_c, _ = _to_time_major(zs[:t_ctx], nb, seq)
        ysf_c, ysb_c, hf_c, hb_c = _s5(u_c, zero_state, zero_state, sp, nbp_c // SSM_BROWS,
                                       seq // SSM_TCHUNK, l)
        u_l, _ = _to_time_major(zs[t_ctx:], nd, lseq)
        h0f = _state_to_blocks(state_ssm_re[:, l, 0], state_ssm_im[:, l, 0], nbp_l)
        h0b = _state_to_blocks(state_ssm_re[:, l, 1], state_ssm_im[:, l, 1], nbp_l)
        ysf_l, ysb_l, _, _ = _s5(u_l, h0f, h0b, sp, nbp_l // SSM_BROWS, lseq // SSM_TCHUNK, l)
        ysf = jnp.concatenate([_from_time_major(ysf_c, nb, nbp_c, seq), _from_time_major(ysf_l, nd, nbp_l, lseq)], axis=0)
        ysb = jnp.concatenate([_from_time_major(ysb_c, nb, nbp_c, seq), _from_time_major(ysb_l, nd, nbp_l, lseq)], axis=0)

        x1, xm2 = _merge(x, mods_all, mod_row, yf, yc, ya, ysf, ysb, z,
                         ssm_d3, w_glu_b, w_branch_b, w_out_b, ln1_g3, ln1_b3, l)

        routing = _route(xm2, wq_t, keys, l)
        x = _peer(xm2, u_b, vt_b, routing, x1, mods_all, mod_row_tt, ln2_g3, ln2_b3, l)

        kv = z[:t_ctx, ZK_BLK * 128:(ZV_BLK + 1) * 128].reshape(nb, seq, 2, N_KV, HEAD_DIM)
        new_k.append(kv[:, :, 0])
        new_v.append(kv[:, :, 1])
        fre, fim = _blocks_to_state(hf_c, nb)
        bre, bim = _blocks_to_state(hb_c, nb)
        new_re.append(jnp.stack([fre, bre], axis=1))
        new_im.append(jnp.stack([fim, bim], axis=1))

    return (x[:t_ctx].reshape(nb, seq, D_MODEL), x[t_ctx:].reshape(nd, lseq, D_MODEL),
            jnp.stack(new_k, axis=1), jnp.stack(new_v, axis=1),
            jnp.stack(new_re, axis=1), jnp.stack(new_im, axis=1))
```

```python
import functools
import math

import numpy as np
import jax
import jax.numpy as jnp
from jax import lax
from jax.experimental import pallas as pl
from jax.experimental.pallas import tpu as pltpu

F32 = jnp.float32
BF16 = jnp.bfloat16

D_MODEL = 1024
GRID_W = 64
N_BRANCH = 4
BRANCH_WIDTH = 512
FFT_GROUPS = 4
FFT_GROUP_CH = 128
CONV_K = 3
SSM_GROUPS = 32
SSM_CH = 16
SSM_STATE = 64
N_HEADS = 8
N_KV = 2
Q_PER_KV = N_HEADS // N_KV
HEAD_DIM = 64
WINDOW = 128
ATT_BLOCK = 128
ROPE_BASE = 10000.0
PEER_HEADS = 8
N_KEYS = 128
N_EXPERTS = N_KEYS * N_KEYS
PEER_TOPK = 16
KEY_DIM = 256
LN_EPS = 1e-5
NEG_INF = -1e30

Z_COLS = N_BRANCH * D_MODEL + 6 * BRANCH_WIDTH + 2 * N_KV * HEAD_DIM
ZG_BLK = 0
ZF_BLK, ZB_BLK, ZC_BLK, ZH_BLK, ZS_BLK, ZQ_BLK = 8, 9, 10, 11, 12, 13
ZK_BLK, ZV_BLK = 56, 57

V7X_VMEM_LIMIT_BYTES = 56 * 1024 * 1024
SSM_GBLK = 8
SSM_TCHUNK = 256
SSM_BROWS = 8
ROW_TILE = 256
IN_TILE = 512
PEER_TT = 512
PEER_EB = 2048
ROUTE_TT = 256


def _params(sem):
    return pltpu.CompilerParams(dimension_semantics=sem, vmem_limit_bytes=V7X_VMEM_LIMIT_BYTES)


def _gelu(x):
    return 0.5 * x * (1.0 + jnp.tanh(0.7978845608028654 * (x + 0.044715 * (x * x * x))))


def _layer_norm(h, g, b):
    mu = jnp.mean(h, axis=-1, keepdims=True)
    hc = h - mu
    var = jnp.mean(hc * hc, axis=-1, keepdims=True)
    return hc * lax.rsqrt(var + LN_EPS) * g + b


def _mod_kernel(c_ref, w_ref, b_ref, o_ref):
    cv = c_ref[...]
    s = (cv * jax.nn.sigmoid(cv)).astype(BF16)
    o_ref[0] = jnp.dot(s, w_ref[0].astype(BF16), preferred_element_type=F32) + b_ref[0]


def _modulation(cvecs, w_ada, b_ada):
    depth = w_ada.shape[0]
    nrow = cvecs.shape[0]
    return pl.pallas_call(
        _mod_kernel,
        grid=(depth, 6),
        in_specs=[
            pl.BlockSpec((nrow, D_MODEL), lambda l, j: (0, 0)),
            pl.BlockSpec((1, D_MODEL, D_MODEL), lambda l, j: (l, 0, j)),
            pl.BlockSpec((1, 1, D_MODEL), lambda l, j: (l, 0, j)),
        ],
        out_specs=pl.BlockSpec((1, nrow, D_MODEL), lambda l, j: (l, 0, j)),
        out_shape=jax.ShapeDtypeStruct((depth, nrow, 6 * D_MODEL), F32),
        compiler_params=_params(("parallel", "parallel")),
    )(cvecs, w_ada, b_ada.reshape(depth, 1, 6 * D_MODEL))


def _win_kernel(x_ref, mod_ref, w_ref, z_ref):
    sh = mod_ref[0, 0:1, :]
    sc = mod_ref[0, 1:2, :]
    xm = (x_ref[...] * (1.0 + sc) + sh).astype(BF16)
    z_ref[...] = jnp.dot(xm, w_ref[...], preferred_element_type=F32)


def _in_proj(x, mods, w_in, mod_row, l):
    t = x.shape[0]
    ncol = Z_COLS // 2
    return pl.pallas_call(
        _win_kernel,
        grid=(2, t // IN_TILE),
        in_specs=[
            pl.BlockSpec((IN_TILE, D_MODEL), lambda c, i: (i, 0)),
            pl.BlockSpec((None, 1, 6, D_MODEL), lambda c, i: (l, mod_row(i), 0, 0)),
            pl.BlockSpec((None, D_MODEL, ncol), lambda c, i: (l, 0, c)),
        ],
        out_specs=pl.BlockSpec((IN_TILE, ncol), lambda c, i: (i, c)),
        out_shape=jax.ShapeDtypeStruct((t, Z_COLS), F32),
        compiler_params=_params(("parallel", "parallel")),
    )(x, mods, w_in)


def _fft_conv(zf_ref, zb_ref, zc_ref, zh_ref, cw_ref, dl_ref, dc_ref, yf_ref, yc_ref):
    length = zf_ref.shape[0]
    zf = zf_ref[...].astype(BF16)
    ab = jnp.dot(zf, dc_ref[...], preferred_element_type=F32)
    ab = jnp.concatenate([ab[:, :BRANCH_WIDTH], ab[:, BRANCH_WIDTH:]], axis=0).astype(BF16)
    yf_ref[...] = jnp.dot(dl_ref[...], ab, preferred_element_type=F32).astype(BF16)
    g = zc_ref[...] * zh_ref[...]
    row = lax.broadcasted_iota(jnp.int32, g.shape, 0)
    prev = jnp.where(row == 0, 0.0, pltpu.roll(g, 1, 0))
    nxt = jnp.where(row == length - 1, 0.0, pltpu.roll(g, length - 1, 0))
    conv = cw_ref[0:1, :] * prev + cw_ref[1:2, :] * g + cw_ref[2:3, :] * nxt
    yc_ref[...] = (zb_ref[...] * conv).astype(BF16)


def _softmax_pv(s, sink, v):
    m = jnp.maximum(jnp.max(s, axis=1, keepdims=True), sink)
    p = jnp.exp(s - m)
    den = jnp.sum(p, axis=1, keepdims=True) + jnp.exp(sink - m)
    return jnp.dot(p.astype(BF16), v, preferred_element_type=F32) / den


def _mixer_ctx_kernel(zf_ref, zb_ref, zc_ref, zh_ref, zq_ref, zk_ref, zv_ref, cw_ref, sink_ref,
                      dl_ref, dc_ref, yf_in, yc_in, ya_in, yf_ref, yc_ref, ya_ref):
    del yf_in, yc_in, ya_in
    _fft_conv(zf_ref, zb_ref, zc_ref, zh_ref, cw_ref, dl_ref, dc_ref, yf_ref, yc_ref)
    q = zq_ref[...] * (HEAD_DIM ** -0.5)
    k = zk_ref[...]
    v = zv_ref[...]
    outs = []
    for h in range(N_HEADS):
        g = h // Q_PER_KV
        qh = q[:, h * HEAD_DIM:(h + 1) * HEAD_DIM].astype(BF16)
        kg = k[:, g * HEAD_DIM:(g + 1) * HEAD_DIM].astype(BF16)
        vg = v[:, g * HEAD_DIM:(g + 1) * HEAD_DIM].astype(BF16)
        s = lax.dot_general(qh, kg, (((1,), (1,)), ((), ())), preferred_element_type=F32)
        outs.append(_softmax_pv(s, sink_ref[0:1, h:h + 1], vg))
    ya_ref[...] = jnp.concatenate(outs, axis=1).astype(BF16)


def _zspec(rows, width, row_fn, col_blk):
    return pl.BlockSpec((rows, width), lambda *a: (row_fn(*a), col_blk))


def _mixer_ctx(z, nb, seq, conv_w, sink, dft_l, dft_c, l):
    rf = lambda b: b
    full = lambda shape: pl.BlockSpec(shape, lambda b: (0,) * len(shape))
    layer = lambda shape: pl.BlockSpec((None,) + shape, lambda b: (l,) + (0,) * len(shape))
    out = jax.ShapeDtypeStruct((z.shape[0], BRANCH_WIDTH), BF16)
    ospec = pl.BlockSpec((seq, BRANCH_WIDTH), lambda b: (b, 0))
    anyspec = pl.BlockSpec(memory_space=pl.ANY)
    zeros = [jnp.zeros(out.shape, BF16) for _ in range(3)]
    return pl.pallas_call(
        _mixer_ctx_kernel,
        grid=(nb,),
        in_specs=[
            _zspec(seq, 512, rf, ZF_BLK), _zspec(seq, 512, rf, ZB_BLK), _zspec(seq, 512, rf, ZC_BLK),
            _zspec(seq, 512, rf, ZH_BLK), _zspec(seq, 512, rf, ZQ_BLK),
            _zspec(seq, 128, rf, ZK_BLK), _zspec(seq, 128, rf, ZV_BLK),
            layer((CONV_K, BRANCH_WIDTH)), layer((1, N_HEADS)),
            full((seq, 2 * seq)), full((BRANCH_WIDTH, 2 * BRANCH_WIDTH)),
            anyspec, anyspec, anyspec,
        ],
        out_specs=[ospec, ospec, ospec],
        out_shape=[out, out, out],
        input_output_aliases={11: 0, 12: 1, 13: 2},
        compiler_params=_params(("parallel",)),
    )(z, z, z, z, z, z, z, conv_w, sink, dft_l, dft_c, *zeros)


def _fftconv_lat_kernel(zf_ref, zb_ref, zc_ref, zh_ref, cw_ref, dl_ref, dc_ref, yf_in, yc_in,
                        yf_ref, yc_ref):
    del yf_in, yc_in
    _fft_conv(zf_ref, zb_ref, zc_ref, zh_ref, cw_ref, dl_ref, dc_ref, yf_ref, yc_ref)


def _fftconv_lat(z, row0, nb, seq, conv_w, dft_l, dft_c, yf, yc, l):
    rf = lambda b: row0 // seq + b
    full = lambda shape: pl.BlockSpec(shape, lambda b: (0,) * len(shape))
    out = jax.ShapeDtypeStruct(yf.shape, BF16)
    ospec = pl.BlockSpec((seq, BRANCH_WIDTH), lambda b: (row0 // seq + b, 0))
    anyspec = pl.BlockSpec(memory_space=pl.ANY)
    return pl.pallas_call(
        _fftconv_lat_kernel,
        grid=(nb,),
        in_specs=[
            _zspec(seq, 512, rf, ZF_BLK), _zspec(seq, 512, rf, ZB_BLK), _zspec(seq, 512, rf, ZC_BLK),
            _zspec(seq, 512, rf, ZH_BLK),
            pl.BlockSpec((None, CONV_K, BRANCH_WIDTH), lambda b: (l, 0, 0)),
            full((seq, 2 * seq)), full((BRANCH_WIDTH, 2 * BRANCH_WIDTH)),
            anyspec, anyspec,
        ],
        out_specs=[ospec, ospec],
        out_shape=[out, out],
        input_output_aliases={7: 0, 8: 1},
        compiler_params=_params(("parallel",)),
    )(z, z, z, z, conv_w, dft_l, dft_c, yf, yc)


def _rope(x, cos, sin):
    lane = lax.broadcasted_iota(jnp.int32, (x.shape[0], 128), 1)
    first = (lane & 31) < 16
    parts = []
    for c in range(x.shape[1] // 128):
        xc = x[:, c * 128:(c + 1) * 128]
        swapped = jnp.where(first, pltpu.roll(xc, 112, 1), pltpu.roll(xc, 16, 1))
        parts.append(xc * cos[:, c * 128:(c + 1) * 128] + swapped * sin[:, c * 128:(c + 1) * 128])
    return parts[0] if len(parts) == 1 else jnp.concatenate(parts, axis=1)


def _attn_lat_kernel(zq_ref, zk_ref, zv_ref, ck_ref, cv_ref, cosq_ref, sinq_ref, cosk_ref, sinkk_ref,
                     sink_ref, ya_in, ya_ref):
    del ya_in
    n = pl.program_id(1)
    nblk = pl.num_programs(1)
    nwin = 3 * ATT_BLOCK
    q = _rope(zq_ref[...], cosq_ref[...], sinq_ref[...]) * (HEAD_DIM ** -0.5)
    ws = pl.multiple_of(jnp.clip(n - 1, 0, nblk - 3) * ATT_BLOCK, ATT_BLOCK)
    kw = _rope(zk_ref[pl.ds(ws, nwin), :], cosk_ref[pl.ds(ws, nwin), :], sinkk_ref[pl.ds(ws, nwin), :])
    vw = zv_ref[pl.ds(ws, nwin), :]
    k_all = jnp.concatenate([kw, ck_ref[...]], axis=0)
    v_all = jnp.concatenate([vw, cv_ref[...]], axis=0)
    nkey = k_all.shape[0]
    qpos = n * ATT_BLOCK + lax.broadcasted_iota(jnp.int32, (ATT_BLOCK, nkey), 0)
    col = lax.broadcasted_iota(jnp.int32, (ATT_BLOCK, nkey), 1)
    valid = (jnp.abs(qpos - (ws + col)) <= WINDOW) | (col >= nwin)
    outs = []
    for h in range(N_HEADS):
        g = h // Q_PER_KV
        qh = q[:, h * HEAD_DIM:(h + 1) * HEAD_DIM].astype(BF16)
        kg = k_all[:, g * HEAD_DIM:(g + 1) * HEAD_DIM].astype(BF16)
        vg = v_all[:, g * HEAD_DIM:(g + 1) * HEAD_DIM].astype(BF16)
        s = lax.dot_general(qh, kg, (((1,), (1,)), ((), ())), preferred_element_type=F32)
        s = jnp.where(valid, s, NEG_INF)
        outs.append(_softmax_pv(s, sink_ref[0:1, h:h + 1], vg))
    ya_ref[...] = jnp.concatenate(outs, axis=1).astype(BF16)


def _attn_lat(z, row0, nb, seq, ck, cv, cosq, sinq, cosk, sin_k, sink, ya, l):
    nblk = seq // ATT_BLOCK
    kvw = N_KV * HEAD_DIM
    past = ck.shape[2]
    full = lambda shape: pl.BlockSpec(shape, lambda b, n: (0,) * len(shape))
    cache = pl.BlockSpec((None, None, past, kvw), lambda b, n: (b, l, 0, 0))
    return pl.pallas_call(
        _attn_lat_kernel,
        grid=(nb, nblk),
        in_specs=[
            pl.BlockSpec((ATT_BLOCK, 512), lambda b, n: (row0 // ATT_BLOCK + b * nblk + n, ZQ_BLK)),
            pl.BlockSpec((seq, kvw), lambda b, n: (row0 // seq + b, ZK_BLK)),
            pl.BlockSpec((seq, kvw), lambda b, n: (row0 // seq + b, ZV_BLK)),
            cache, cache,
            pl.BlockSpec((ATT_BLOCK, 512), lambda b, n: (n, 0)),
            pl.BlockSpec((ATT_BLOCK, 512), lambda b, n: (n, 0)),
            full((seq, kvw)), full((seq, kvw)),
            pl.BlockSpec((None, 1, N_HEADS), lambda b, n: (l, 0, 0)),
            pl.BlockSpec(memory_space=pl.ANY),
        ],
        out_specs=pl.BlockSpec((ATT_BLOCK, 512), lambda b, n: (row0 // ATT_BLOCK + b * nblk + n, 0)),
        out_shape=jax.ShapeDtypeStruct(ya.shape, BF16),
        input_output_aliases={10: 0},
        compiler_params=_params(("parallel", "parallel")),
    )(z, z, z, ck, cv, cosq, sinq, cosk, sin_k, sink, ya)


def _s5_kernel(uf_ref, ub_ref, h0f_ref, h0b_ref, wbf_ref, wbb_ref, cf_ref, cb_ref, af_ref, ab_ref,
               yf_ref, yb_ref, hf_ref, hb_ref, buff, bufb, hst):
    c = pl.program_id(2)
    half = SSM_GBLK * SSM_STATE
    steps = uf_ref.shape[0] // SSM_BROWS

    @pl.when(c == 0)
    def _():
        hst[0] = h0f_ref[0]
        hst[1] = h0b_ref[0]

    buff[...] = jnp.dot(uf_ref[...].astype(BF16), wbf_ref[...], preferred_element_type=F32)
    bufb[...] = jnp.dot(ub_ref[...].astype(BF16), wbb_ref[...], preferred_element_type=F32)
    afr = jnp.broadcast_to(af_ref[0:1, :], (SSM_BROWS, half))
    afi = jnp.broadcast_to(af_ref[1:2, :], (SSM_BROWS, half))
    abr = jnp.broadcast_to(ab_ref[0:1, :], (SSM_BROWS, half))
    abi = jnp.broadcast_to(ab_ref[1:2, :], (SSM_BROWS, half))

    def step(t, carry):
        hfr, hfi, hbr, hbi = carry
        rf = pl.multiple_of(t * SSM_BROWS, SSM_BROWS)
        nfr = afr * hfr - afi * hfi + buff[pl.ds(rf, SSM_BROWS), 0:half]
        nfi = afr * hfi + afi * hfr + buff[pl.ds(rf, SSM_BROWS), half:2 * half]
        buff[pl.ds(rf, SSM_BROWS), 0:half] = nfr
        buff[pl.ds(rf, SSM_BROWS), half:2 * half] = nfi
        rb = pl.multiple_of((steps - 1 - t) * SSM_BROWS, SSM_BROWS)
        nbr = abr * hbr - abi * hbi + bufb[pl.ds(rb, SSM_BROWS), 0:half]
        nbi = abr * hbi + abi * hbr + bufb[pl.ds(rb, SSM_BROWS), half:2 * half]
        bufb[pl.ds(rb, SSM_BROWS), 0:half] = nbr
        bufb[pl.ds(rb, SSM_BROWS), half:2 * half] = nbi
        return nfr, nfi, nbr, nbi

    init = (hst[0, :, 0:half], hst[0, :, half:2 * half], hst[1, :, 0:half], hst[1, :, half:2 * half])
    hfr, hfi, hbr, hbi = lax.fori_loop(0, steps, step, init, unroll=4)
    hst[0, :, 0:half] = hfr
    hst[0, :, half:2 * half] = hfi
    hst[1, :, 0:half] = hbr
    hst[1, :, half:2 * half] = hbi
    yf_ref[...] = jnp.dot(buff[...].astype(BF16), cf_ref[...], preferred_element_type=F32)
    yb_ref[...] = jnp.dot(bufb[...].astype(BF16), cb_ref[...], preferred_element_type=F32)

    @pl.when(c == pl.num_programs(2) - 1)
    def _():
        hf_ref[0] = hst[0]
        hb_ref[0] = hst[1]


def _s5(u_tm, h0f, h0b, sp, nbb, nchunk, l):
    rows = SSM_TCHUNK * SSM_BROWS
    ngb = SSM_GROUPS // SSM_GBLK
    width = 2 * SSM_GBLK * SSM_STATE
    nbrow = nbb * SSM_BROWS
    cw = SSM_GBLK * SSM_CH
    fwd = lambda bb, j, c: (bb * nchunk + c, j)
    bwd = lambda bb, j, c: (bb * nchunk + nchunk - 1 - c, j)
    par = lambda shape, d: pl.BlockSpec((None, None, None) + shape, lambda bb, j, c: (l, d, j, 0, 0))
    st = pl.BlockSpec((1, SSM_BROWS, width), lambda bb, j, c: (j, bb, 0))
    wb, cm, a = sp
    ysh = jax.ShapeDtypeStruct(u_tm.shape, F32)
    hsh = jax.ShapeDtypeStruct((ngb, nbrow, width), F32)
    return pl.pallas_call(
        _s5_kernel,
        grid=(nbb, ngb, nchunk),
        in_specs=[
            pl.BlockSpec((rows, cw), fwd), pl.BlockSpec((rows, cw), bwd), st, st,
            par((cw, width), 0), par((cw, width), 1), par((width, cw), 0), par((width, cw), 1),
            par((2, width // 2), 0), par((2, width // 2), 1),
        ],
        out_specs=[pl.BlockSpec((rows, cw), fwd), pl.BlockSpec((rows, cw), bwd), st, st],
        out_shape=[ysh, ysh, hsh, hsh],
        scratch_shapes=[pltpu.VMEM((rows, width), F32), pltpu.VMEM((rows, width), F32),
                        pltpu.VMEM((2, SSM_BROWS, width), F32)],
        compiler_params=_params(("parallel", "parallel", "arbitrary")),
    )(u_tm, u_tm, h0f, h0b, wb, wb, cm, cm, a, a)


def _s5_params(lam_re, lam_im, log_step, b_re, b_im, c_re, c_im):
    lead = lam_re.shape[:-2]
    dt = jnp.exp(log_step)[..., None]
    mag = jnp.exp(lam_re * dt)
    ar = mag * jnp.cos(lam_im * dt)
    ai = mag * jnp.sin(lam_im * dt)
    den = lam_re * lam_re + lam_im * lam_im
    kr = ((ar - 1.0) * lam_re + ai * lam_im) / den
    ki = (ai * lam_re - (ar - 1.0) * lam_im) / den
    bbr = kr[..., None] * b_re - ki[..., None] * b_im
    bbi = kr[..., None] * b_im + ki[..., None] * b_re
    ngb = SSM_GROUPS // SSM_GBLK
    eye = jnp.eye(SSM_GBLK, dtype=F32)

    def blockdiag_in(m):
        m = m.reshape(lead + (ngb, SSM_GBLK, SSM_STATE, SSM_CH))
        m = jnp.einsum("...jgph,gk->...jghkp", m, eye)
        return m.reshape(lead + (ngb, SSM_GBLK * SSM_CH, SSM_GBLK * SSM_STATE))

    def blockdiag_out(m):
        m = m.reshape(lead + (ngb, SSM_GBLK, SSM_CH, SSM_STATE))
        m = jnp.einsum("...jghp,gk->...jgpkh", m, eye)
        return m.reshape(lead + (ngb, SSM_GBLK * SSM_STATE, SSM_GBLK * SSM_CH))

    wb = jnp.concatenate([blockdiag_in(bbr), blockdiag_in(bbi)], axis=-1).astype(BF16)
    cm = jnp.concatenate([blockdiag_out(c_re), -blockdiag_out(c_im)], axis=-2).astype(BF16)
    a = jnp.stack([ar.reshape(lead + (ngb, -1)), ai.reshape(lead + (ngb, -1))], axis=-2)
    return wb, cm, a


def _to_time_major(u, nb, seq):
    cdim = u.shape[1]
    nbp = -(-nb // SSM_BROWS) * SSM_BROWS
    u = u.reshape(nb, seq, cdim)
    if nbp != nb:
        u = jnp.pad(u, ((0, nbp - nb), (0, 0), (0, 0)))
    u = u.reshape(nbp // SSM_BROWS, SSM_BROWS, seq, cdim).transpose(0, 2, 1, 3)
    return u.reshape(nbp * seq, cdim), nbp


def _from_time_major(y, nb, nbp, seq):
    cdim = y.shape[1]
    y = y.reshape(nbp // SSM_BROWS, seq, SSM_BROWS, cdim).transpose(0, 2, 1, 3)
    return y.reshape(nbp, seq, cdim)[:nb].reshape(nb * seq, cdim)


def _state_to_blocks(re, im, nbp):
    nb = re.shape[0]
    ngb = SSM_GROUPS // SSM_GBLK
    def blk(x):
        return x.reshape(nb, ngb, SSM_GBLK * SSM_STATE).transpose(1, 0, 2)
    h = jnp.concatenate([blk(re), blk(im)], axis=2)
    if nbp != nb:
        h = jnp.pad(h, ((0, 0), (0, nbp - nb), (0, 0)))
    return h


def _blocks_to_state(h, nb):
    half = SSM_GBLK * SSM_STATE
    def unblk(x):
        return x[:, :nb].transpose(1, 0, 2).reshape(nb, SSM_GROUPS, SSM_STATE)
    return unblk(h[:, :, :half]), unblk(h[:, :, half:])


def _merge_kernel(x_ref, mod_ref, yf_ref, yc_ref, ya_ref, ysf_ref, ysb_ref, zs_ref,
                  zg0_ref, zg1_ref, zg2_ref, zg3_ref, d_ref, wglu_ref, wb_ref, wout_ref,
                  g_ref, b_ref, x1_ref, xm_ref):
    ys = ysf_ref[...] + ysb_ref[...] + d_ref[...] * zs_ref[...]
    ys = _gelu(ys)
    yssm = ys * jax.nn.sigmoid(jnp.dot(ys.astype(BF16), wglu_ref[...], preferred_element_type=F32))
    acc = jax.nn.sigmoid(zg0_ref[...]) * jnp.dot(yf_ref[...], wb_ref[0], preferred_element_type=F32)
    acc += jax.nn.sigmoid(zg1_ref[...]) * jnp.dot(yc_ref[...], wb_ref[1], preferred_element_type=F32)
    acc += jax.nn.sigmoid(zg2_ref[...]) * jnp.dot(yssm.astype(BF16), wb_ref[2], preferred_element_type=F32)
    acc += jax.nn.sigmoid(zg3_ref[...]) * jnp.dot(ya_ref[...], wb_ref[3], preferred_element_type=F32)
    mix = jnp.dot(acc.astype(BF16), wout_ref[...], preferred_element_type=F32)
    alpha = (2 * 4) ** 0.25
    x1 = _layer_norm(alpha * x_ref[...] + mod_ref[0, 2:3, :] * mix, g_ref[...], b_ref[...])
    x1_ref[...] = x1
    xm_ref[...] = (x1 * (1.0 + mod_ref[0, 4:5, :]) + mod_ref[0, 3:4, :]).astype(BF16)


def _merge(x, mods, mod_row, yf, yc, ya, ysf, ysb, z, ssm_d, w_glu, w_branch, w_out, ln_g, ln_b, l):
    t = x.shape[0]
    rf = lambda i: i
    row = lambda w: pl.BlockSpec((ROW_TILE, w), lambda i: (i, 0))
    full = lambda shape: pl.BlockSpec((None,) + shape, lambda i: (l,) + (0,) * len(shape))
    return pl.pallas_call(
        _merge_kernel,
        grid=(t // ROW_TILE,),
        in_specs=[
            row(D_MODEL), pl.BlockSpec((None, 1, 6, D_MODEL), lambda i: (l, mod_row(i), 0, 0)),
            row(512), row(512), row(512), row(512), row(512),
            _zspec(ROW_TILE, 512, rf, ZS_BLK),
            _zspec(ROW_TILE, 1024, rf, 0), _zspec(ROW_TILE, 1024, rf, 1),
            _zspec(ROW_TILE, 1024, rf, 2), _zspec(ROW_TILE, 1024, rf, 3),
            full((1, 512)), full((512, 512)), full((N_BRANCH, 512, D_MODEL)), full((D_MODEL, D_MODEL)),
            full((1, D_MODEL)), full((1, D_MODEL)),
        ],
        out_specs=[row(D_MODEL), row(D_MODEL)],
        out_shape=[jax.ShapeDtypeStruct((t, D_MODEL), F32), jax.ShapeDtypeStruct((t, D_MODEL), BF16)],
        compiler_params=_params(("parallel",)),
    )(x, mods, yf, yc, ya, ysf, ysb, z, z, z, z, z, ssm_d, w_glu, w_branch, w_out, ln_g, ln_b)


def _top16(s):
    n, w = s.shape
    iota = lax.broadcasted_iota(jnp.int32, (n, w), 0).astype(F32)
    kio = lax.broadcasted_iota(jnp.int32, (PEER_TOPK, w), 0)

    def body(k, carry):
        work, rank, vals, _ = carry
        m = jnp.max(work, axis=0, keepdims=True)
        pos = jnp.min(jnp.where(work == m, iota, float(n)), axis=0, keepdims=True)
        hit = iota == pos
        rank = jnp.where(hit, lax.convert_element_type(k, F32), rank)
        work = jnp.where(hit, -jnp.inf, work)
        vals = jnp.where(kio == k, m, vals)
        return work, rank, vals, pos

    init = (s, jnp.full((n, w), 1e9, F32), jnp.zeros((PEER_TOPK, w), F32), jnp.zeros((1, w), F32))
    _, rank, vals, pos = lax.fori_loop(0, PEER_TOPK, body, init)
    return vals, rank, pos


def _bitonic_desc(xs, first_k):
    n = len(xs)
    k = first_k
    while k <= n:
        j = k // 2
        while j >= 1:
            for i in range(n):
                p = i ^ j
                if p > i:
                    hi, lo = jnp.maximum(xs[i], xs[p]), jnp.minimum(xs[i], xs[p])
                    xs[i], xs[p] = (hi, lo) if (i & k) == 0 else (lo, hi)
            j //= 2
        k *= 2
    return xs


def _sorted_top16(s):
    n = s.shape[0] // 8
    xs = _bitonic_desc([s[8 * a:8 * a + 8, :] for a in range(n)], 2)
    shift = 4
    while len(xs) < PEER_TOPK:
        ys = [pltpu.roll(x, shift, 0) for x in xs]
        xs = _bitonic_desc(xs + ys[::-1], 2 * len(xs))
        shift //= 2
    n = len(xs)
    while shift >= 1:
        ys = [pltpu.roll(x, shift, 0) for x in xs]
        xs = _bitonic_desc([jnp.maximum(xs[i], ys[n - 1 - i]) for i in range(n)], n)
        shift //= 2
    return jnp.concatenate([x[0:1, :] for x in xs], axis=0)


_STAIR = [(j, PEER_TOPK // (j + 1)) for j in range(PEER_TOPK)]
_STAIR_ROWS = 64


def _stair_candidates(v1, v2):
    w = v1.shape[1]
    rows = [v1[j:j + 1] + v2[0:k] for j, k in _STAIR]
    npad = _STAIR_ROWS - sum(k for _, k in _STAIR)
    return jnp.concatenate(rows + [jnp.full((npad, w), -jnp.inf, F32)], axis=0)


def _stair_positions(w):
    rows = [float(PEER_TOPK * j) + lax.broadcasted_iota(jnp.int32, (k, w), 0).astype(F32) for j, k in _STAIR]
    npad = _STAIR_ROWS - sum(k for _, k in _STAIR)
    return jnp.concatenate(rows + [jnp.full((npad, w), 1e9, F32)], axis=0)


def _next_up(x):
    b = lax.bitcast_convert_type(x, jnp.int32)
    up = jnp.where(x > 0.0, b + 1, jnp.where(x < 0.0, b - 1, jnp.int32(0x00800000)))
    return lax.bitcast_convert_type(up, F32)


def _route_kernel(xm_ref, wq_ref, keys_ref, s1m_ref, qrow_ref, e1_ref, s2m_ref, pb_ref, e2_ref,
                  thr_ref, qs):
    qs[...] = lax.dot_general(wq_ref[...], xm_ref[...], (((1,), (1,)), ((), ())),
                              preferred_element_type=F32)
    w = xm_ref.shape[0]

    def count(mask):
        return jnp.sum(mask.astype(F32), axis=0, keepdims=True)

    def emit(h, s1, s2, in1, in2, m1, m2, vc, qrow, pb, thr_up, bad, thr_low):
        z = jnp.sum(jnp.exp(vc - vc[0:1]), axis=0, keepdims=True)
        s1m_ref[h] = jnp.where(in1, s1, -jnp.inf)
        s2m_ref[h] = jnp.where(in2, s2, -jnp.inf)
        e1_ref[h] = jnp.where(in1, jnp.exp(s1 - m1), 0.0) / z
        e2_ref[h] = jnp.where(in2, jnp.exp(s2 - m2), 0.0)
        qrow_ref[h] = qrow
        pb_ref[h] = pb
        thr = vc[PEER_TOPK - 1:PEER_TOPK]
        thr_ref[h] = jnp.concatenate([thr, thr_up, bad, thr_low, jnp.zeros((4, w), F32)], axis=0)

    def head(h, carry):
        base = pl.multiple_of(h * KEY_DIM, KEY_DIM)
        q1 = qs[pl.ds(base, N_KEYS), :].astype(BF16)
        q2 = qs[pl.ds(base + N_KEYS, N_KEYS), :].astype(BF16)
        s1 = jnp.dot(keys_ref[2 * h], q1, preferred_element_type=F32)
        s2 = jnp.dot(keys_ref[2 * h + 1], q2, preferred_element_type=F32)

        v1 = _sorted_top16(s1)
        v2 = _sorted_top16(s2)
        in1 = s1 >= v1[PEER_TOPK - 1:PEER_TOPK]
        in2 = s2 >= v2[PEER_TOPK - 1:PEER_TOPK]
        cand = _stair_candidates(v1, v2)
        vc = _sorted_top16(cand)
        thr = vc[PEER_TOPK - 1:PEER_TOPK]
        zero = jnp.zeros((N_KEYS, w), F32)
        k = float(PEER_TOPK)
        bad = jnp.abs(count(in1) - k) + jnp.abs(count(in2) - k) + jnp.abs(count(cand >= thr) - k)
        top = lambda v: jnp.maximum(jnp.abs(v[0:1]), jnp.abs(v[PEER_TOPK - 1:PEER_TOPK]))
        delta = (top(v1) + top(v2)) * (2.0 ** -21)
        lo = thr - 2.0 * delta
        flag = bad
        for jrow in range(PEER_TOPK):
            csum = v1[jrow:jrow + 1] + v2
            flag = flag + count((csum < thr) & (csum >= lo))
        emit(h, s1, s2, in1, in2, v1[0:1], v2[0:1], vc, zero, zero, thr, flag, thr - delta)

        @pl.when(jnp.max(bad) > 0.0)
        def _():
            xv1, r1, _ = _top16(s1)
            xv2, r2, _ = _top16(s2)
            xcand = _stair_candidates(xv1, xv2)
            xvc, _, prow = _top16(xcand)
            riota = lax.broadcasted_iota(jnp.int32, xcand.shape, 0).astype(F32)
            pthr = jnp.sum(jnp.where(riota == prow, _stair_positions(w), 0.0), axis=0, keepdims=True)
            emit(h, s1, s2, r1 < 100.0, r2 < 100.0, xv1[0:1], xv2[0:1], xvc,
                 pthr - k * r1, r2, _next_up(xvc[PEER_TOPK - 1:PEER_TOPK]), flag, thr - delta)

        return carry

    lax.fori_loop(0, PEER_HEADS, head, 0)


def _route(xm, wq_t, keys, l):
    t = xm.shape[0]
    big = jax.ShapeDtypeStruct((PEER_HEADS, N_KEYS, t), F32)
    bspec = pl.BlockSpec((PEER_HEADS, N_KEYS, ROUTE_TT), lambda i: (0, 0, i))
    return pl.pallas_call(
        _route_kernel,
        grid=(t // ROUTE_TT,),
        in_specs=[
            pl.BlockSpec((ROUTE_TT, D_MODEL), lambda i: (i, 0)),
            pl.BlockSpec((None, PEER_HEADS * KEY_DIM, D_MODEL), lambda i: (l, 0, 0)),
            pl.BlockSpec((None, 2 * PEER_HEADS, N_KEYS, N_KEYS), lambda i: (l, 0, 0, 0)),
        ],
        out_specs=[bspec] * 6 + [pl.BlockSpec((PEER_HEADS, 8, ROUTE_TT), lambda i: (0, 0, i))],
        out_shape=[big] * 6 + [jax.ShapeDtypeStruct((PEER_HEADS, 8, t), F32)],
        scratch_shapes=[pltpu.VMEM((PEER_HEADS * KEY_DIM, ROUTE_TT), F32)],
        compiler_params=_params(("parallel",)),
    )(xm, wq_t, keys)


def _peer_kernel(flag_ref, xm_ref, u_ref, vt_ref, s1m_ref, qrow_ref, e1_ref, s2m_ref, pb_ref, e2_ref,
                 thr_ref, x1_ref, mod_ref, g_ref, b_ref, o_ref, ht, wacc, pt, acc):
    j = pl.program_id(1)
    nrow = PEER_EB // N_KEYS

    @pl.when(j == 0)
    def _():
        acc[...] = jnp.zeros_like(acc)

    rpass = 4
    nlg = PEER_TT // 128

    def rows_of(tile, r0):
        return jnp.stack([jnp.broadcast_to(tile[r:r + 1, :], (8, 128)) for r in range(r0, r0 + rpass)])

    def gate_work(hp, lg, exact):
        lanes = slice(lg * 128, (lg + 1) * 128)
        heads = (2 * hp, 2 * hp + 1)
        for rp in range(0, nrow, rpass):
            e1r = [rows_of(e1_ref[h, :, lanes], rp) for h in heads]
            if exact:
                s1r = [rows_of(s1m_ref[h, :, lanes], rp) for h in heads]
                qr = [rows_of(qrow_ref[h, :, lanes], rp) for h in heads]
            else:
                need = [thr_ref[h, 3:4, lanes] - rows_of(s1m_ref[h, :, lanes], rp) for h in heads]
            for v in range(N_KEYS // 8):
                sub = slice(v * 8, (v + 1) * 8)
                gate = None
                for i, h in enumerate(heads):
                    if exact:
                        first = pb_ref[h, sub, lanes][None] <= qr[i]
                        limit = jnp.where(first, thr_ref[h, 0:1, lanes], thr_ref[h, 1:2, lanes])
                        sel = s1r[i] + s2m_ref[h, sub, lanes][None] >= limit
                    else:
                        sel = s2m_ref[h, sub, lanes][None] >= need[i]
                    g = jnp.where(sel, e1r[i] * e2_ref[h, sub, lanes][None], 0.0)
                    gate = g if gate is None else gate + g
                if hp == 0:
                    wacc[rp:rp + rpass, sub, lanes] = gate
                else:
                    wacc[rp:rp + rpass, sub, lanes] += gate

    hfull = lax.dot_general(u_ref[...], xm_ref[...], (((1,), (1,)), ((), ())),
                            preferred_element_type=F32)
    for r in range(nrow):
        ht[r] = hfull[r * N_KEYS:(r + 1) * N_KEYS, :]
    for hp in range(PEER_HEADS // 2):
        for lg in range(nlg):
            row = pl.program_id(0) * nlg + lg
            needs_sum = (flag_ref[row, 2 * hp] + flag_ref[row, 2 * hp + 1]) > 0
            pl.when(needs_sum)(functools.partial(gate_work, hp, lg, True))
            pl.when(jnp.logical_not(needs_sum))(functools.partial(gate_work, hp, lg, False))
    for lg in range(PEER_TT // 128):
        lanes = slice(lg * 128, (lg + 1) * 128)
        for r in range(nrow):
            pt[r * N_KEYS:(r + 1) * N_KEYS, lanes] = (
                wacc[r, :, lanes] * _gelu(ht[r, :, lanes])).astype(BF16)
    acc[...] += jnp.dot(vt_ref[...], pt[...], preferred_element_type=F32)

    @pl.when(j == pl.num_programs(1) - 1)
    def _():
        alpha = (2 * 4) ** 0.25
        ff = acc[...].T
        o_ref[...] = _layer_norm(alpha * x1_ref[...] + mod_ref[0, 5:6, :] * ff, g_ref[...], b_ref[...])


def _peer(xm, u_b, vt_b, routing, x1, mods, mod_row_tt, ln_g, ln_b, l):
    t = xm.shape[0]
    once = pl.Buffered(1)
    tok = lambda w: pl.BlockSpec((PEER_TT, w), lambda i, j: (i, 0), pipeline_mode=once)
    rspec = pl.BlockSpec((PEER_HEADS, N_KEYS, PEER_TT), lambda i, j: (0, 0, i), pipeline_mode=once)
    rowspec = pl.BlockSpec((PEER_HEADS, PEER_EB // N_KEYS, PEER_TT), lambda i, j: (0, j, i))
    full = lambda shape: pl.BlockSpec((None,) + shape, lambda i, j: (l,) + (0,) * len(shape))
    nblk = N_EXPERTS // PEER_EB
    bad = routing[6][:, 2, :].reshape(PEER_HEADS, t // 128, 128)
    flags = (jnp.max(bad, axis=2) > 0.0).astype(jnp.int32).T
    return pl.pallas_call(
        _peer_kernel,
        grid=(t // PEER_TT, nblk),
        in_specs=[
            pl.BlockSpec(memory_space=pltpu.SMEM),
            tok(D_MODEL),
            pl.BlockSpec((None, PEER_EB, D_MODEL), lambda i, j: (l, j, 0)),
            pl.BlockSpec((None, D_MODEL, PEER_EB), lambda i, j: (l, 0, j)),
            rowspec, rowspec, rowspec, rspec, rspec, rspec,
            pl.BlockSpec((PEER_HEADS, 8, PEER_TT), lambda i, j: (0, 0, i)),
            tok(D_MODEL),
            pl.BlockSpec((None, 1, 6, D_MODEL), lambda i, j: (l, mod_row_tt(i), 0, 0)),
            full((1, D_MODEL)), full((1, D_MODEL)),
        ],
        out_specs=pl.BlockSpec((PEER_TT, D_MODEL), lambda i, j: (i, 0)),
        out_shape=jax.ShapeDtypeStruct((t, D_MODEL), F32),
        scratch_shapes=[pltpu.VMEM((PEER_EB // N_KEYS, N_KEYS, PEER_TT), F32),
                        pltpu.VMEM((PEER_EB // N_KEYS, N_KEYS, PEER_TT), F32),
                        pltpu.VMEM((PEER_EB, PEER_TT), BF16),
                        pltpu.VMEM((D_MODEL, PEER_TT), F32)],
        compiler_params=_params(("parallel", "arbitrary")),
    )(flags, xm, u_b, vt_b, *routing, x1, mods, ln_g, ln_b)


def _dft_tables(length):
    n = np.arange(length)
    ang = 2.0 * np.pi * ((n[:, None] * n[None, :]) % length) / length
    dl = np.concatenate([np.cos(ang), -np.sin(ang)], axis=1) / math.sqrt(length)
    c = np.arange(FFT_GROUP_CH)
    angc = 2.0 * np.pi * ((c[:, None] * c[None, :]) % FFT_GROUP_CH) / FFT_GROUP_CH
    eye = np.eye(FFT_GROUPS)
    dc = np.concatenate([np.kron(eye, np.cos(angc)), np.kron(eye, np.sin(angc))], axis=1)
    dc = dc / math.sqrt(FFT_GROUP_CH)
    return jnp.asarray(dl, BF16), jnp.asarray(dc, BF16)


def _rope_tables(length, nheads):
    t = np.arange(length)
    pos = np.stack([t // GRID_W, t % GRID_W], axis=1).astype(np.float32)
    n_freq = HEAD_DIM // 4
    inv = (1.0 / (ROPE_BASE ** (np.arange(n_freq, dtype=np.float32) / n_freq))).astype(np.float32)
    ang = pos[:, :, None] * inv[None, None, :]
    cos = np.repeat(np.cos(ang)[:, :, None, :], 2, axis=2).reshape(length, HEAD_DIM)
    sin = np.sin(ang)
    sin = np.stack([-sin, sin], axis=2).reshape(length, HEAD_DIM)
    return (jnp.asarray(np.tile(cos, (1, nheads)), F32), jnp.asarray(np.tile(sin, (1, nheads)), F32))


def kernel(x_prompt, x_sample, cache_k, cache_v, state_ssm_re, state_ssm_im, c, c_ctx, w_ada, b_ada, w_in, conv_w, ssm_lam_re, ssm_lam_im, ssm_log_step, ssm_b_re, ssm_b_im, ssm_c_re, ssm_c_im, ssm_d, ssm_w_glu, attn_sink, w_branch, w_out, ln1_g, ln1_b, ln2_g, ln2_b, peer_wq, peer_subkeys, peer_u, peer_v):
    nb, seq, _ = x_prompt.shape
    nd, lseq, _ = x_sample.shape
    depth = w_in.shape[0]
    t_ctx = nb * seq
    t_all = t_ctx + nd * lseq
    assert t_ctx % lseq == 0 and t_all % PEER_TT == 0 and (2 * seq) % PEER_TT == 0
    assert lseq % SSM_TCHUNK == 0 and seq == SSM_TCHUNK

    x = jnp.concatenate([x_prompt.reshape(t_ctx, D_MODEL), x_sample.reshape(nd * lseq, D_MODEL)], axis=0)

    nrow = -(-(1 + nd) // 8) * 8
    cvecs = jnp.concatenate([c_ctx[None, :], c, jnp.zeros((nrow - 1 - nd, D_MODEL), F32)], axis=0)
    mods_all = _modulation(cvecs, w_ada, b_ada).reshape(depth, nrow, 6, D_MODEL)

    def mod_row_for(tile):
        nctx = t_ctx // tile
        per = lseq // tile
        return lambda i: jnp.where(i < nctx, 0, 1 + (i - nctx) // per)

    mod_row = mod_row_for(ROW_TILE)
    mod_row_tt = mod_row_for(PEER_TT)

    dl_ctx, dft_c = _dft_tables(seq)
    dl_lat, _ = _dft_tables(lseq)
    cosq, sinq = _rope_tables(lseq, N_HEADS)
    cosk, sin_k = _rope_tables(lseq, N_KV)

    gate0 = sum((512,) * 6) + 2 * N_KV * HEAD_DIM
    w_in_b = jnp.concatenate([w_in[:, :, gate0:], w_in[:, :, :gate0]], axis=2).astype(BF16)
    w_glu_b = ssm_w_glu.astype(BF16)
    w_branch_b = w_branch.astype(BF16)
    w_out_b = w_out.astype(BF16)
    wq_t = peer_wq.transpose(0, 2, 1).astype(BF16)
    keys = peer_subkeys.reshape(depth, 2 * PEER_HEADS, N_KEYS, KEY_DIM // 2).astype(BF16)
    u_b = peer_u.astype(BF16)
    vt_b = peer_v.transpose(0, 2, 1).astype(BF16)
    sp = _s5_params(ssm_lam_re, ssm_lam_im, ssm_log_step, ssm_b_re, ssm_b_im, ssm_c_re, ssm_c_im)
    sink = attn_sink.reshape(depth, 1, N_HEADS)
    ssm_d3 = ssm_d.reshape(depth, 1, -1)
    ln1_g3, ln1_b3 = ln1_g.reshape(depth, 1, -1), ln1_b.reshape(depth, 1, -1)
    ln2_g3, ln2_b3 = ln2_g.reshape(depth, 1, -1), ln2_b.reshape(depth, 1, -1)
    ck = cache_k.reshape(nd, depth, -1, N_KV * HEAD_DIM)
    cv = cache_v.reshape(nd, depth, -1, N_KV * HEAD_DIM)
    nbp_c = -(-nb // SSM_BROWS) * SSM_BROWS
    nbp_l = -(-nd // SSM_BROWS) * SSM_BROWS
    zero_state = jnp.zeros((SSM_GROUPS // SSM_GBLK, nbp_c, 2 * SSM_GBLK * SSM_STATE), F32)

    new_k, new_v, new_re, new_im = [], [], [], []
    for l in range(depth):
        z = _in_proj(x, mods_all, w_in_b, mod_row_for(IN_TILE), l)

        yf, yc, ya = _mixer_ctx(z, nb, seq, conv_w, sink, dl_ctx, dft_c, l)
        yf, yc = _fftconv_lat(z, t_ctx, nd, lseq, conv_w, dl_lat, dft_c, yf, yc, l)
        ya = _attn_lat(z, t_ctx, nd, lseq, ck, cv, cosq, sinq, cosk, sin_k, sink, ya, l)

        zs = z[:, ZS_BLK * 512:(ZS_BLK + 1) * 512]
        u_c, _ = _to_time_major(zs[:t_ctx], nb, seq)
        ysf_c, ysb_c, hf_c, hb_c = _s5(u_c, zero_state, zero_state, sp, nbp_c // SSM_BROWS,
                                       seq // SSM_TCHUNK, l)
        u_l, _ = _to_time_major(zs[t_ctx:], nd, lseq)
        h0f = _state_to_blocks(state_ssm_re[:, l, 0], state_ssm_im[:, l, 0], nbp_l)
        h0b = _state_to_blocks(state_ssm_re[:, l, 1], state_ssm_im[:, l, 1], nbp_l)
        ysf_l, ysb_l, _, _ = _s5(u_l, h0f, h0b, sp, nbp_l // SSM_BROWS, lseq // SSM_TCHUNK, l)
        ysf = jnp.concatenate([_from_time_major(ysf_c, nb, nbp_c, seq), _from_time_major(ysf_l, nd, nbp_l, lseq)], axis=0)
        ysb = jnp.concatenate([_from_time_major(ysb_c, nb, nbp_c, seq), _from_time_major(ysb_l, nd, nbp_l, lseq)], axis=0)

        x1, xm2 = _merge(x, mods_all, mod_row, yf, yc, ya, ysf, ysb, z,
                         ssm_d3, w_glu_b, w_branch_b, w_out_b, ln1_g3, ln1_b3, l)

        routing = _route(xm2, wq_t, keys, l)
        x = _peer(xm2, u_b, vt_b, routing, x1, mods_all, mod_row_tt, ln2_g3, ln2_b3, l)

        kv = z[:t_ctx, ZK_BLK * 128:(ZV_BLK + 1) * 128].reshape(nb, seq, 2, N_KV, HEAD_DIM)
        new_k.append(kv[:, :, 0])
        new_v.append(kv[:, :, 1])
        fre, fim = _blocks_to_state(hf_c, nb)
        bre, bim = _blocks_to_state(hb_c, nb)
        new_re.append(jnp.stack([fre, bre], axis=1))
        new_im.append(jnp.stack([fim, bim], axis=1))

    return (x[:t_ctx].reshape(nb, seq, D_MODEL), x[t_ctx:].reshape(nd, lseq, D_MODEL),
            jnp.stack(new_k, axis=1), jnp.stack(new_v, axis=1),
            jnp.stack(new_re, axis=1), jnp.stack(new_im, axis=1))
```

```python
import functools
import math

import numpy as np
import jax
import jax.numpy as jnp
from jax import lax
from jax.experimental import pallas as pl
from jax.experimental.pallas import tpu as pltpu

F32 = jnp.float32
BF16 = jnp.bfloat16

D_MODEL = 1024
GRID_W = 64
N_BRANCH = 4
BRANCH_WIDTH = 512
FFT_GROUPS = 4
FFT_GROUP_CH = 128
CONV_K = 3
SSM_GROUPS = 32
SSM_CH = 16
SSM_STATE = 64
N_HEADS = 8
N_KV = 2
Q_PER_KV = N_HEADS // N_KV
HEAD_DIM = 64
WINDOW = 128
ATT_BLOCK = 128
ROPE_BASE = 10000.0
PEER_HEADS = 8
N_KEYS = 128
N_EXPERTS = N_KEYS * N_KEYS
PEER_TOPK = 16
KEY_DIM = 256
LN_EPS = 1e-5
NEG_INF = -1e30

Z_COLS = N_BRANCH * D_MODEL + 6 * BRANCH_WIDTH + 2 * N_KV * HEAD_DIM
ZG_BLK = 0
ZF_BLK, ZB_BLK, ZC_BLK, ZH_BLK, ZS_BLK, ZQ_BLK = 8, 9, 10, 11, 12, 13
ZK_BLK, ZV_BLK = 56, 57

V7X_VMEM_LIMIT_BYTES = 56 * 1024 * 1024
SSM_GBLK = 8
SSM_TCHUNK = 256
SSM_BROWS = 8
MIX_BATCH = 2
ROW_TILE = 256
IN_TILE = 512
PEER_TT = 512
PEER_EB = 2048
ROUTE_TT = 256


def _params(sem):
    return pltpu.CompilerParams(dimension_semantics=sem, vmem_limit_bytes=V7X_VMEM_LIMIT_BYTES)


def _gelu(x):
    return 0.5 * x * (1.0 + jnp.tanh(0.7978845608028654 * (x + 0.044715 * (x * x * x))))


def _layer_norm(h, g, b):
    mu = jnp.mean(h, axis=-1, keepdims=True)
    hc = h - mu
    var = jnp.mean(hc * hc, axis=-1, keepdims=True)
    return hc * lax.rsqrt(var + LN_EPS) * g + b


def _mod_kernel(c_ref, w_ref, b_ref, o_ref):
    cv = c_ref[...]
    s = (cv * jax.nn.sigmoid(cv)).astype(BF16)
    o_ref[0] = jnp.dot(s, w_ref[0].astype(BF16), preferred_element_type=F32) + b_ref[0]


def _modulation(cvecs, w_ada, b_ada):
    depth = w_ada.shape[0]
    nrow = cvecs.shape[0]
    return pl.pallas_call(
        _mod_kernel,
        grid=(depth, 6),
        in_specs=[
            pl.BlockSpec((nrow, D_MODEL), lambda l, j: (0, 0)),
            pl.BlockSpec((1, D_MODEL, D_MODEL), lambda l, j: (l, 0, j)),
            pl.BlockSpec((1, 1, D_MODEL), lambda l, j: (l, 0, j)),
        ],
        out_specs=pl.BlockSpec((1, nrow, D_MODEL), lambda l, j: (l, 0, j)),
        out_shape=jax.ShapeDtypeStruct((depth, nrow, 6 * D_MODEL), F32),
        compiler_params=_params(("parallel", "parallel")),
    )(cvecs, w_ada, b_ada.reshape(depth, 1, 6 * D_MODEL))


def _win_kernel(x_ref, mod_ref, w_ref, z_ref):
    sh = mod_ref[0, 0:1, :]
    sc = mod_ref[0, 1:2, :]
    xm = (x_ref[...] * (1.0 + sc) + sh).astype(BF16)
    z_ref[...] = jnp.dot(xm, w_ref[...], preferred_element_type=F32)


def _in_proj(x, mods, w_in, mod_row, l):
    t = x.shape[0]
    ncol = Z_COLS // 2
    return pl.pallas_call(
        _win_kernel,
        grid=(2, t // IN_TILE),
        in_specs=[
            pl.BlockSpec((IN_TILE, D_MODEL), lambda c, i: (i, 0)),
            pl.BlockSpec((None, 1, 6, D_MODEL), lambda c, i: (l, mod_row(i), 0, 0)),
            pl.BlockSpec((None, D_MODEL, ncol), lambda c, i: (l, 0, c)),
        ],
        out_specs=pl.BlockSpec((IN_TILE, ncol), lambda c, i: (i, c)),
        out_shape=jax.ShapeDtypeStruct((t, Z_COLS), F32),
        compiler_params=_params(("parallel", "parallel")),
    )(x, mods, w_in)


def _fft_conv(zf_ref, zb_ref, zc_ref, zh_ref, cw_ref, dl_ref, dc_ref, yf_ref, yc_ref):
    length = zf_ref.shape[0]
    zf = zf_ref[...].astype(BF16)
    ab = jnp.dot(zf, dc_ref[...], preferred_element_type=F32)
    ab = jnp.concatenate([ab[:, :BRANCH_WIDTH], ab[:, BRANCH_WIDTH:]], axis=0).astype(BF16)
    yf_ref[...] = jnp.dot(dl_ref[...], ab, preferred_element_type=F32).astype(BF16)
    g = zc_ref[...] * zh_ref[...]
    row = lax.broadcasted_iota(jnp.int32, g.shape, 0)
    prev = jnp.where(row == 0, 0.0, pltpu.roll(g, 1, 0))
    nxt = jnp.where(row == length - 1, 0.0, pltpu.roll(g, length - 1, 0))
    conv = cw_ref[0:1, :] * prev + cw_ref[1:2, :] * g + cw_ref[2:3, :] * nxt
    yc_ref[...] = (zb_ref[...] * conv).astype(BF16)


def _softmax_pv(s, sink, v):
    m = jnp.maximum(jnp.max(s, axis=1, keepdims=True), sink)
    p = jnp.exp(s - m)
    den = jnp.sum(p, axis=1, keepdims=True) + jnp.exp(sink - m)
    return jnp.dot(p.astype(BF16), v, preferred_element_type=F32) / den


def _mixer_ctx_kernel(zf_ref, zb_ref, zc_ref, zh_ref, zq_ref, zk_ref, zv_ref, cw_ref, sink_ref,
                      dl_ref, dc_ref, yf_in, yc_in, ya_in, yf_ref, yc_ref, ya_ref):
    del yf_in, yc_in, ya_in
    seq = dl_ref.shape[0]
    for b in range(zf_ref.shape[0] // seq):
        rows = pl.ds(b * seq, seq)
        _fft_conv(zf_ref.at[rows], zb_ref.at[rows], zc_ref.at[rows], zh_ref.at[rows], cw_ref, dl_ref,
                  dc_ref, yf_ref.at[rows], yc_ref.at[rows])
        q = zq_ref[rows, :] * (HEAD_DIM ** -0.5)
        k = zk_ref[rows, :]
        v = zv_ref[rows, :]
        outs = []
        for h in range(N_HEADS):
            g = h // Q_PER_KV
            qh = q[:, h * HEAD_DIM:(h + 1) * HEAD_DIM].astype(BF16)
            kg = k[:, g * HEAD_DIM:(g + 1) * HEAD_DIM].astype(BF16)
            vg = v[:, g * HEAD_DIM:(g + 1) * HEAD_DIM].astype(BF16)
            s = lax.dot_general(qh, kg, (((1,), (1,)), ((), ())), preferred_element_type=F32)
            outs.append(_softmax_pv(s, sink_ref[0:1, h:h + 1], vg))
        ya_ref[rows, :] = jnp.concatenate(outs, axis=1).astype(BF16)


def _zspec(rows, width, row_fn, col_blk):
    return pl.BlockSpec((rows, width), lambda *a: (row_fn(*a), col_blk))


def _mixer_ctx(z, nb, seq, conv_w, sink, dft_l, dft_c, l):
    rf = lambda b: b
    full = lambda shape: pl.BlockSpec(shape, lambda b: (0,) * len(shape))
    layer = lambda shape: pl.BlockSpec((None,) + shape, lambda b: (l,) + (0,) * len(shape))
    out = jax.ShapeDtypeStruct((z.shape[0], BRANCH_WIDTH), BF16)
    rows = MIX_BATCH * seq
    ospec = pl.BlockSpec((rows, BRANCH_WIDTH), lambda b: (b, 0))
    anyspec = pl.BlockSpec(memory_space=pl.ANY)
    zeros = [jnp.zeros(out.shape, BF16) for _ in range(3)]
    return pl.pallas_call(
        _mixer_ctx_kernel,
        grid=(nb // MIX_BATCH,),
        in_specs=[
            _zspec(rows, 512, rf, ZF_BLK), _zspec(rows, 512, rf, ZB_BLK), _zspec(rows, 512, rf, ZC_BLK),
            _zspec(rows, 512, rf, ZH_BLK), _zspec(rows, 512, rf, ZQ_BLK),
            _zspec(rows, 128, rf, ZK_BLK), _zspec(rows, 128, rf, ZV_BLK),
            layer((CONV_K, BRANCH_WIDTH)), layer((1, N_HEADS)),
            full((seq, 2 * seq)), full((BRANCH_WIDTH, 2 * BRANCH_WIDTH)),
            anyspec, anyspec, anyspec,
        ],
        out_specs=[ospec, ospec, ospec],
        out_shape=[out, out, out],
        input_output_aliases={11: 0, 12: 1, 13: 2},
        compiler_params=_params(("parallel",)),
    )(z, z, z, z, z, z, z, conv_w, sink, dft_l, dft_c, *zeros)


def _fftconv_lat_kernel(zf_ref, zb_ref, zc_ref, zh_ref, cw_ref, dl_ref, dc_ref, yf_in, yc_in,
                        yf_ref, yc_ref):
    del yf_in, yc_in
    _fft_conv(zf_ref, zb_ref, zc_ref, zh_ref, cw_ref, dl_ref, dc_ref, yf_ref, yc_ref)


def _fftconv_lat(z, row0, nb, seq, conv_w, dft_l, dft_c, yf, yc, l):
    rf = lambda b: row0 // seq + b
    full = lambda shape: pl.BlockSpec(shape, lambda b: (0,) * len(shape))
    out = jax.ShapeDtypeStruct(yf.shape, BF16)
    ospec = pl.BlockSpec((seq, BRANCH_WIDTH), lambda b: (row0 // seq + b, 0))
    anyspec = pl.BlockSpec(memory_space=pl.ANY)
    return pl.pallas_call(
        _fftconv_lat_kernel,
        grid=(nb,),
        in_specs=[
            _zspec(seq, 512, rf, ZF_BLK), _zspec(seq, 512, rf, ZB_BLK), _zspec(seq, 512, rf, ZC_BLK),
            _zspec(seq, 512, rf, ZH_BLK),
            pl.BlockSpec((None, CONV_K, BRANCH_WIDTH), lambda b: (l, 0, 0)),
            full((seq, 2 * seq)), full((BRANCH_WIDTH, 2 * BRANCH_WIDTH)),
            anyspec, anyspec,
        ],
        out_specs=[ospec, ospec],
        out_shape=[out, out],
        input_output_aliases={7: 0, 8: 1},
        compiler_params=_params(("parallel",)),
    )(z, z, z, z, conv_w, dft_l, dft_c, yf, yc)


def _rope(x, cos, sin):
    lane = lax.broadcasted_iota(jnp.int32, (x.shape[0], 128), 1)
    first = (lane & 31) < 16
    parts = []
    for c in range(x.shape[1] // 128):
        xc = x[:, c * 128:(c + 1) * 128]
        swapped = jnp.where(first, pltpu.roll(xc, 112, 1), pltpu.roll(xc, 16, 1))
        parts.append(xc * cos[:, c * 128:(c + 1) * 128] + swapped * sin[:, c * 128:(c + 1) * 128])
    return parts[0] if len(parts) == 1 else jnp.concatenate(parts, axis=1)


def _attn_lat_kernel(zq_ref, zk_ref, zv_ref, ck_ref, cv_ref, cosq_ref, sinq_ref, cosk_ref, sinkk_ref,
                     sink_ref, ya_in, ya_ref):
    del ya_in
    n = pl.program_id(1)
    nblk = pl.num_programs(1)
    nwin = 3 * ATT_BLOCK
    q = _rope(zq_ref[...], cosq_ref[...], sinq_ref[...]) * (HEAD_DIM ** -0.5)
    ws = pl.multiple_of(jnp.clip(n - 1, 0, nblk - 3) * ATT_BLOCK, ATT_BLOCK)
    kw = _rope(zk_ref[pl.ds(ws, nwin), :], cosk_ref[pl.ds(ws, nwin), :], sinkk_ref[pl.ds(ws, nwin), :])
    vw = zv_ref[pl.ds(ws, nwin), :]
    k_all = jnp.concatenate([kw, ck_ref[...]], axis=0)
    v_all = jnp.concatenate([vw, cv_ref[...]], axis=0)
    nkey = k_all.shape[0]
    qpos = n * ATT_BLOCK + lax.broadcasted_iota(jnp.int32, (ATT_BLOCK, nkey), 0)
    col = lax.broadcasted_iota(jnp.int32, (ATT_BLOCK, nkey), 1)
    valid = (jnp.abs(qpos - (ws + col)) <= WINDOW) | (col >= nwin)
    outs = []
    for h in range(N_HEADS):
        g = h // Q_PER_KV
        qh = q[:, h * HEAD_DIM:(h + 1) * HEAD_DIM].astype(BF16)
        kg = k_all[:, g * HEAD_DIM:(g + 1) * HEAD_DIM].astype(BF16)
        vg = v_all[:, g * HEAD_DIM:(g + 1) * HEAD_DIM].astype(BF16)
        s = lax.dot_general(qh, kg, (((1,), (1,)), ((), ())), preferred_element_type=F32)
        s = jnp.where(valid, s, NEG_INF)
        outs.append(_softmax_pv(s, sink_ref[0:1, h:h + 1], vg))
    ya_ref[...] = jnp.concatenate(outs, axis=1).astype(BF16)


def _attn_lat(z, row0, nb, seq, ck, cv, cosq, sinq, cosk, sin_k, sink, ya, l):
    nblk = seq // ATT_BLOCK
    kvw = N_KV * HEAD_DIM
    past = ck.shape[2]
    full = lambda shape: pl.BlockSpec(shape, lambda b, n: (0,) * len(shape))
    cache = pl.BlockSpec((None, None, past, kvw), lambda b, n: (b, l, 0, 0))
    return pl.pallas_call(
        _attn_lat_kernel,
        grid=(nb, nblk),
        in_specs=[
            pl.BlockSpec((ATT_BLOCK, 512), lambda b, n: (row0 // ATT_BLOCK + b * nblk + n, ZQ_BLK)),
            pl.BlockSpec((seq, kvw), lambda b, n: (row0 // seq + b, ZK_BLK)),
            pl.BlockSpec((seq, kvw), lambda b, n: (row0 // seq + b, ZV_BLK)),
            cache, cache,
            pl.BlockSpec((ATT_BLOCK, 512), lambda b, n: (n, 0)),
            pl.BlockSpec((ATT_BLOCK, 512), lambda b, n: (n, 0)),
            full((seq, kvw)), full((seq, kvw)),
            pl.BlockSpec((None, 1, N_HEADS), lambda b, n: (l, 0, 0)),
            pl.BlockSpec(memory_space=pl.ANY),
        ],
        out_specs=pl.BlockSpec((ATT_BLOCK, 512), lambda b, n: (row0 // ATT_BLOCK + b * nblk + n, 0)),
        out_shape=jax.ShapeDtypeStruct(ya.shape, BF16),
        input_output_aliases={10: 0},
        compiler_params=_params(("parallel", "parallel")),
    )(z, z, z, ck, cv, cosq, sinq, cosk, sin_k, sink, ya)


def _s5_kernel(uf_ref, ub_ref, h0f_ref, h0b_ref, wbf_ref, wbb_ref, cf_ref, cb_ref, af_ref, ab_ref,
               yf_ref, yb_ref, hf_ref, hb_ref, buff, bufb, hst):
    c = pl.program_id(2)
    half = SSM_GBLK * SSM_STATE
    steps = uf_ref.shape[0] // SSM_BROWS

    @pl.when(c == 0)
    def _():
        hst[0] = h0f_ref[0]
        hst[1] = h0b_ref[0]

    buff[...] = jnp.dot(uf_ref[...].astype(BF16), wbf_ref[...], preferred_element_type=F32)
    bufb[...] = jnp.dot(ub_ref[...].astype(BF16), wbb_ref[...], preferred_element_type=F32)
    afr = jnp.broadcast_to(af_ref[0:1, :], (SSM_BROWS, half))
    afi = jnp.broadcast_to(af_ref[1:2, :], (SSM_BROWS, half))
    abr = jnp.broadcast_to(ab_ref[0:1, :], (SSM_BROWS, half))
    abi = jnp.broadcast_to(ab_ref[1:2, :], (SSM_BROWS, half))

    def step(t, carry):
        hfr, hfi, hbr, hbi = carry
        rf = pl.multiple_of(t * SSM_BROWS, SSM_BROWS)
        nfr = afr * hfr - afi * hfi + buff[pl.ds(rf, SSM_BROWS), 0:half]
        nfi = afr * hfi + afi * hfr + buff[pl.ds(rf, SSM_BROWS), half:2 * half]
        buff[pl.ds(rf, SSM_BROWS), 0:half] = nfr
        buff[pl.ds(rf, SSM_BROWS), half:2 * half] = nfi
        rb = pl.multiple_of((steps - 1 - t) * SSM_BROWS, SSM_BROWS)
        nbr = abr * hbr - abi * hbi + bufb[pl.ds(rb, SSM_BROWS), 0:half]
        nbi = abr * hbi + abi * hbr + bufb[pl.ds(rb, SSM_BROWS), half:2 * half]
        bufb[pl.ds(rb, SSM_BROWS), 0:half] = nbr
        bufb[pl.ds(rb, SSM_BROWS), half:2 * half] = nbi
        return nfr, nfi, nbr, nbi

    init = (hst[0, :, 0:half], hst[0, :, half:2 * half], hst[1, :, 0:half], hst[1, :, half:2 * half])
    hfr, hfi, hbr, hbi = lax.fori_loop(0, steps, step, init, unroll=4)
    hst[0, :, 0:half] = hfr
    hst[0, :, half:2 * half] = hfi
    hst[1, :, 0:half] = hbr
    hst[1, :, half:2 * half] = hbi
    yf_ref[...] = jnp.dot(buff[...].astype(BF16), cf_ref[...], preferred_element_type=F32)
    yb_ref[...] = jnp.dot(bufb[...].astype(BF16), cb_ref[...], preferred_element_type=F32)

    @pl.when(c == pl.num_programs(2) - 1)
    def _():
        hf_ref[0] = hst[0]
        hb_ref[0] = hst[1]


def _s5(u_tm, h0f, h0b, sp, nbb, nchunk, l):
    rows = SSM_TCHUNK * SSM_BROWS
    ngb = SSM_GROUPS // SSM_GBLK
    width = 2 * SSM_GBLK * SSM_STATE
    nbrow = nbb * SSM_BROWS
    cw = SSM_GBLK * SSM_CH
    fwd = lambda bb, j, c: (bb * nchunk + c, j)
    bwd = lambda bb, j, c: (bb * nchunk + nchunk - 1 - c, j)
    par = lambda shape, d: pl.BlockSpec((None, None, None) + shape, lambda bb, j, c: (l, d, j, 0, 0))
    st = pl.BlockSpec((1, SSM_BROWS, width), lambda bb, j, c: (j, bb, 0))
    wb, cm, a = sp
    ysh = jax.ShapeDtypeStruct(u_tm.shape, F32)
    hsh = jax.ShapeDtypeStruct((ngb, nbrow, width), F32)
    return pl.pallas_call(
        _s5_kernel,
        grid=(nbb, ngb, nchunk),
        in_specs=[
            pl.BlockSpec((rows, cw), fwd), pl.BlockSpec((rows, cw), bwd), st, st,
            par((cw, width), 0), par((cw, width), 1), par((width, cw), 0), par((width, cw), 1),
            par((2, width // 2), 0), par((2, width // 2), 1),
        ],
        out_specs=[pl.BlockSpec((rows, cw), fwd), pl.BlockSpec((rows, cw), bwd), st, st],
        out_shape=[ysh, ysh, hsh, hsh],
        scratch_shapes=[pltpu.VMEM((rows, width), F32), pltpu.VMEM((rows, width), F32),
                        pltpu.VMEM((2, SSM_BROWS, width), F32)],
        compiler_params=_params(("parallel", "parallel", "arbitrary")),
    )(u_tm, u_tm, h0f, h0b, wb, wb, cm, cm, a, a)


def _s5_params(lam_re, lam_im, log_step, b_re, b_im, c_re, c_im):
    lead = lam_re.shape[:-2]
    dt = jnp.exp(log_step)[..., None]
    mag = jnp.exp(lam_re * dt)
    ar = mag * jnp.cos(lam_im * dt)
    ai = mag * jnp.sin(lam_im * dt)
    den = lam_re * lam_re + lam_im * lam_im
    kr = ((ar - 1.0) * lam_re + ai * lam_im) / den
    ki = (ai * lam_re - (ar - 1.0) * lam_im) / den
    bbr = kr[..., None] * b_re - ki[..., None] * b_im
    bbi = kr[..., None] * b_im + ki[..., None] * b_re
    ngb = SSM_GROUPS // SSM_GBLK
    eye = jnp.eye(SSM_GBLK, dtype=F32)

    def blockdiag_in(m):
        m = m.reshape(lead + (ngb, SSM_GBLK, SSM_STATE, SSM_CH))
        m = jnp.einsum("...jgph,gk->...jghkp", m, eye)
        return m.reshape(lead + (ngb, SSM_GBLK * SSM_CH, SSM_GBLK * SSM_STATE))

    def blockdiag_out(m):
        m = m.reshape(lead + (ngb, SSM_GBLK, SSM_CH, SSM_STATE))
        m = jnp.einsum("...jghp,gk->...jgpkh", m, eye)
        return m.reshape(lead + (ngb, SSM_GBLK * SSM_STATE, SSM_GBLK * SSM_CH))

    wb = jnp.concatenate([blockdiag_in(bbr), blockdiag_in(bbi)], axis=-1).astype(BF16)
    cm = jnp.concatenate([blockdiag_out(c_re), -blockdiag_out(c_im)], axis=-2).astype(BF16)
    a = jnp.stack([ar.reshape(lead + (ngb, -1)), ai.reshape(lead + (ngb, -1))], axis=-2)
    return wb, cm, a


def _to_time_major(u, nb, seq):
    cdim = u.shape[1]
    nbp = -(-nb // SSM_BROWS) * SSM_BROWS
    u = u.reshape(nb, seq, cdim)
    if nbp != nb:
        u = jnp.pad(u, ((0, nbp - nb), (0, 0), (0, 0)))
    u = u.reshape(nbp // SSM_BROWS, SSM_BROWS, seq, cdim).transpose(0, 2, 1, 3)
    return u.reshape(nbp * seq, cdim), nbp


def _from_time_major(y, nb, nbp, seq):
    cdim = y.shape[1]
    y = y.reshape(nbp // SSM_BROWS, seq, SSM_BROWS, cdim).transpose(0, 2, 1, 3)
    return y.reshape(nbp, seq, cdim)[:nb].reshape(nb * seq, cdim)


def _state_to_blocks(re, im, nbp):
    nb = re.shape[0]
    ngb = SSM_GROUPS // SSM_GBLK
    def blk(x):
        return x.reshape(nb, ngb, SSM_GBLK * SSM_STATE).transpose(1, 0, 2)
    h = jnp.concatenate([blk(re), blk(im)], axis=2)
    if nbp != nb:
        h = jnp.pad(h, ((0, 0), (0, nbp - nb), (0, 0)))
    return h


def _blocks_to_state(h, nb):
    half = SSM_GBLK * SSM_STATE
    def unblk(x):
        return x[:, :nb].transpose(1, 0, 2).reshape(nb, SSM_GROUPS, SSM_STATE)
    return unblk(h[:, :, :half]), unblk(h[:, :, half:])


def _merge_kernel(x_ref, mod_ref, yf_ref, yc_ref, ya_ref, ysf_ref, ysb_ref, zs_ref,
                  zg0_ref, zg1_ref, zg2_ref, zg3_ref, d_ref, wglu_ref, wb_ref, wout_ref,
                  g_ref, b_ref, x1_ref, xm_ref):
    ys = ysf_ref[...] + ysb_ref[...] + d_ref[...] * zs_ref[...]
    ys = _gelu(ys)
    yssm = ys * jax.nn.sigmoid(jnp.dot(ys.astype(BF16), wglu_ref[...], preferred_element_type=F32))
    acc = jax.nn.sigmoid(zg0_ref[...]) * jnp.dot(yf_ref[...], wb_ref[0], preferred_element_type=F32)
    acc += jax.nn.sigmoid(zg1_ref[...]) * jnp.dot(yc_ref[...], wb_ref[1], preferred_element_type=F32)
    acc += jax.nn.sigmoid(zg2_ref[...]) * jnp.dot(yssm.astype(BF16), wb_ref[2], preferred_element_type=F32)
    acc += jax.nn.sigmoid(zg3_ref[...]) * jnp.dot(ya_ref[...], wb_ref[3], preferred_element_type=F32)
    mix = jnp.dot(acc.astype(BF16), wout_ref[...], preferred_element_type=F32)
    alpha = (2 * 4) ** 0.25
    x1 = _layer_norm(alpha * x_ref[...] + mod_ref[0, 2:3, :] * mix, g_ref[...], b_ref[...])
    x1_ref[...] = x1
    xm_ref[...] = (x1 * (1.0 + mod_ref[0, 4:5, :]) + mod_ref[0, 3:4, :]).astype(BF16)


def _merge(x, mods, mod_row, yf, yc, ya, ysf, ysb, z, ssm_d, w_glu, w_branch, w_out, ln_g, ln_b, l):
    t = x.shape[0]
    rf = lambda i: i
    row = lambda w: pl.BlockSpec((ROW_TILE, w), lambda i: (i, 0))
    full = lambda shape: pl.BlockSpec((None,) + shape, lambda i: (l,) + (0,) * len(shape))
    return pl.pallas_call(
        _merge_kernel,
        grid=(t // ROW_TILE,),
        in_specs=[
            row(D_MODEL), pl.BlockSpec((None, 1, 6, D_MODEL), lambda i: (l, mod_row(i), 0, 0)),
            row(512), row(512), row(512), row(512), row(512),
            _zspec(ROW_TILE, 512, rf, ZS_BLK),
            _zspec(ROW_TILE, 1024, rf, 0), _zspec(ROW_TILE, 1024, rf, 1),
            _zspec(ROW_TILE, 1024, rf, 2), _zspec(ROW_TILE, 1024, rf, 3),
            full((1, 512)), full((512, 512)), full((N_BRANCH, 512, D_MODEL)), full((D_MODEL, D_MODEL)),
            full((1, D_MODEL)), full((1, D_MODEL)),
        ],
        out_specs=[row(D_MODEL), row(D_MODEL)],
        out_shape=[jax.ShapeDtypeStruct((t, D_MODEL), F32), jax.ShapeDtypeStruct((t, D_MODEL), BF16)],
        compiler_params=_params(("parallel",)),
    )(x, mods, yf, yc, ya, ysf, ysb, z, z, z, z, z, ssm_d, w_glu, w_branch, w_out, ln_g, ln_b)


def _top16(s):
    n, w = s.shape
    iota = lax.broadcasted_iota(jnp.int32, (n, w), 0).astype(F32)
    kio = lax.broadcasted_iota(jnp.int32, (PEER_TOPK, w), 0)

    def body(k, carry):
        work, rank, vals, _ = carry
        m = jnp.max(work, axis=0, keepdims=True)
        pos = jnp.min(jnp.where(work == m, iota, float(n)), axis=0, keepdims=True)
        hit = iota == pos
        rank = jnp.where(hit, lax.convert_element_type(k, F32), rank)
        work = jnp.where(hit, -jnp.inf, work)
        vals = jnp.where(kio == k, m, vals)
        return work, rank, vals, pos

    init = (s, jnp.full((n, w), 1e9, F32), jnp.zeros((PEER_TOPK, w), F32), jnp.zeros((1, w), F32))
    _, rank, vals, pos = lax.fori_loop(0, PEER_TOPK, body, init)
    return vals, rank, pos


def _bitonic_desc(xs, first_k):
    n = len(xs)
    k = first_k
    while k <= n:
        j = k // 2
        while j >= 1:
            for i in range(n):
                p = i ^ j
                if p > i:
                    hi, lo = jnp.maximum(xs[i], xs[p]), jnp.minimum(xs[i], xs[p])
                    xs[i], xs[p] = (hi, lo) if (i & k) == 0 else (lo, hi)
            j //= 2
        k *= 2
    return xs


def _sorted_top16(s):
    n = s.shape[0] // 8
    xs = _bitonic_desc([s[8 * a:8 * a + 8, :] for a in range(n)], 2)
    shift = 4
    while len(xs) < PEER_TOPK:
        ys = [pltpu.roll(x, shift, 0) for x in xs]
        xs = _bitonic_desc(xs + ys[::-1], 2 * len(xs))
        shift //= 2
    n = len(xs)
    while shift >= 1:
        ys = [pltpu.roll(x, shift, 0) for x in xs]
        xs = _bitonic_desc([jnp.maximum(xs[i], ys[n - 1 - i]) for i in range(n)], n)
        shift //= 2
    return jnp.concatenate([x[0:1, :] for x in xs], axis=0)


_STAIR = [(j, PEER_TOPK // (j + 1)) for j in range(PEER_TOPK)]
_STAIR_ROWS = 64


def _stair_candidates(v1, v2):
    w = v1.shape[1]
    rows = [v1[j:j + 1] + v2[0:k] for j, k in _STAIR]
    npad = _STAIR_ROWS - sum(k for _, k in _STAIR)
    return jnp.concatenate(rows + [jnp.full((npad, w), -jnp.inf, F32)], axis=0)


def _stair_positions(w):
    rows = [float(PEER_TOPK * j) + lax.broadcasted_iota(jnp.int32, (k, w), 0).astype(F32) for j, k in _STAIR]
    npad = _STAIR_ROWS - sum(k for _, k in _STAIR)
    return jnp.concatenate(rows + [jnp.full((npad, w), 1e9, F32)], axis=0)


def _next_up(x):
    b = lax.bitcast_convert_type(x, jnp.int32)
    up = jnp.where(x > 0.0, b + 1, jnp.where(x < 0.0, b - 1, jnp.int32(0x00800000)))
    return lax.bitcast_convert_type(up, F32)


def _route_kernel(xm_ref, wq_ref, keys_ref, s1m_ref, qrow_ref, e1_ref, s2m_ref, pb_ref, e2_ref,
                  thr_ref, qs):
    qs[...] = lax.dot_general(wq_ref[...], xm_ref[...], (((1,), (1,)), ((), ())),
                              preferred_element_type=F32)
    w = xm_ref.shape[0]

    def count(mask):
        return jnp.sum(mask.astype(F32), axis=0, keepdims=True)

    def emit(h, s1, s2, in1, in2, m1, m2, vc, qrow, pb, thr_up, bad, thr_low):
        z = jnp.sum(jnp.exp(vc - vc[0:1]), axis=0, keepdims=True)
        s1m_ref[h] = jnp.where(in1, s1, -jnp.inf)
        s2m_ref[h] = jnp.where(in2, s2, -jnp.inf)
        e1_ref[h] = jnp.where(in1, jnp.exp(s1 - m1), 0.0) / z
        e2_ref[h] = jnp.where(in2, jnp.exp(s2 - m2), 0.0)
        qrow_ref[h] = qrow
        pb_ref[h] = pb
        thr = vc[PEER_TOPK - 1:PEER_TOPK]
        thr_ref[h] = jnp.concatenate([thr, thr_up, bad, thr_low, jnp.zeros((4, w), F32)], axis=0)

    def head(h, carry):
        base = pl.multiple_of(h * KEY_DIM, KEY_DIM)
        q1 = qs[pl.ds(base, N_KEYS), :].astype(BF16)
        q2 = qs[pl.ds(base + N_KEYS, N_KEYS), :].astype(BF16)
        s1 = jnp.dot(keys_ref[2 * h], q1, preferred_element_type=F32)
        s2 = jnp.dot(keys_ref[2 * h + 1], q2, preferred_element_type=F32)

        v1 = _sorted_top16(s1)
        v2 = _sorted_top16(s2)
        in1 = s1 >= v1[PEER_TOPK - 1:PEER_TOPK]
        in2 = s2 >= v2[PEER_TOPK - 1:PEER_TOPK]
        cand = _stair_candidates(v1, v2)
        vc = _sorted_top16(cand)
        thr = vc[PEER_TOPK - 1:PEER_TOPK]
        zero = jnp.zeros((N_KEYS, w), F32)
        k = float(PEER_TOPK)
        bad = jnp.abs(count(in1) - k) + jnp.abs(count(in2) - k) + jnp.abs(count(cand >= thr) - k)
        top = lambda v: jnp.maximum(jnp.abs(v[0:1]), jnp.abs(v[PEER_TOPK - 1:PEER_TOPK]))
        delta = (top(v1) + top(v2)) * (2.0 ** -21)
        lo = thr - 2.0 * delta
        flag = bad
        for jrow in range(PEER_TOPK):
            csum = v1[jrow:jrow + 1] + v2
            flag = flag + count((csum < thr) & (csum >= lo))
        emit(h, s1, s2, in1, in2, v1[0:1], v2[0:1], vc, zero, zero, thr, flag, thr - delta)

        @pl.when(jnp.max(bad) > 0.0)
        def _():
            xv1, r1, _ = _top16(s1)
            xv2, r2, _ = _top16(s2)
            xcand = _stair_candidates(xv1, xv2)
            xvc, _, prow = _top16(xcand)
            riota = lax.broadcasted_iota(jnp.int32, xcand.shape, 0).astype(F32)
            pthr = jnp.sum(jnp.where(riota == prow, _stair_positions(w), 0.0), axis=0, keepdims=True)
            emit(h, s1, s2, r1 < 100.0, r2 < 100.0, xv1[0:1], xv2[0:1], xvc,
                 pthr - k * r1, r2, _next_up(xvc[PEER_TOPK - 1:PEER_TOPK]), flag, thr - delta)

        return carry

    lax.fori_loop(0, PEER_HEADS, head, 0)


def _route(xm, wq_t, keys, l):
    t = xm.shape[0]
    big = jax.ShapeDtypeStruct((PEER_HEADS, N_KEYS, t), F32)
    bspec = pl.BlockSpec((PEER_HEADS, N_KEYS, ROUTE_TT), lambda i: (0, 0, i))
    return pl.pallas_call(
        _route_kernel,
        grid=(t // ROUTE_TT,),
        in_specs=[
            pl.BlockSpec((ROUTE_TT, D_MODEL), lambda i: (i, 0)),
            pl.BlockSpec((None, PEER_HEADS * KEY_DIM, D_MODEL), lambda i: (l, 0, 0)),
            pl.BlockSpec((None, 2 * PEER_HEADS, N_KEYS, N_KEYS), lambda i: (l, 0, 0, 0)),
        ],
        out_specs=[bspec] * 6 + [pl.BlockSpec((PEER_HEADS, 8, ROUTE_TT), lambda i: (0, 0, i))],
        out_shape=[big] * 6 + [jax.ShapeDtypeStruct((PEER_HEADS, 8, t), F32)],
        scratch_shapes=[pltpu.VMEM((PEER_HEADS * KEY_DIM, ROUTE_TT), F32)],
        compiler_params=_params(("parallel",)),
    )(xm, wq_t, keys)


def _peer_kernel(flag_ref, xm_ref, u_ref, vt_ref, s1m_ref, qrow_ref, e1_ref, s2m_ref, pb_ref, e2_ref,
                 thr_ref, x1_ref, mod_ref, g_ref, b_ref, o_ref, ht, wacc, pt, acc):
    j = pl.program_id(1)
    nrow = PEER_EB // N_KEYS

    @pl.when(j == 0)
    def _():
        acc[...] = jnp.zeros_like(acc)

    rpass = 4
    nlg = PEER_TT // 128

    def rows_of(tile, r0):
        return jnp.stack([jnp.broadcast_to(tile[r:r + 1, :], (8, 128)) for r in range(r0, r0 + rpass)])

    def gate_work(hp, lg, exact):
        lanes = slice(lg * 128, (lg + 1) * 128)
        heads = (2 * hp, 2 * hp + 1)
        for rp in range(0, nrow, rpass):
            e1r = [rows_of(e1_ref[h, :, lanes], rp) for h in heads]
            if exact:
                s1r = [rows_of(s1m_ref[h, :, lanes], rp) for h in heads]
                qr = [rows_of(qrow_ref[h, :, lanes], rp) for h in heads]
            else:
                need = [thr_ref[h, 3:4, lanes] - rows_of(s1m_ref[h, :, lanes], rp) for h in heads]
            for v in range(N_KEYS // 8):
                sub = slice(v * 8, (v + 1) * 8)
                gate = None
                for i, h in enumerate(heads):
                    if exact:
                        first = pb_ref[h, sub, lanes][None] <= qr[i]
                        limit = jnp.where(first, thr_ref[h, 0:1, lanes], thr_ref[h, 1:2, lanes])
                        sel = s1r[i] + s2m_ref[h, sub, lanes][None] >= limit
                    else:
                        sel = s2m_ref[h, sub, lanes][None] >= need[i]
                    g = jnp.where(sel, e1r[i] * e2_ref[h, sub, lanes][None], 0.0)
                    gate = g if gate is None else gate + g
                if hp == 0:
                    wacc[rp:rp + rpass, sub, lanes] = gate
                else:
                    wacc[rp:rp + rpass, sub, lanes] += gate

    hfull = lax.dot_general(u_ref[...], xm_ref[...], (((1,), (1,)), ((), ())),
                            preferred_element_type=F32)
    for r in range(nrow):
        ht[r] = hfull[r * N_KEYS:(r + 1) * N_KEYS, :]
    for hp in range(PEER_HEADS // 2):
        for lg in range(nlg):
            row = pl.program_id(0) * nlg + lg
            needs_sum = (flag_ref[row, 2 * hp] + flag_ref[row, 2 * hp + 1]) > 0
            pl.when(needs_sum)(functools.partial(gate_work, hp, lg, True))
            pl.when(jnp.logical_not(needs_sum))(functools.partial(gate_work, hp, lg, False))
    for lg in range(PEER_TT // 128):
        lanes = slice(lg * 128, (lg + 1) * 128)
        for r in range(nrow):
            pt[r * N_KEYS:(r + 1) * N_KEYS, lanes] = (
                wacc[r, :, lanes] * _gelu(ht[r, :, lanes])).astype(BF16)
    acc[...] += jnp.dot(vt_ref[...], pt[...], preferred_element_type=F32)

    @pl.when(j == pl.num_programs(1) - 1)
    def _():
        alpha = (2 * 4) ** 0.25
        ff = acc[...].T
        o_ref[...] = _layer_norm(alpha * x1_ref[...] + mod_ref[0, 5:6, :] * ff, g_ref[...], b_ref[...])


def _peer(xm, u_b, vt_b, routing, x1, mods, mod_row_tt, ln_g, ln_b, l):
    t = xm.shape[0]
    once = pl.Buffered(1)
    tok = lambda w: pl.BlockSpec((PEER_TT, w), lambda i, j: (i, 0), pipeline_mode=once)
    rspec = pl.BlockSpec((PEER_HEADS, N_KEYS, PEER_TT), lambda i, j: (0, 0, i), pipeline_mode=once)
    rowspec = pl.BlockSpec((PEER_HEADS, PEER_EB // N_KEYS, PEER_TT), lambda i, j: (0, j, i))
    full = lambda shape: pl.BlockSpec((None,) + shape, lambda i, j: (l,) + (0,) * len(shape))
    nblk = N_EXPERTS // PEER_EB
    bad = routing[6][:, 2, :].reshape(PEER_HEADS, t // 128, 128)
    flags = (jnp.max(bad, axis=2) > 0.0).astype(jnp.int32).T
    return pl.pallas_call(
        _peer_kernel,
        grid=(t // PEER_TT, nblk),
        in_specs=[
            pl.BlockSpec(memory_space=pltpu.SMEM),
            tok(D_MODEL),
            pl.BlockSpec((None, PEER_EB, D_MODEL), lambda i, j: (l, j, 0)),
            pl.BlockSpec((None, D_MODEL, PEER_EB), lambda i, j: (l, 0, j)),
            rowspec, rowspec, rowspec, rspec, rspec, rspec,
            pl.BlockSpec((PEER_HEADS, 8, PEER_TT), lambda i, j: (0, 0, i)),
            tok(D_MODEL),
            pl.BlockSpec((None, 1, 6, D_MODEL), lambda i, j: (l, mod_row_tt(i), 0, 0)),
            full((1, D_MODEL)), full((1, D_MODEL)),
        ],
        out_specs=pl.BlockSpec((PEER_TT, D_MODEL), lambda i, j: (i, 0)),
        out_shape=jax.ShapeDtypeStruct((t, D_MODEL), F32),
        scratch_shapes=[pltpu.VMEM((PEER_EB // N_KEYS, N_KEYS, PEER_TT), F32),
                        pltpu.VMEM((PEER_EB // N_KEYS, N_KEYS, PEER_TT), F32),
                        pltpu.VMEM((PEER_EB, PEER_TT), BF16),
                        pltpu.VMEM((D_MODEL, PEER_TT), F32)],
        compiler_params=_params(("parallel", "arbitrary")),
    )(flags, xm, u_b, vt_b, *routing, x1, mods, ln_g, ln_b)


def _dft_tables(length):
    n = np.arange(length)
    ang = 2.0 * np.pi * ((n[:, None] * n[None, :]) % length) / length
    dl = np.concatenate([np.cos(ang), -np.sin(ang)], axis=1) / math.sqrt(length)
    c = np.arange(FFT_GROUP_CH)
    angc = 2.0 * np.pi * ((c[:, None] * c[None, :]) % FFT_GROUP_CH) / FFT_GROUP_CH
    eye = np.eye(FFT_GROUPS)
    dc = np.concatenate([np.kron(eye, np.cos(angc)), np.kron(eye, np.sin(angc))], axis=1)
    dc = dc / math.sqrt(FFT_GROUP_CH)
    return jnp.asarray(dl, BF16), jnp.asarray(dc, BF16)


def _rope_tables(length, nheads):
    t = np.arange(length)
    pos = np.stack([t // GRID_W, t % GRID_W], axis=1).astype(np.float32)
    n_freq = HEAD_DIM // 4
    inv = (1.0 / (ROPE_BASE ** (np.arange(n_freq, dtype=np.float32) / n_freq))).astype(np.float32)
    ang = pos[:, :, None] * inv[None, None, :]
    cos = np.repeat(np.cos(ang)[:, :, None, :], 2, axis=2).reshape(length, HEAD_DIM)
    sin = np.sin(ang)
    sin = np.stack([-sin, sin], axis=2).reshape(length, HEAD_DIM)
    return (jnp.asarray(np.tile(cos, (1, nheads)), F32), jnp.asarray(np.tile(sin, (1, nheads)), F32))


def kernel(x_prompt, x_sample, cache_k, cache_v, state_ssm_re, state_ssm_im, c, c_ctx, w_ada, b_ada, w_in, conv_w, ssm_lam_re, ssm_lam_im, ssm_log_step, ssm_b_re, ssm_b_im, ssm_c_re, ssm_c_im, ssm_d, ssm_w_glu, attn_sink, w_branch, w_out, ln1_g, ln1_b, ln2_g, ln2_b, peer_wq, peer_subkeys, peer_u, peer_v):
    nb, seq, _ = x_prompt.shape
    nd, lseq, _ = x_sample.shape
    depth = w_in.shape[0]
    t_ctx = nb * seq
    t_all = t_ctx + nd * lseq
    assert t_ctx % lseq == 0 and t_all % PEER_TT == 0 and (2 * seq) % PEER_TT == 0
    assert lseq % SSM_TCHUNK == 0 and seq == SSM_TCHUNK

    x = jnp.concatenate([x_prompt.reshape(t_ctx, D_MODEL), x_sample.reshape(nd * lseq, D_MODEL)], axis=0)

    nrow = -(-(1 + nd) // 8) * 8
    cvecs = jnp.concatenate([c_ctx[None, :], c, jnp.zeros((nrow - 1 - nd, D_MODEL), F32)], axis=0)
    mods_all = _modulation(cvecs, w_ada, b_ada).reshape(depth, nrow, 6, D_MODEL)

    def mod_row_for(tile):
        nctx = t_ctx // tile
        per = lseq // tile
        return lambda i: jnp.where(i < nctx, 0, 1 + (i - nctx) // per)

    mod_row = mod_row_for(ROW_TILE)
    mod_row_tt = mod_row_for(PEER_TT)

    dl_ctx, dft_c = _dft_tables(seq)
    dl_lat, _ = _dft_tables(lseq)
    cosq, sinq = _rope_tables(lseq, N_HEADS)
    cosk, sin_k = _rope_tables(lseq, N_KV)

    gate0 = sum((512,) * 6) + 2 * N_KV * HEAD_DIM
    w_in_b = jnp.concatenate([w_in[:, :, gate0:], w_in[:, :, :gate0]], axis=2).astype(BF16)
    w_glu_b = ssm_w_glu.astype(BF16)
    w_branch_b = w_branch.astype(BF16)
    w_out_b = w_out.astype(BF16)
    wq_t = peer_wq.transpose(0, 2, 1).astype(BF16)
    keys = peer_subkeys.reshape(depth, 2 * PEER_HEADS, N_KEYS, KEY_DIM // 2).astype(BF16)
    u_b = peer_u.astype(BF16)
    vt_b = peer_v.transpose(0, 2, 1).astype(BF16)
    sp = _s5_params(ssm_lam_re, ssm_lam_im, ssm_log_step, ssm_b_re, ssm_b_im, ssm_c_re, ssm_c_im)
    sink = attn_sink.reshape(depth, 1, N_HEADS)
    ssm_d3 = ssm_d.reshape(depth, 1, -1)
    ln1_g3, ln1_b3 = ln1_g.reshape(depth, 1, -1), ln1_b.reshape(depth, 1, -1)
    ln2_g3, ln2_b3 = ln2_g.reshape(depth, 1, -1), ln2_b.reshape(depth, 1, -1)
    ck = cache_k.reshape(nd, depth, -1, N_KV * HEAD_DIM)
    cv = cache_v.reshape(nd, depth, -1, N_KV * HEAD_DIM)
    nbp_c = -(-nb // SSM_BROWS) * SSM_BROWS
    nbp_l = -(-nd // SSM_BROWS) * SSM_BROWS
    zero_state = jnp.zeros((SSM_GROUPS // SSM_GBLK, nbp_c, 2 * SSM_GBLK * SSM_STATE), F32)

    new_k, new_v, new_re, new_im = [], [], [], []
    for l in range(depth):
        z = _in_proj(x, mods_all, w_in_b, mod_row_for(IN_TILE), l)

        yf, yc, ya = _mixer_ctx(z, nb, seq, conv_w, sink, dl_ctx, dft_c, l)
        yf, yc = _fftconv_lat(z, t_ctx, nd, lseq, conv_w, dl_lat, dft_c, yf, yc, l)
        ya = _attn_lat(z, t_ctx, nd, lseq, ck, cv, cosq, sinq, cosk, sin_k, sink, ya, l)

        zs = z[:, ZS_BLK * 512:(ZS_BLK + 1) * 512]
        u_c, _ = _to_time_major(zs[:t_ctx], nb, seq)
        ysf_c, ysb_c, hf_c, hb_c = _s5(u_c, zero_state, zero_state, sp, nbp_c // SSM_BROWS,
                                       seq // SSM_TCHUNK, l)
        u_l, _ = _to_time_major(zs[t_ctx:], nd, lseq)
        h0f = _state_to_blocks(state_ssm_re[:, l, 0], state_ssm_im[:, l, 0], nbp_l)
        h0b = _state_to_blocks(state_ssm_re[:, l, 1], state_ssm_im[:, l, 1], nbp_l)
        ysf_l, ysb_l, _, _ = _s5(u_l, h0f, h0b, sp, nbp_l // SSM_BROWS, lseq // SSM_TCHUNK, l)
        ysf = jnp.concatenate([_from_time_major(ysf_c, nb, nbp_c, seq), _from_time_major(ysf_l, nd, nbp_l, lseq)], axis=0)
        ysb = jnp.concatenate([_from_time_major(ysb_c, nb, nbp_c, seq), _from_time_major(ysb_l, nd, nbp_l, lseq)], axis=0)

        x1, xm2 = _merge(x, mods_all, mod_row, yf, yc, ya, ysf, ysb, z,
                         ssm_d3, w_glu_b, w_branch_b, w_out_b, ln1_g3, ln1_b3, l)

        routing = _route(xm2, wq_t, keys, l)
        x = _peer(xm2, u_b, vt_b, routing, x1, mods_all, mod_row_tt, ln2_g3, ln2_b3, l)

        kv = z[:t_ctx, ZK_BLK * 128:(ZV_BLK + 1) * 128].reshape(nb, seq, 2, N_KV, HEAD_DIM)
        new_k.append(kv[:, :, 0])
        new_v.append(kv[:, :, 1])
        fre, fim = _blocks_to_state(hf_c, nb)
        bre, bim = _blocks_to_state(hb_c, nb)
        new_re.append(jnp.stack([fre, bre], axis=1))
        new_im.append(jnp.stack([fim, bim], axis=1))

    return (x[:t_ctx].reshape(nb, seq, D_MODEL), x[t_ctx:].reshape(nd, lseq, D_MODEL),
            jnp.stack(new_k, axis=1), jnp.stack(new_v, axis=1),
            jnp.stack(new_re, axis=1), jnp.stack(new_im, axis=1))
```

```python
import functools
import math

import numpy as np
import jax
import jax.numpy as jnp
from jax import lax
from jax.experimental import pallas as pl
from jax.experimental.pallas import tpu as pltpu

F32 = jnp.float32
BF16 = jnp.bfloat16

D_MODEL = 1024
GRID_W = 64
N_BRANCH = 4
BRANCH_WIDTH = 512
FFT_GROUPS = 4
FFT_GROUP_CH = 128
CONV_K = 3
SSM_GROUPS = 32
SSM_CH = 16
SSM_STATE = 64
N_HEADS = 8
N_KV = 2
Q_PER_KV = N_HEADS // N_KV
HEAD_DIM = 64
WINDOW = 128
ATT_BLOCK = 128
ROPE_BASE = 10000.0
PEER_HEADS = 8
N_KEYS = 128
N_EXPERTS = N_KEYS * N_KEYS
PEER_TOPK = 16
KEY_DIM = 256
LN_EPS = 1e-5
NEG_INF = -1e30

Z_COLS = N_BRANCH * D_MODEL + 6 * BRANCH_WIDTH + 2 * N_KV * HEAD_DIM
ZG_BLK = 0
ZF_BLK, ZB_BLK, ZC_BLK, ZH_BLK, ZS_BLK, ZQ_BLK = 8, 9, 10, 11, 12, 13
ZK_BLK, ZV_BLK = 56, 57

V7X_VMEM_LIMIT_BYTES = 56 * 1024 * 1024
SSM_GBLK = 8
SSM_TCHUNK = 256
SSM_BROWS = 8
MIX_BATCH = 2
ATT_QBLOCKS = 2
ROW_TILE = 256
IN_TILE = 512
PEER_TT = 512
PEER_EB = 2048
ROUTE_TT = 256


def _params(sem):
    return pltpu.CompilerParams(dimension_semantics=sem, vmem_limit_bytes=V7X_VMEM_LIMIT_BYTES)


def _gelu(x):
    return 0.5 * x * (1.0 + jnp.tanh(0.7978845608028654 * (x + 0.044715 * (x * x * x))))


def _layer_norm(h, g, b):
    mu = jnp.mean(h, axis=-1, keepdims=True)
    hc = h - mu
    var = jnp.mean(hc * hc, axis=-1, keepdims=True)
    return hc * lax.rsqrt(var + LN_EPS) * g + b


def _mod_kernel(c_ref, w_ref, b_ref, o_ref):
    cv = c_ref[...]
    s = (cv * jax.nn.sigmoid(cv)).astype(BF16)
    o_ref[0] = jnp.dot(s, w_ref[0].astype(BF16), preferred_element_type=F32) + b_ref[0]


def _modulation(cvecs, w_ada, b_ada):
    depth = w_ada.shape[0]
    nrow = cvecs.shape[0]
    return pl.pallas_call(
        _mod_kernel,
        grid=(depth, 6),
        in_specs=[
            pl.BlockSpec((nrow, D_MODEL), lambda l, j: (0, 0)),
            pl.BlockSpec((1, D_MODEL, D_MODEL), lambda l, j: (l, 0, j)),
            pl.BlockSpec((1, 1, D_MODEL), lambda l, j: (l, 0, j)),
        ],
        out_specs=pl.BlockSpec((1, nrow, D_MODEL), lambda l, j: (l, 0, j)),
        out_shape=jax.ShapeDtypeStruct((depth, nrow, 6 * D_MODEL), F32),
        compiler_params=_params(("parallel", "parallel")),
    )(cvecs, w_ada, b_ada.reshape(depth, 1, 6 * D_MODEL))


def _win_kernel(x_ref, mod_ref, w_ref, z_ref):
    sh = mod_ref[0, 0:1, :]
    sc = mod_ref[0, 1:2, :]
    xm = (x_ref[...] * (1.0 + sc) + sh).astype(BF16)
    z_ref[...] = jnp.dot(xm, w_ref[...], preferred_element_type=F32)


def _in_proj(x, mods, w_in, mod_row, l):
    t = x.shape[0]
    ncol = Z_COLS // 2
    return pl.pallas_call(
        _win_kernel,
        grid=(2, t // IN_TILE),
        in_specs=[
            pl.BlockSpec((IN_TILE, D_MODEL), lambda c, i: (i, 0)),
            pl.BlockSpec((None, 1, 6, D_MODEL), lambda c, i: (l, mod_row(i), 0, 0)),
            pl.BlockSpec((None, D_MODEL, ncol), lambda c, i: (l, 0, c)),
        ],
        out_specs=pl.BlockSpec((IN_TILE, ncol), lambda c, i: (i, c)),
        out_shape=jax.ShapeDtypeStruct((t, Z_COLS), F32),
        compiler_params=_params(("parallel", "parallel")),
    )(x, mods, w_in)


def _fft_conv(zf_ref, zb_ref, zc_ref, zh_ref, cw_ref, dl_ref, dc_ref, yf_ref, yc_ref):
    length = zf_ref.shape[0]
    zf = zf_ref[...].astype(BF16)
    ab = jnp.dot(zf, dc_ref[...], preferred_element_type=F32)
    ab = jnp.concatenate([ab[:, :BRANCH_WIDTH], ab[:, BRANCH_WIDTH:]], axis=0).astype(BF16)
    yf_ref[...] = jnp.dot(dl_ref[...], ab, preferred_element_type=F32).astype(BF16)
    g = zc_ref[...] * zh_ref[...]
    row = lax.broadcasted_iota(jnp.int32, g.shape, 0)
    prev = jnp.where(row == 0, 0.0, pltpu.roll(g, 1, 0))
    nxt = jnp.where(row == length - 1, 0.0, pltpu.roll(g, length - 1, 0))
    conv = cw_ref[0:1, :] * prev + cw_ref[1:2, :] * g + cw_ref[2:3, :] * nxt
    yc_ref[...] = (zb_ref[...] * conv).astype(BF16)


def _softmax_pv(s, sink, v):
    m = jnp.maximum(jnp.max(s, axis=1, keepdims=True), sink)
    p = jnp.exp(s - m)
    den = jnp.sum(p, axis=1, keepdims=True) + jnp.exp(sink - m)
    return jnp.dot(p.astype(BF16), v, preferred_element_type=F32) / den


def _mixer_ctx_kernel(zf_ref, zb_ref, zc_ref, zh_ref, zq_ref, zk_ref, zv_ref, cw_ref, sink_ref,
                      dl_ref, dc_ref, yf_in, yc_in, ya_in, yf_ref, yc_ref, ya_ref):
    del yf_in, yc_in, ya_in
    seq = dl_ref.shape[0]
    for b in range(zf_ref.shape[0] // seq):
        rows = pl.ds(b * seq, seq)
        _fft_conv(zf_ref.at[rows], zb_ref.at[rows], zc_ref.at[rows], zh_ref.at[rows], cw_ref, dl_ref,
                  dc_ref, yf_ref.at[rows], yc_ref.at[rows])
        q = zq_ref[rows, :] * (HEAD_DIM ** -0.5)
        k = zk_ref[rows, :]
        v = zv_ref[rows, :]
        outs = []
        for h in range(N_HEADS):
            g = h // Q_PER_KV
            qh = q[:, h * HEAD_DIM:(h + 1) * HEAD_DIM].astype(BF16)
            kg = k[:, g * HEAD_DIM:(g + 1) * HEAD_DIM].astype(BF16)
            vg = v[:, g * HEAD_DIM:(g + 1) * HEAD_DIM].astype(BF16)
            s = lax.dot_general(qh, kg, (((1,), (1,)), ((), ())), preferred_element_type=F32)
            outs.append(_softmax_pv(s, sink_ref[0:1, h:h + 1], vg))
        ya_ref[rows, :] = jnp.concatenate(outs, axis=1).astype(BF16)


def _zspec(rows, width, row_fn, col_blk):
    return pl.BlockSpec((rows, width), lambda *a: (row_fn(*a), col_blk))


def _mixer_ctx(z, nb, seq, conv_w, sink, dft_l, dft_c, l):
    rf = lambda b: b
    full = lambda shape: pl.BlockSpec(shape, lambda b: (0,) * len(shape))
    layer = lambda shape: pl.BlockSpec((None,) + shape, lambda b: (l,) + (0,) * len(shape))
    out = jax.ShapeDtypeStruct((z.shape[0], BRANCH_WIDTH), BF16)
    rows = MIX_BATCH * seq
    ospec = pl.BlockSpec((rows, BRANCH_WIDTH), lambda b: (b, 0))
    anyspec = pl.BlockSpec(memory_space=pl.ANY)
    zeros = [jnp.zeros(out.shape, BF16) for _ in range(3)]
    return pl.pallas_call(
        _mixer_ctx_kernel,
        grid=(nb // MIX_BATCH,),
        in_specs=[
            _zspec(rows, 512, rf, ZF_BLK), _zspec(rows, 512, rf, ZB_BLK), _zspec(rows, 512, rf, ZC_BLK),
            _zspec(rows, 512, rf, ZH_BLK), _zspec(rows, 512, rf, ZQ_BLK),
            _zspec(rows, 128, rf, ZK_BLK), _zspec(rows, 128, rf, ZV_BLK),
            layer((CONV_K, BRANCH_WIDTH)), layer((1, N_HEADS)),
            full((seq, 2 * seq)), full((BRANCH_WIDTH, 2 * BRANCH_WIDTH)),
            anyspec, anyspec, anyspec,
        ],
        out_specs=[ospec, ospec, ospec],
        out_shape=[out, out, out],
        input_output_aliases={11: 0, 12: 1, 13: 2},
        compiler_params=_params(("parallel",)),
    )(z, z, z, z, z, z, z, conv_w, sink, dft_l, dft_c, *zeros)


def _fftconv_lat_kernel(zf_ref, zb_ref, zc_ref, zh_ref, cw_ref, dl_ref, dc_ref, yf_in, yc_in,
                        yf_ref, yc_ref):
    del yf_in, yc_in
    _fft_conv(zf_ref, zb_ref, zc_ref, zh_ref, cw_ref, dl_ref, dc_ref, yf_ref, yc_ref)


def _fftconv_lat(z, row0, nb, seq, conv_w, dft_l, dft_c, yf, yc, l):
    rf = lambda b: row0 // seq + b
    full = lambda shape: pl.BlockSpec(shape, lambda b: (0,) * len(shape))
    out = jax.ShapeDtypeStruct(yf.shape, BF16)
    ospec = pl.BlockSpec((seq, BRANCH_WIDTH), lambda b: (row0 // seq + b, 0))
    anyspec = pl.BlockSpec(memory_space=pl.ANY)
    return pl.pallas_call(
        _fftconv_lat_kernel,
        grid=(nb,),
        in_specs=[
            _zspec(seq, 512, rf, ZF_BLK), _zspec(seq, 512, rf, ZB_BLK), _zspec(seq, 512, rf, ZC_BLK),
            _zspec(seq, 512, rf, ZH_BLK),
            pl.BlockSpec((None, CONV_K, BRANCH_WIDTH), lambda b: (l, 0, 0)),
            full((seq, 2 * seq)), full((BRANCH_WIDTH, 2 * BRANCH_WIDTH)),
            anyspec, anyspec,
        ],
        out_specs=[ospec, ospec],
        out_shape=[out, out],
        input_output_aliases={7: 0, 8: 1},
        compiler_params=_params(("parallel",)),
    )(z, z, z, z, conv_w, dft_l, dft_c, yf, yc)


def _rope(x, cos, sin):
    lane = lax.broadcasted_iota(jnp.int32, (x.shape[0], 128), 1)
    first = (lane & 31) < 16
    parts = []
    for c in range(x.shape[1] // 128):
        xc = x[:, c * 128:(c + 1) * 128]
        swapped = jnp.where(first, pltpu.roll(xc, 112, 1), pltpu.roll(xc, 16, 1))
        parts.append(xc * cos[:, c * 128:(c + 1) * 128] + swapped * sin[:, c * 128:(c + 1) * 128])
    return parts[0] if len(parts) == 1 else jnp.concatenate(parts, axis=1)


def _attn_lat_kernel(zq_ref, zk_ref, zv_ref, ck_ref, cv_ref, cosq_ref, sinq_ref, cosk_ref, sinkk_ref,
                     sink_ref, ya_in, ya_ref):
    del ya_in
    nblk = pl.num_programs(1) * ATT_QBLOCKS
    nwin = 3 * ATT_BLOCK
    for qi in range(ATT_QBLOCKS):
        n = pl.program_id(1) * ATT_QBLOCKS + qi
        qrows = pl.ds(qi * ATT_BLOCK, ATT_BLOCK)
        q = _rope(zq_ref[qrows, :], cosq_ref[qrows, :], sinq_ref[qrows, :]) * (HEAD_DIM ** -0.5)
        ws = pl.multiple_of(jnp.clip(n - 1, 0, nblk - 3) * ATT_BLOCK, ATT_BLOCK)
        kw = _rope(zk_ref[pl.ds(ws, nwin), :], cosk_ref[pl.ds(ws, nwin), :], sinkk_ref[pl.ds(ws, nwin), :])
        vw = zv_ref[pl.ds(ws, nwin), :]
        k_all = jnp.concatenate([kw, ck_ref[...]], axis=0)
        v_all = jnp.concatenate([vw, cv_ref[...]], axis=0)
        nkey = k_all.shape[0]
        qpos = n * ATT_BLOCK + lax.broadcasted_iota(jnp.int32, (ATT_BLOCK, nkey), 0)
        col = lax.broadcasted_iota(jnp.int32, (ATT_BLOCK, nkey), 1)
        valid = (jnp.abs(qpos - (ws + col)) <= WINDOW) | (col >= nwin)
        outs = []
        for h in range(N_HEADS):
            g = h // Q_PER_KV
            qh = q[:, h * HEAD_DIM:(h + 1) * HEAD_DIM].astype(BF16)
            kg = k_all[:, g * HEAD_DIM:(g + 1) * HEAD_DIM].astype(BF16)
            vg = v_all[:, g * HEAD_DIM:(g + 1) * HEAD_DIM].astype(BF16)
            s = lax.dot_general(qh, kg, (((1,), (1,)), ((), ())), preferred_element_type=F32)
            s = jnp.where(valid, s, NEG_INF)
            outs.append(_softmax_pv(s, sink_ref[0:1, h:h + 1], vg))
        ya_ref[qrows, :] = jnp.concatenate(outs, axis=1).astype(BF16)


def _attn_lat(z, row0, nb, seq, ck, cv, cosq, sinq, cosk, sin_k, sink, ya, l):
    qrows = ATT_QBLOCKS * ATT_BLOCK
    nblk = seq // qrows
    kvw = N_KV * HEAD_DIM
    past = ck.shape[2]
    full = lambda shape: pl.BlockSpec(shape, lambda b, n: (0,) * len(shape))
    cache = pl.BlockSpec((None, None, past, kvw), lambda b, n: (b, l, 0, 0))
    return pl.pallas_call(
        _attn_lat_kernel,
        grid=(nb, nblk),
        in_specs=[
            pl.BlockSpec((qrows, 512), lambda b, n: (row0 // qrows + b * nblk + n, ZQ_BLK)),
            pl.BlockSpec((seq, kvw), lambda b, n: (row0 // seq + b, ZK_BLK)),
            pl.BlockSpec((seq, kvw), lambda b, n: (row0 // seq + b, ZV_BLK)),
            cache, cache,
            pl.BlockSpec((qrows, 512), lambda b, n: (n, 0)),
            pl.BlockSpec((qrows, 512), lambda b, n: (n, 0)),
            full((seq, kvw)), full((seq, kvw)),
            pl.BlockSpec((None, 1, N_HEADS), lambda b, n: (l, 0, 0)),
            pl.BlockSpec(memory_space=pl.ANY),
        ],
        out_specs=pl.BlockSpec((qrows, 512), lambda b, n: (row0 // qrows + b * nblk + n, 0)),
        out_shape=jax.ShapeDtypeStruct(ya.shape, BF16),
        input_output_aliases={10: 0},
        compiler_params=_params(("parallel", "parallel")),
    )(z, z, z, ck, cv, cosq, sinq, cosk, sin_k, sink, ya)


def _s5_kernel(uf_ref, ub_ref, h0f_ref, h0b_ref, wbf_ref, wbb_ref, cf_ref, cb_ref, af_ref, ab_ref,
               yf_ref, yb_ref, hf_ref, hb_ref, buff, bufb, hst):
    c = pl.program_id(2)
    half = SSM_GBLK * SSM_STATE
    steps = uf_ref.shape[0] // SSM_BROWS

    @pl.when(c == 0)
    def _():
        hst[0] = h0f_ref[0]
        hst[1] = h0b_ref[0]

    buff[...] = jnp.dot(uf_ref[...].astype(BF16), wbf_ref[...], preferred_element_type=F32)
    bufb[...] = jnp.dot(ub_ref[...].astype(BF16), wbb_ref[...], preferred_element_type=F32)
    afr = jnp.broadcast_to(af_ref[0:1, :], (SSM_BROWS, half))
    afi = jnp.broadcast_to(af_ref[1:2, :], (SSM_BROWS, half))
    abr = jnp.broadcast_to(ab_ref[0:1, :], (SSM_BROWS, half))
    abi = jnp.broadcast_to(ab_ref[1:2, :], (SSM_BROWS, half))

    def step(t, carry):
        hfr, hfi, hbr, hbi = carry
        rf = pl.multiple_of(t * SSM_BROWS, SSM_BROWS)
        nfr = afr * hfr - afi * hfi + buff[pl.ds(rf, SSM_BROWS), 0:half]
        nfi = afr * hfi + afi * hfr + buff[pl.ds(rf, SSM_BROWS), half:2 * half]
        buff[pl.ds(rf, SSM_BROWS), 0:half] = nfr
        buff[pl.ds(rf, SSM_BROWS), half:2 * half] = nfi
        rb = pl.multiple_of((steps - 1 - t) * SSM_BROWS, SSM_BROWS)
        nbr = abr * hbr - abi * hbi + bufb[pl.ds(rb, SSM_BROWS), 0:half]
        nbi = abr * hbi + abi * hbr + bufb[pl.ds(rb, SSM_BROWS), half:2 * half]
        bufb[pl.ds(rb, SSM_BROWS), 0:half] = nbr
        bufb[pl.ds(rb, SSM_BROWS), half:2 * half] = nbi
        return nfr, nfi, nbr, nbi

    init = (hst[0, :, 0:half], hst[0, :, half:2 * half], hst[1, :, 0:half], hst[1, :, half:2 * half])
    hfr, hfi, hbr, hbi = lax.fori_loop(0, steps, step, init, unroll=4)
    hst[0, :, 0:half] = hfr
    hst[0, :, half:2 * half] = hfi
    hst[1, :, 0:half] = hbr
    hst[1, :, half:2 * half] = hbi
    yf_ref[...] = jnp.dot(buff[...].astype(BF16), cf_ref[...], preferred_element_type=F32)
    yb_ref[...] = jnp.dot(bufb[...].astype(BF16), cb_ref[...], preferred_element_type=F32)

    @pl.when(c == pl.num_programs(2) - 1)
    def _():
        hf_ref[0] = hst[0]
        hb_ref[0] = hst[1]


def _s5(u_tm, h0f, h0b, sp, nbb, nchunk, l):
    rows = SSM_TCHUNK * SSM_BROWS
    ngb = SSM_GROUPS // SSM_GBLK
    width = 2 * SSM_GBLK * SSM_STATE
    nbrow = nbb * SSM_BROWS
    cw = SSM_GBLK * SSM_CH
    fwd = lambda bb, j, c: (bb * nchunk + c, j)
    bwd = lambda bb, j, c: (bb * nchunk + nchunk - 1 - c, j)
    par = lambda shape, d: pl.BlockSpec((None, None, None) + shape, lambda bb, j, c: (l, d, j, 0, 0))
    st = pl.BlockSpec((1, SSM_BROWS, width), lambda bb, j, c: (j, bb, 0))
    wb, cm, a = sp
    ysh = jax.ShapeDtypeStruct(u_tm.shape, F32)
    hsh = jax.ShapeDtypeStruct((ngb, nbrow, width), F32)
    return pl.pallas_call(
        _s5_kernel,
        grid=(nbb, ngb, nchunk),
        in_specs=[
            pl.BlockSpec((rows, cw), fwd), pl.BlockSpec((rows, cw), bwd), st, st,
            par((cw, width), 0), par((cw, width), 1), par((width, cw), 0), par((width, cw), 1),
            par((2, width // 2), 0), par((2, width // 2), 1),
        ],
        out_specs=[pl.BlockSpec((rows, cw), fwd), pl.BlockSpec((rows, cw), bwd), st, st],
        out_shape=[ysh, ysh, hsh, hsh],
        scratch_shapes=[pltpu.VMEM((rows, width), F32), pltpu.VMEM((rows, width), F32),
                        pltpu.VMEM((2, SSM_BROWS, width), F32)],
        compiler_params=_params(("parallel", "parallel", "arbitrary")),
    )(u_tm, u_tm, h0f, h0b, wb, wb, cm, cm, a, a)


def _s5_params(lam_re, lam_im, log_step, b_re, b_im, c_re, c_im):
    lead = lam_re.shape[:-2]
    dt = jnp.exp(log_step)[..., None]
    mag = jnp.exp(lam_re * dt)
    ar = mag * jnp.cos(lam_im * dt)
    ai = mag * jnp.sin(lam_im * dt)
    den = lam_re * lam_re + lam_im * lam_im
    kr = ((ar - 1.0) * lam_re + ai * lam_im) / den
    ki = (ai * lam_re - (ar - 1.0) * lam_im) / den
    bbr = kr[..., None] * b_re - ki[..., None] * b_im
    bbi = kr[..., None] * b_im + ki[..., None] * b_re
    ngb = SSM_GROUPS // SSM_GBLK
    eye = jnp.eye(SSM_GBLK, dtype=F32)

    def blockdiag_in(m):
        m = m.reshape(lead + (ngb, SSM_GBLK, SSM_STATE, SSM_CH))
        m = jnp.einsum("...jgph,gk->...jghkp", m, eye)
        return m.reshape(lead + (ngb, SSM_GBLK * SSM_CH, SSM_GBLK * SSM_STATE))

    def blockdiag_out(m):
        m = m.reshape(lead + (ngb, SSM_GBLK, SSM_CH, SSM_STATE))
        m = jnp.einsum("...jghp,gk->...jgpkh", m, eye)
        return m.reshape(lead + (ngb, SSM_GBLK * SSM_STATE, SSM_GBLK * SSM_CH))

    wb = jnp.concatenate([blockdiag_in(bbr), blockdiag_in(bbi)], axis=-1).astype(BF16)
    cm = jnp.concatenate([blockdiag_out(c_re), -blockdiag_out(c_im)], axis=-2).astype(BF16)
    a = jnp.stack([ar.reshape(lead + (ngb, -1)), ai.reshape(lead + (ngb, -1))], axis=-2)
    return wb, cm, a


def _to_time_major(u, nb, seq):
    cdim = u.shape[1]
    nbp = -(-nb // SSM_BROWS) * SSM_BROWS
    u = u.reshape(nb, seq, cdim)
    if nbp != nb:
        u = jnp.pad(u, ((0, nbp - nb), (0, 0), (0, 0)))
    u = u.reshape(nbp // SSM_BROWS, SSM_BROWS, seq, cdim).transpose(0, 2, 1, 3)
    return u.reshape(nbp * seq, cdim), nbp


def _from_time_major(y, nb, nbp, seq):
    cdim = y.shape[1]
    y = y.reshape(nbp // SSM_BROWS, seq, SSM_BROWS, cdim).transpose(0, 2, 1, 3)
    return y.reshape(nbp, seq, cdim)[:nb].reshape(nb * seq, cdim)


def _state_to_blocks(re, im, nbp):
    nb = re.shape[0]
    ngb = SSM_GROUPS // SSM_GBLK
    def blk(x):
        return x.reshape(nb, ngb, SSM_GBLK * SSM_STATE).transpose(1, 0, 2)
    h = jnp.concatenate([blk(re), blk(im)], axis=2)
    if nbp != nb:
        h = jnp.pad(h, ((0, 0), (0, nbp - nb), (0, 0)))
    return h


def _blocks_to_state(h, nb):
    half = SSM_GBLK * SSM_STATE
    def unblk(x):
        return x[:, :nb].transpose(1, 0, 2).reshape(nb, SSM_GROUPS, SSM_STATE)
    return unblk(h[:, :, :half]), unblk(h[:, :, half:])


def _merge_kernel(x_ref, mod_ref, yf_ref, yc_ref, ya_ref, ysf_ref, ysb_ref, zs_ref,
                  zg0_ref, zg1_ref, zg2_ref, zg3_ref, d_ref, wglu_ref, wb_ref, wout_ref,
                  g_ref, b_ref, x1_ref, xm_ref):
    ys = ysf_ref[...] + ysb_ref[...] + d_ref[...] * zs_ref[...]
    ys = _gelu(ys)
    yssm = ys * jax.nn.sigmoid(jnp.dot(ys.astype(BF16), wglu_ref[...], preferred_element_type=F32))
    acc = jax.nn.sigmoid(zg0_ref[...]) * jnp.dot(yf_ref[...], wb_ref[0], preferred_element_type=F32)
    acc += jax.nn.sigmoid(zg1_ref[...]) * jnp.dot(yc_ref[...], wb_ref[1], preferred_element_type=F32)
    acc += jax.nn.sigmoid(zg2_ref[...]) * jnp.dot(yssm.astype(BF16), wb_ref[2], preferred_element_type=F32)
    acc += jax.nn.sigmoid(zg3_ref[...]) * jnp.dot(ya_ref[...], wb_ref[3], preferred_element_type=F32)
    mix = jnp.dot(acc.astype(BF16), wout_ref[...], preferred_element_type=F32)
    alpha = (2 * 4) ** 0.25
    x1 = _layer_norm(alpha * x_ref[...] + mod_ref[0, 2:3, :] * mix, g_ref[...], b_ref[...])
    x1_ref[...] = x1
    xm_ref[...] = (x1 * (1.0 + mod_ref[0, 4:5, :]) + mod_ref[0, 3:4, :]).astype(BF16)


def _merge(x, mods, mod_row, yf, yc, ya, ysf, ysb, z, ssm_d, w_glu, w_branch, w_out, ln_g, ln_b, l):
    t = x.shape[0]
    rf = lambda i: i
    row = lambda w: pl.BlockSpec((ROW_TILE, w), lambda i: (i, 0))
    full = lambda shape: pl.BlockSpec((None,) + shape, lambda i: (l,) + (0,) * len(shape))
    return pl.pallas_call(
        _merge_kernel,
        grid=(t // ROW_TILE,),
        in_specs=[
            row(D_MODEL), pl.BlockSpec((None, 1, 6, D_MODEL), lambda i: (l, mod_row(i), 0, 0)),
            row(512), row(512), row(512), row(512), row(512),
            _zspec(ROW_TILE, 512, rf, ZS_BLK),
            _zspec(ROW_TILE, 1024, rf, 0), _zspec(ROW_TILE, 1024, rf, 1),
            _zspec(ROW_TILE, 1024, rf, 2), _zspec(ROW_TILE, 1024, rf, 3),
            full((1, 512)), full((512, 512)), full((N_BRANCH, 512, D_MODEL)), full((D_MODEL, D_MODEL)),
            full((1, D_MODEL)), full((1, D_MODEL)),
        ],
        out_specs=[row(D_MODEL), row(D_MODEL)],
        out_shape=[jax.ShapeDtypeStruct((t, D_MODEL), F32), jax.ShapeDtypeStruct((t, D_MODEL), BF16)],
        compiler_params=_params(("parallel",)),
    )(x, mods, yf, yc, ya, ysf, ysb, z, z, z, z, z, ssm_d, w_glu, w_branch, w_out, ln_g, ln_b)


def _top16(s):
    n, w = s.shape
    iota = lax.broadcasted_iota(jnp.int32, (n, w), 0).astype(F32)
    kio = lax.broadcasted_iota(jnp.int32, (PEER_TOPK, w), 0)

    def body(k, carry):
        work, rank, vals, _ = carry
        m = jnp.max(work, axis=0, keepdims=True)
        pos = jnp.min(jnp.where(work == m, iota, float(n)), axis=0, keepdims=True)
        hit = iota == pos
        rank = jnp.where(hit, lax.convert_element_type(k, F32), rank)
        work = jnp.where(hit, -jnp.inf, work)
        vals = jnp.where(kio == k, m, vals)
        return work, rank, vals, pos

    init = (s, jnp.full((n, w), 1e9, F32), jnp.zeros((PEER_TOPK, w), F32), jnp.zeros((1, w), F32))
    _, rank, vals, pos = lax.fori_loop(0, PEER_TOPK, body, init)
    return vals, rank, pos


def _bitonic_desc(xs, first_k):
    n = len(xs)
    k = first_k
    while k <= n:
        j = k // 2
        while j >= 1:
            for i in range(n):
                p = i ^ j
                if p > i:
                    hi, lo = jnp.maximum(xs[i], xs[p]), jnp.minimum(xs[i], xs[p])
                    xs[i], xs[p] = (hi, lo) if (i & k) == 0 else (lo, hi)
            j //= 2
        k *= 2
    return xs


def _sorted_top16(s):
    n = s.shape[0] // 8
    xs = _bitonic_desc([s[8 * a:8 * a + 8, :] for a in range(n)], 2)
    shift = 4
    while len(xs) < PEER_TOPK:
        ys = [pltpu.roll(x, shift, 0) for x in xs]
        xs = _bitonic_desc(xs + ys[::-1], 2 * len(xs))
        shift //= 2
    n = len(xs)
    while shift >= 1:
        ys = [pltpu.roll(x, shift, 0) for x in xs]
        xs = _bitonic_desc([jnp.maximum(xs[i], ys[n - 1 - i]) for i in range(n)], n)
        shift //= 2
    return jnp.concatenate([x[0:1, :] for x in xs], axis=0)


_STAIR = [(j, PEER_TOPK // (j + 1)) for j in range(PEER_TOPK)]
_STAIR_ROWS = 64


def _stair_candidates(v1, v2):
    w = v1.shape[1]
    rows = [v1[j:j + 1] + v2[0:k] for j, k in _STAIR]
    npad = _STAIR_ROWS - sum(k for _, k in _STAIR)
    return jnp.concatenate(rows + [jnp.full((npad, w), -jnp.inf, F32)], axis=0)


def _stair_positions(w):
    rows = [float(PEER_TOPK * j) + lax.broadcasted_iota(jnp.int32, (k, w), 0).astype(F32) for j, k in _STAIR]
    npad = _STAIR_ROWS - sum(k for _, k in _STAIR)
    return jnp.concatenate(rows + [jnp.full((npad, w), 1e9, F32)], axis=0)


def _next_up(x):
    b = lax.bitcast_convert_type(x, jnp.int32)
    up = jnp.where(x > 0.0, b + 1, jnp.where(x < 0.0, b - 1, jnp.int32(0x00800000)))
    return lax.bitcast_convert_type(up, F32)


def _route_kernel(xm_ref, wq_ref, keys_ref, s1m_ref, qrow_ref, e1_ref, s2m_ref, pb_ref, e2_ref,
                  thr_ref, qs):
    qs[...] = lax.dot_general(wq_ref[...], xm_ref[...], (((1,), (1,)), ((), ())),
                              preferred_element_type=F32)
    w = xm_ref.shape[0]

    def count(mask):
        return jnp.sum(mask.astype(F32), axis=0, keepdims=True)

    def emit(h, s1, s2, in1, in2, m1, m2, vc, qrow, pb, thr_up, bad, thr_low):
        z = jnp.sum(jnp.exp(vc - vc[0:1]), axis=0, keepdims=True)
        s1m_ref[h] = jnp.where(in1, s1, -jnp.inf)
        s2m_ref[h] = jnp.where(in2, s2, -jnp.inf)
        e1_ref[h] = jnp.where(in1, jnp.exp(s1 - m1), 0.0) / z
        e2_ref[h] = jnp.where(in2, jnp.exp(s2 - m2), 0.0)
        qrow_ref[h] = qrow
        pb_ref[h] = pb
        thr = vc[PEER_TOPK - 1:PEER_TOPK]
        thr_ref[h] = jnp.concatenate([thr, thr_up, bad, thr_low, jnp.zeros((4, w), F32)], axis=0)

    def head(h, carry):
        base = pl.multiple_of(h * KEY_DIM, KEY_DIM)
        q1 = qs[pl.ds(base, N_KEYS), :].astype(BF16)
        q2 = qs[pl.ds(base + N_KEYS, N_KEYS), :].astype(BF16)
        s1 = jnp.dot(keys_ref[2 * h], q1, preferred_element_type=F32)
        s2 = jnp.dot(keys_ref[2 * h + 1], q2, preferred_element_type=F32)

        v1 = _sorted_top16(s1)
        v2 = _sorted_top16(s2)
        in1 = s1 >= v1[PEER_TOPK - 1:PEER_TOPK]
        in2 = s2 >= v2[PEER_TOPK - 1:PEER_TOPK]
        cand = _stair_candidates(v1, v2)
        vc = _sorted_top16(cand)
        thr = vc[PEER_TOPK - 1:PEER_TOPK]
        zero = jnp.zeros((N_KEYS, w), F32)
        k = float(PEER_TOPK)
        bad = jnp.abs(count(in1) - k) + jnp.abs(count(in2) - k) + jnp.abs(count(cand >= thr) - k)
        top = lambda v: jnp.maximum(jnp.abs(v[0:1]), jnp.abs(v[PEER_TOPK - 1:PEER_TOPK]))
        delta = (top(v1) + top(v2)) * (2.0 ** -21)
        lo = thr - 2.0 * delta
        flag = bad
        for jrow in range(PEER_TOPK):
            csum = v1[jrow:jrow + 1] + v2
            flag = flag + count((csum < thr) & (csum >= lo))
        emit(h, s1, s2, in1, in2, v1[0:1], v2[0:1], vc, zero, zero, thr, flag, thr - delta)

        @pl.when(jnp.max(bad) > 0.0)
        def _():
            xv1, r1, _ = _top16(s1)
            xv2, r2, _ = _top16(s2)
            xcand = _stair_candidates(xv1, xv2)
            xvc, _, prow = _top16(xcand)
            riota = lax.broadcasted_iota(jnp.int32, xcand.shape, 0).astype(F32)
            pthr = jnp.sum(jnp.where(riota == prow, _stair_positions(w), 0.0), axis=0, keepdims=True)
            emit(h, s1, s2, r1 < 100.0, r2 < 100.0, xv1[0:1], xv2[0:1], xvc,
                 pthr - k * r1, r2, _next_up(xvc[PEER_TOPK - 1:PEER_TOPK]), flag, thr - delta)

        return carry

    lax.fori_loop(0, PEER_HEADS, head, 0)


def _route(xm, wq_t, keys, l):
    t = xm.shape[0]
    big = jax.ShapeDtypeStruct((PEER_HEADS, N_KEYS, t), F32)
    bspec = pl.BlockSpec((PEER_HEADS, N_KEYS, ROUTE_TT), lambda i: (0, 0, i))
    return pl.pallas_call(
        _route_kernel,
        grid=(t // ROUTE_TT,),
        in_specs=[
            pl.BlockSpec((ROUTE_TT, D_MODEL), lambda i: (i, 0)),
            pl.BlockSpec((None, PEER_HEADS * KEY_DIM, D_MODEL), lambda i: (l, 0, 0)),
            pl.BlockSpec((None, 2 * PEER_HEADS, N_KEYS, N_KEYS), lambda i: (l, 0, 0, 0)),
        ],
        out_specs=[bspec] * 6 + [pl.BlockSpec((PEER_HEADS, 8, ROUTE_TT), lambda i: (0, 0, i))],
        out_shape=[big] * 6 + [jax.ShapeDtypeStruct((PEER_HEADS, 8, t), F32)],
        scratch_shapes=[pltpu.VMEM((PEER_HEADS * KEY_DIM, ROUTE_TT), F32)],
        compiler_params=_params(("parallel",)),
    )(xm, wq_t, keys)


def _peer_kernel(flag_ref, xm_ref, u_ref, vt_ref, s1m_ref, qrow_ref, e1_ref, s2m_ref, pb_ref, e2_ref,
                 thr_ref, x1_ref, mod_ref, g_ref, b_ref, o_ref, ht, wacc, pt, acc):
    j = pl.program_id(1)
    nrow = PEER_EB // N_KEYS

    @pl.when(j == 0)
    def _():
        acc[...] = jnp.zeros_like(acc)

    rpass = 4
    nlg = PEER_TT // 128

    def rows_of(tile, r0):
        return jnp.stack([jnp.broadcast_to(tile[r:r + 1, :], (8, 128)) for r in range(r0, r0 + rpass)])

    def gate_work(hp, lg, exact):
        lanes = slice(lg * 128, (lg + 1) * 128)
        heads = (2 * hp, 2 * hp + 1)
        for rp in range(0, nrow, rpass):
            e1r = [rows_of(e1_ref[h, :, lanes], rp) for h in heads]
            if exact:
                s1r = [rows_of(s1m_ref[h, :, lanes], rp) for h in heads]
                qr = [rows_of(qrow_ref[h, :, lanes], rp) for h in heads]
            else:
                need = [thr_ref[h, 3:4, lanes] - rows_of(s1m_ref[h, :, lanes], rp) for h in heads]
            for v in range(N_KEYS // 8):
                sub = slice(v * 8, (v + 1) * 8)
                gate = None
                for i, h in enumerate(heads):
                    if exact:
                        first = pb_ref[h, sub, lanes][None] <= qr[i]
                        limit = jnp.where(first, thr_ref[h, 0:1, lanes], thr_ref[h, 1:2, lanes])
                        sel = s1r[i] + s2m_ref[h, sub, lanes][None] >= limit
                    else:
                        sel = s2m_ref[h, sub, lanes][None] >= need[i]
                    g = jnp.where(sel, e1r[i] * e2_ref[h, sub, lanes][None], 0.0)
                    gate = g if gate is None else gate + g
                if hp == 0:
                    wacc[rp:rp + rpass, sub, lanes] = gate
                else:
                    wacc[rp:rp + rpass, sub, lanes] += gate

    hfull = lax.dot_general(u_ref[...], xm_ref[...], (((1,), (1,)), ((), ())),
                            preferred_element_type=F32)
    for r in range(nrow):
        ht[r] = hfull[r * N_KEYS:(r + 1) * N_KEYS, :]
    for hp in range(PEER_HEADS // 2):
        for lg in range(nlg):
            row = pl.program_id(0) * nlg + lg
            needs_sum = (flag_ref[row, 2 * hp] + flag_ref[row, 2 * hp + 1]) > 0
            pl.when(needs_sum)(functools.partial(gate_work, hp, lg, True))
            pl.when(jnp.logical_not(needs_sum))(functools.partial(gate_work, hp, lg, False))
    for lg in range(PEER_TT // 128):
        lanes = slice(lg * 128, (lg + 1) * 128)
        for r in range(nrow):
            pt[r * N_KEYS:(r + 1) * N_KEYS, lanes] = (
                wacc[r, :, lanes] * _gelu(ht[r, :, lanes])).astype(BF16)
    acc[...] += jnp.dot(vt_ref[...], pt[...], preferred_element_type=F32)

    @pl.when(j == pl.num_programs(1) - 1)
    def _():
        alpha = (2 * 4) ** 0.25
        ff = acc[...].T
        o_ref[...] = _layer_norm(alpha * x1_ref[...] + mod_ref[0, 5:6, :] * ff, g_ref[...], b_ref[...])


def _peer(xm, u_b, vt_b, routing, x1, mods, mod_row_tt, ln_g, ln_b, l):
    t = xm.shape[0]
    once = pl.Buffered(1)
    tok = lambda w: pl.BlockSpec((PEER_TT, w), lambda i, j: (i, 0), pipeline_mode=once)
    rspec = pl.BlockSpec((PEER_HEADS, N_KEYS, PEER_TT), lambda i, j: (0, 0, i), pipeline_mode=once)
    rowspec = pl.BlockSpec((PEER_HEADS, PEER_EB // N_KEYS, PEER_TT), lambda i, j: (0, j, i))
    full = lambda shape: pl.BlockSpec((None,) + shape, lambda i, j: (l,) + (0,) * len(shape))
    nblk = N_EXPERTS // PEER_EB
    bad = routing[6][:, 2, :].reshape(PEER_HEADS, t // 128, 128)
    flags = (jnp.max(bad, axis=2) > 0.0).astype(jnp.int32).T
    return pl.pallas_call(
        _peer_kernel,
        grid=(t // PEER_TT, nblk),
        in_specs=[
            pl.BlockSpec(memory_space=pltpu.SMEM),
            tok(D_MODEL),
            pl.BlockSpec((None, PEER_EB, D_MODEL), lambda i, j: (l, j, 0)),
            pl.BlockSpec((None, D_MODEL, PEER_EB), lambda i, j: (l, 0, j)),
            rowspec, rowspec, rowspec, rspec, rspec, rspec,
            pl.BlockSpec((PEER_HEADS, 8, PEER_TT), lambda i, j: (0, 0, i)),
            tok(D_MODEL),
            pl.BlockSpec((None, 1, 6, D_MODEL), lambda i, j: (l, mod_row_tt(i), 0, 0)),
            full((1, D_MODEL)), full((1, D_MODEL)),
        ],
        out_specs=pl.BlockSpec((PEER_TT, D_MODEL), lambda i, j: (i, 0)),
        out_shape=jax.ShapeDtypeStruct((t, D_MODEL), F32),
        scratch_shapes=[pltpu.VMEM((PEER_EB // N_KEYS, N_KEYS, PEER_TT), F32),
                        pltpu.VMEM((PEER_EB // N_KEYS, N_KEYS, PEER_TT), F32),
                        pltpu.VMEM((PEER_EB, PEER_TT), BF16),
                        pltpu.VMEM((D_MODEL, PEER_TT), F32)],
        compiler_params=_params(("parallel", "arbitrary")),
    )(flags, xm, u_b, vt_b, *routing, x1, mods, ln_g, ln_b)


def _dft_tables(length):
    n = np.arange(length)
    ang = 2.0 * np.pi * ((n[:, None] * n[None, :]) % length) / length
    dl = np.concatenate([np.cos(ang), -np.sin(ang)], axis=1) / math.sqrt(length)
    c = np.arange(FFT_GROUP_CH)
    angc = 2.0 * np.pi * ((c[:, None] * c[None, :]) % FFT_GROUP_CH) / FFT_GROUP_CH
    eye = np.eye(FFT_GROUPS)
    dc = np.concatenate([np.kron(eye, np.cos(angc)), np.kron(eye, np.sin(angc))], axis=1)
    dc = dc / math.sqrt(FFT_GROUP_CH)
    return jnp.asarray(dl, BF16), jnp.asarray(dc, BF16)


def _rope_tables(length, nheads):
    t = np.arange(length)
    pos = np.stack([t // GRID_W, t % GRID_W], axis=1).astype(np.float32)
    n_freq = HEAD_DIM // 4
    inv = (1.0 / (ROPE_BASE ** (np.arange(n_freq, dtype=np.float32) / n_freq))).astype(np.float32)
    ang = pos[:, :, None] * inv[None, None, :]
    cos = np.repeat(np.cos(ang)[:, :, None, :], 2, axis=2).reshape(length, HEAD_DIM)
    sin = np.sin(ang)
    sin = np.stack([-sin, sin], axis=2).reshape(length, HEAD_DIM)
    return (jnp.asarray(np.tile(cos, (1, nheads)), F32), jnp.asarray(np.tile(sin, (1, nheads)), F32))


def kernel(x_prompt, x_sample, cache_k, cache_v, state_ssm_re, state_ssm_im, c, c_ctx, w_ada, b_ada, w_in, conv_w, ssm_lam_re, ssm_lam_im, ssm_log_step, ssm_b_re, ssm_b_im, ssm_c_re, ssm_c_im, ssm_d, ssm_w_glu, attn_sink, w_branch, w_out, ln1_g, ln1_b, ln2_g, ln2_b, peer_wq, peer_subkeys, peer_u, peer_v):
    nb, seq, _ = x_prompt.shape
    nd, lseq, _ = x_sample.shape
    depth = w_in.shape[0]
    t_ctx = nb * seq
    t_all = t_ctx + nd * lseq
    assert t_ctx % lseq == 0 and t_all % PEER_TT == 0 and (2 * seq) % PEER_TT == 0
    assert lseq % SSM_TCHUNK == 0 and seq == SSM_TCHUNK

    x = jnp.concatenate([x_prompt.reshape(t_ctx, D_MODEL), x_sample.reshape(nd * lseq, D_MODEL)], axis=0)

    nrow = -(-(1 + nd) // 8) * 8
    cvecs = jnp.concatenate([c_ctx[None, :], c, jnp.zeros((nrow - 1 - nd, D_MODEL), F32)], axis=0)
    mods_all = _modulation(cvecs, w_ada, b_ada).reshape(depth, nrow, 6, D_MODEL)

    def mod_row_for(tile):
        nctx = t_ctx // tile
        per = lseq // tile
        return lambda i: jnp.where(i < nctx, 0, 1 + (i - nctx) // per)

    mod_row = mod_row_for(ROW_TILE)
    mod_row_tt = mod_row_for(PEER_TT)

    dl_ctx, dft_c = _dft_tables(seq)
    dl_lat, _ = _dft_tables(lseq)
    cosq, sinq = _rope_tables(lseq, N_HEADS)
    cosk, sin_k = _rope_tables(lseq, N_KV)

    gate0 = sum((512,) * 6) + 2 * N_KV * HEAD_DIM
    w_in_b = jnp.concatenate([w_in[:, :, gate0:], w_in[:, :, :gate0]], axis=2).astype(BF16)
    w_glu_b = ssm_w_glu.astype(BF16)
    w_branch_b = w_branch.astype(BF16)
    w_out_b = w_out.astype(BF16)
    wq_t = peer_wq.transpose(0, 2, 1).astype(BF16)
    keys = peer_subkeys.reshape(depth, 2 * PEER_HEADS, N_KEYS, KEY_DIM // 2).astype(BF16)
    u_b = peer_u.astype(BF16)
    vt_b = peer_v.transpose(0, 2, 1).astype(BF16)
    sp = _s5_params(ssm_lam_re, ssm_lam_im, ssm_log_step, ssm_b_re, ssm_b_im, ssm_c_re, ssm_c_im)
    sink = attn_sink.reshape(depth, 1, N_HEADS)
    ssm_d3 = ssm_d.reshape(depth, 1, -1)
    ln1_g3, ln1_b3 = ln1_g.reshape(depth, 1, -1), ln1_b.reshape(depth, 1, -1)
    ln2_g3, ln2_b3 = ln2_g.reshape(depth, 1, -1), ln2_b.reshape(depth, 1, -1)
    ck = cache_k.reshape(nd, depth, -1, N_KV * HEAD_DIM)
    cv = cache_v.reshape(nd, depth, -1, N_KV * HEAD_DIM)
    nbp_c = -(-nb // SSM_BROWS) * SSM_BROWS
    nbp_l = -(-nd // SSM_BROWS) * SSM_BROWS
    zero_state = jnp.zeros((SSM_GROUPS // SSM_GBLK, nbp_c, 2 * SSM_GBLK * SSM_STATE), F32)

    new_k, new_v, new_re, new_im = [], [], [], []
    for l in range(depth):
        z = _in_proj(x, mods_all, w_in_b, mod_row_for(IN_TILE), l)

        yf, yc, ya = _mixer_ctx(z, nb, seq, conv_w, sink, dl_ctx, dft_c, l)
        yf, yc = _fftconv_lat(z, t_ctx, nd, lseq, conv_w, dl_lat, dft_c, yf, yc, l)
        ya = _attn_lat(z, t_ctx, nd, lseq, ck, cv, cosq, sinq, cosk, sin_k, sink, ya, l)

        zs = z[:, ZS_BLK * 512:(ZS_BLK + 1) * 512]
        u_c, _ = _to_time_major(zs[:t_ctx], nb, seq)
        ysf_c, ysb_c, hf_c, hb_c = _s5(u_c, zero_state, zero_state, sp, nbp_c // SSM_BROWS,
                                       seq // SSM_TCHUNK, l)
        u_l, _ = _to_time_major(zs[t_ctx:], nd, lseq)
        h0f = _state_to_blocks(state_ssm_re[:, l, 0], state_ssm_im[:, l, 0], nbp_l)
        h0b = _state_to_blocks(state_ssm_re[:, l, 1], state_ssm_im[:, l, 1], nbp_l)
        ysf_l, ysb_l, _, _ = _s5(u_l, h0f, h0b, sp, nbp_l // SSM_BROWS, lseq // SSM_TCHUNK, l)
        ysf = jnp.concatenate([_from_time_major(ysf_c, nb, nbp_c, seq), _from_time_major(ysf_l, nd, nbp_l, lseq)], axis=0)
        ysb = jnp.concatenate([_from_time_major(ysb_c, nb, nbp_c, seq), _from_time_major(ysb_l, nd, nbp_l, lseq)], axis=0)

        x1, xm2 = _merge(x, mods_all, mod_row, yf, yc, ya, ysf, ysb, z,
                         ssm_d3, w_glu_b, w_branch_b, w_out_b, ln1_g3, ln1_b3, l)

        routing = _route(xm2, wq_t, keys, l)
        x = _peer(xm2, u_b, vt_b, routing, x1, mods_all, mod_row_tt, ln2_g3, ln2_b3, l)

        kv = z[:t_ctx, ZK_BLK * 128:(ZV_BLK + 1) * 128].reshape(nb, seq, 2, N_KV, HEAD_DIM)
        new_k.append(kv[:, :, 0])
        new_v.append(kv[:, :, 1])
        fre, fim = _blocks_to_state(hf_c, nb)
        bre, bim = _blocks_to_state(hb_c, nb)
        new_re.append(jnp.stack([fre, bre], axis=1))
        new_im.append(jnp.stack([fim, bim], axis=1))

    return (x[:t_ctx].reshape(nb, seq, D_MODEL), x[t_ctx:].reshape(nd, lseq, D_MODEL),
            jnp.stack(new_k, axis=1), jnp.stack(new_v, axis=1),
            jnp.stack(new_re, axis=1), jnp.stack(new_im, axis=1))
```
